```python
import math
import jax, jax.numpy as jnp
from jax import lax
import numpy as np

D_MODEL = 2048
BATCH = 2
SEQ = 8192
DEPTH = 1

N_META = 16
ATTN_HEADS = 8
ATTN_HEAD_DIM = 128
D_ATTN = ATTN_HEADS * ATTN_HEAD_DIM
D_SSM = D_MODEL // 2
SSM_GROUP = 16
SSM_GROUPS = D_SSM // SSM_GROUP
SSM_STATE = 64
D_FF = 5632
CONV_WIDTH = 3
Q_BLOCK = 128
EPS = 1e-6
IN_SPLITS = [D_ATTN, D_ATTN, D_ATTN, ATTN_HEADS, D_SSM, D_MODEL, D_MODEL]
N_IN = sum(IN_SPLITS)
IN_OFFSETS = [int(o) for o in np.cumsum(IN_SPLITS)[:-1]]

kernel_name = "hybrid_s5_forgetting_attn_convffn"


def rmsnorm(x, g):
    xf = x.astype(jnp.float32)
    y = xf * lax.rsqrt(jnp.mean(xf * xf, axis=-1, keepdims=True) + EPS)
    return (y * g.astype(jnp.float32)).astype(x.dtype)


def _fox_block(qb, Fq, qpos, k, v, Fk, kpos):
    s = jnp.einsum('bqhd,bkhd->bhqk', qb, k, preferred_element_type=jnp.float32) * (ATTN_HEAD_DIM ** -0.5)
    s = s + jnp.transpose(Fq, (0, 2, 1))[..., None] - jnp.transpose(Fk, (0, 2, 1))[:, :, None, :]
    mask = kpos[None, :] <= qpos[:, None]
    s = jnp.where(mask[None, None], s, -jnp.inf)
    p = jax.nn.softmax(s, axis=-1)
    return jnp.einsum('bhqk,bkhd->bqhd', p.astype(v.dtype), v)


def forgetting_attention(q, k, v, log_f):
    b, L, H, hd = q.shape
    F = jnp.cumsum(log_f, axis=1)
    pos = jnp.arange(L)
    out_meta = _fox_block(q[:, :N_META], F[:, :N_META], pos[:N_META],
                          k[:, :N_META], v[:, :N_META], F[:, :N_META], pos[:N_META])
    n_blk = (L - N_META) // Q_BLOCK
    qr = q[:, N_META:].reshape(b, n_blk, Q_BLOCK, H, hd).transpose(1, 0, 2, 3, 4)
    Fr = F[:, N_META:].reshape(b, n_blk, Q_BLOCK, H).transpose(1, 0, 2, 3)
    qpos = (N_META + jnp.arange(L - N_META)).reshape(n_blk, Q_BLOCK)
    out_r = lax.map(lambda a: _fox_block(a[0], a[1], a[2], k, v, F, pos), (qr, Fr, qpos))
    out_r = out_r.transpose(1, 0, 2, 3, 4).reshape(b, L - N_META, H, hd)
    return jnp.concatenate([out_meta, out_r], axis=1)


def s5_ssm(u, lam_re, lam_im, log_dt, b_re, b_im, c_re, c_im, d_skip):
    bsz, L, _ = u.shape
    f32 = jnp.float32
    uf = u.astype(f32).reshape(bsz, L, SSM_GROUPS, SSM_GROUP)
    dt = jnp.exp(log_dt.astype(f32))[:, None]
    lr = lam_re.astype(f32)
    li = lam_im.astype(f32)
    mag = jnp.exp(lr * dt)
    a_re = mag * jnp.cos(li * dt)
    a_im = mag * jnp.sin(li * dt)
    den = lr * lr + li * li
    nr = a_re - 1.0
    z_re = (nr * lr + a_im * li) / den
    z_im = (a_im * lr - nr * li) / den
    br = b_re.astype(f32)
    bi = b_im.astype(f32)
    bb_re = z_re[..., None] * br - z_im[..., None] * bi
    bb_im = z_re[..., None] * bi + z_im[..., None] * br
    bu_re = jnp.einsum('gpc,blgc->blgp', bb_re, uf)
    bu_im = jnp.einsum('gpc,blgc->blgp', bb_im, uf)
    at_re = jnp.broadcast_to(a_re, (1, L, SSM_GROUPS, SSM_STATE))
    at_im = jnp.broadcast_to(a_im, (1, L, SSM_GROUPS, SSM_STATE))

    def combine(e1, e2):
        ar1, ai1, br1, bi1 = e1
        ar2, ai2, br2, bi2 = e2
        return (ar2 * ar1 - ai2 * ai1,
                ar2 * ai1 + ai2 * ar1,
                ar2 * br1 - ai2 * bi1 + br2,
                ar2 * bi1 + ai2 * br1 + bi2)

    _, _, h_re, h_im = lax.associative_scan(combine, (at_re, at_im, bu_re, bu_im), axis=1)
    y = (jnp.einsum('gcp,blgp->blgc', c_re.astype(f32), h_re)
         - jnp.einsum('gcp,blgp->blgc', c_im.astype(f32), h_im))
    y = y.reshape(bsz, L, D_SSM) + d_skip.astype(f32) * u.astype(f32)
    return y.astype(u.dtype)


def conv_ffn(x, w_up, conv_w, conv_b, w_down):
    gu = x @ w_up
    g, u = jnp.split(gu, 2, axis=-1)
    L = g.shape[1]
    gp = jnp.pad(g, ((0, 0), (CONV_WIDTH - 1, 0), (0, 0)))
    gc = conv_b + conv_w[0] * gp[:, 0:L]
    for j in range(1, CONV_WIDTH):
        gc = gc + conv_w[j] * gp[:, j:j + L]
    return (jax.nn.silu(gc) * u) @ w_down


def mixer(n, w_in, b_f, lam_re, lam_im, log_dt, b_re, b_im, c_re, c_im, d_skip, w_glu, w_attn_o, w_out):
    bsz, L, _ = n.shape
    z = n @ w_in
    q, k, v, f, u, ga, gb = jnp.split(z, IN_OFFSETS, axis=-1)
    q = q.reshape(bsz, L, ATTN_HEADS, ATTN_HEAD_DIM)
    k = k.reshape(bsz, L, ATTN_HEADS, ATTN_HEAD_DIM)
    v = v.reshape(bsz, L, ATTN_HEADS, ATTN_HEAD_DIM)
    log_f = jax.nn.log_sigmoid(f.astype(jnp.float32) + b_f.astype(jnp.float32))
    attn = forgetting_attention(q, k, v, log_f).reshape(bsz, L, D_ATTN) @ w_attn_o
    y = s5_ssm(u, lam_re, lam_im, log_dt, b_re, b_im, c_re, c_im, d_skip)
    ya, yb = jnp.split(jax.nn.gelu(y) @ w_glu, 2, axis=-1)
    ssm_out = ya * jax.nn.sigmoid(yb)
    merged = jax.nn.sigmoid(ga) * ssm_out + jax.nn.sigmoid(gb) * attn
    return merged @ w_out


def setup_inputs(seed: int = 0) -> dict:
    key = jax.random.key(seed)
    ks = jax.random.split(key, 24)
    nrm = lambda k, s, sc: jax.random.normal(k, s, jnp.float32) * sc
    Dp = DEPTH
    n_idx = jnp.arange(SSM_STATE, dtype=jnp.float32)
    return {
        "x": nrm(ks[0], (BATCH, SEQ, D_MODEL), 1.0),
        "meta": nrm(ks[1], (N_META, D_MODEL), 1.0),
        "g_mix": 1.0 + nrm(ks[2], (Dp, D_MODEL), 0.01),
        "w_in": nrm(ks[3], (Dp, D_MODEL, N_IN), D_MODEL ** -0.5),
        "b_f": jax.random.uniform(ks[4], (Dp, ATTN_HEADS), jnp.float32, 1.0, 6.0),
        "lam_re": -0.5 + nrm(ks[5], (Dp, SSM_GROUPS, SSM_STATE), 0.01),
        "lam_im": math.pi * n_idx + nrm(ks[6], (Dp, SSM_GROUPS, SSM_STATE), 0.01),
        "log_dt": jax.random.uniform(ks[7], (Dp, SSM_GROUPS), jnp.float32, math.log(1e-3), math.log(1e-1)),
        "b_re": nrm(ks[8], (Dp, SSM_GROUPS, SSM_STATE, SSM_GROUP), (2 * SSM_GROUP) ** -0.5),
        "b_im": nrm(ks[9], (Dp, SSM_GROUPS, SSM_STATE, SSM_GROUP), (2 * SSM_GROUP) ** -0.5),
        "c_re": nrm(ks[10], (Dp, SSM_GROUPS, SSM_GROUP, SSM_STATE), (2 * SSM_STATE) ** -0.5),
        "c_im": nrm(ks[11], (Dp, SSM_GROUPS, SSM_GROUP, SSM_STATE), (2 * SSM_STATE) ** -0.5),
        "d_skip": nrm(ks[12], (Dp, D_SSM), 1.0),
        "w_glu": nrm(ks[13], (Dp, D_SSM, 2 * D_MODEL), D_SSM ** -0.5),
        "w_attn_o": nrm(ks[14], (Dp, D_ATTN, D_MODEL), D_ATTN ** -0.5),
        "w_out": nrm(ks[15], (Dp, D_MODEL, D_MODEL), D_MODEL ** -0.5),
        "g_ffn": 1.0 + nrm(ks[16], (Dp, D_MODEL), 0.01),
        "w_up": nrm(ks[17], (Dp, D_MODEL, 2 * D_FF), D_MODEL ** -0.5),
        "conv_w": nrm(ks[18], (Dp, CONV_WIDTH, D_FF), CONV_WIDTH ** -0.5),
        "conv_b": nrm(ks[19], (Dp, D_FF), 0.01),
        "w_down": nrm(ks[20], (Dp, D_FF, D_MODEL), D_FF ** -0.5),
        "g_final": 1.0 + nrm(ks[21], (D_MODEL,), 0.01),
    }


def reference(x, meta, g_mix, w_in, b_f, lam_re, lam_im, log_dt, b_re, b_im, c_re, c_im, d_skip,
              w_glu, w_attn_o, w_out, g_ffn, w_up, conv_w, conv_b, w_down, g_final):
    bsz = x.shape[0]
    m = jnp.broadcast_to(meta.astype(x.dtype)[None], (bsz, N_META, D_MODEL))
    h = jnp.concatenate([m, x], axis=1)
    for l in range(DEPTH):
        n = rmsnorm(h, g_mix[l])
        h = h + mixer(n, w_in[l], b_f[l], lam_re[l], lam_im[l], log_dt[l], b_re[l], b_im[l],
                      c_re[l], c_im[l], d_skip[l], w_glu[l], w_attn_o[l], w_out[l])
        n2 = rmsnorm(h, g_ffn[l])
        h = h + conv_ffn(n2, w_up[l], conv_w[l], conv_b[l], w_down[l])
    return rmsnorm(h, g_final)[:, N_META:]
```

```python
import functools
import math

import jax
import jax.numpy as jnp
from jax import lax
from jax.experimental import pallas as pl
from jax.experimental.pallas import tpu as pltpu

N_META = 16
HEADS = 8
HEAD_DIM = 128
D_ATTN = HEADS * HEAD_DIM
SSM_GROUP = 16
SSM_STATE = 64
SSM_CHUNK = 16
CONV_WIDTH = 3
EPS = 1e-6

ROW_ALIGN = 256
PAD = ROW_ALIGN - N_META
ATT_TILE = 256
QK_AUG = 2 * HEAD_DIM
MASK_BIG = 1e30
NEG = -3e38
VMEM_LIMIT = 56 * 1024 * 1024

F32 = jnp.float32
BF16 = jnp.bfloat16


def _sigmoid(x):
    return 1.0 / (1.0 + jnp.exp(-x))


def _gelu_tanh(x):
    c = math.sqrt(2.0 / math.pi)
    return 0.5 * x * (1.0 + jnp.tanh(c * (x + 0.044715 * (x * x * x))))


def _params(sem, limit=VMEM_LIMIT):
    return pltpu.CompilerParams(dimension_semantics=sem, vmem_limit_bytes=limit)


def _resident(shape, index_map):
    return pl.BlockSpec(shape, index_map, pipeline_mode=pl.Buffered(1))


def _inproj_kernel(x_ref, g_ref, w_ref, wf_ref, z_ref, f_ref, n_scr):
    @pl.when(pl.program_id(1) == 0)
    def _():
        x = x_ref[...]
        ms = jnp.mean(x * x, axis=-1, keepdims=True)
        n = (x * lax.rsqrt(ms + EPS) * g_ref[...]).astype(BF16)
        n_scr[...] = n
        f_ref[...] = jnp.dot(n, wf_ref[...], preferred_element_type=F32)

    z_ref[...] = jnp.dot(n_scr[...], w_ref[...], preferred_element_type=F32).astype(z_ref.dtype)


def _inproj(xp, g, w, wf, tm=512, tn=2048):
    rows, d = xp.shape
    n_out = w.shape[1]
    return pl.pallas_call(
        _inproj_kernel,
        grid=(rows // tm, n_out // tn),
        in_specs=[
            pl.BlockSpec((tm, d), lambda i, j: (i, 0)),
            pl.BlockSpec((1, d), lambda i, j: (0, 0)),
            pl.BlockSpec((d, tn), lambda i, j: (0, j)),
            pl.BlockSpec((d, 128), lambda i, j: (0, 0)),
        ],
        out_specs=[
            pl.BlockSpec((tm, tn), lambda i, j: (i, j)),
            pl.BlockSpec((tm, 128), lambda i, j: (i, 0)),
        ],
        out_shape=[
            jax.ShapeDtypeStruct((rows, n_out), BF16),
            jax.ShapeDtypeStruct((rows, 128), F32),
        ],
        scratch_shapes=[pltpu.VMEM((tm, d), BF16)],
        compiler_params=_params(("parallel", "arbitrary")),
        name="inproj",
    )(xp, g, w, wf)


def _split3(x):
    hi = x.astype(BF16)
    r1 = x - hi.astype(F32)
    mid = r1.astype(BF16)
    lo = (r1 - mid.astype(F32)).astype(BF16)
    return hi, mid, lo


def _prep_kernel(f_ref, bf_ref, q_ref, k_ref, qa_ref, ka_ref, carry):
    t = pl.program_id(1)
    tm = f_ref.shape[0]

    @pl.when(t == 0)
    def _():
        carry[...] = jnp.zeros_like(carry)

    x = f_ref[...] + bf_ref[...]
    logf = jnp.minimum(x, 0.0) - jnp.log1p(jnp.exp(-jnp.abs(x)))
    row = lax.broadcasted_iota(jnp.int32, (tm, 1), 0)
    valid = jnp.logical_or(t > 0, row >= PAD)
    logf = jnp.where(valid, logf, 0.0)

    hi, mid, lo = _split3(logf)
    r = lax.broadcasted_iota(jnp.int32, (tm, tm), 0)
    c = lax.broadcasted_iota(jnp.int32, (tm, tm), 1)
    tri = (r >= c).astype(BF16)
    cs = (jnp.dot(tri, hi, preferred_element_type=F32)
          + jnp.dot(tri, mid, preferred_element_type=F32)
          + jnp.dot(tri, lo, preferred_element_type=F32))
    fcum = cs + carry[0:1, :]
    carry[0:1, :] = fcum[tm - 1:tm, :]

    lane = lax.broadcasted_iota(jnp.int32, (tm, HEAD_DIM), 1)
    one = jnp.ones((tm, HEAD_DIM), F32)
    zero = jnp.zeros((tm, HEAD_DIM), F32)
    for h in range(HEADS):
        fh = fcum[:, h:h + 1]
        hi, mid, lo = _split3(fh)
        hi, mid, lo = hi.astype(F32), mid.astype(F32), lo.astype(F32)
        qb = jnp.where(lane == 0, hi, jnp.where(lane == 1, mid, jnp.where(lane == 2, lo,
             jnp.where(lane < 6, one, zero))))
        khi = jnp.where(valid, hi, MASK_BIG)
        kmid = jnp.where(valid, mid, 0.0)
        klo = jnp.where(valid, lo, 0.0)
        kb = jnp.where(lane < 3, one, jnp.where(lane == 3, -khi, jnp.where(lane == 4, -kmid,
             jnp.where(lane == 5, -klo, zero))))
        qa_ref[0, h, :, 0:HEAD_DIM] = q_ref[:, h * HEAD_DIM:(h + 1) * HEAD_DIM]
        qa_ref[0, h, :, HEAD_DIM:QK_AUG] = qb.astype(BF16)
        ka_ref[0, h, :, 0:HEAD_DIM] = k_ref[:, h * HEAD_DIM:(h + 1) * HEAD_DIM]
        ka_ref[0, h, :, HEAD_DIM:QK_AUG] = kb.astype(BF16)


def _prep(f, bf, z, bsz, lp, tm=ROW_ALIGN):
    nt = lp // tm
    shape = jax.ShapeDtypeStruct((bsz, HEADS, lp, QK_AUG), BF16)
    blk = pl.BlockSpec((1, HEADS, tm, QK_AUG), lambda b, t: (b, 0, t, 0))
    return pl.pallas_call(
        _prep_kernel,
        grid=(bsz, nt),
        in_specs=[
            pl.BlockSpec((tm, 128), lambda b, t: (b * nt + t, 0)),
            pl.BlockSpec((1, 128), lambda b, t: (0, 0)),
            pl.BlockSpec((tm, D_ATTN), lambda b, t: (b * nt + t, 0)),
            pl.BlockSpec((tm, D_ATTN), lambda b, t: (b * nt + t, 1)),
        ],
        out_specs=[blk, blk],
        out_shape=[shape, shape],
        scratch_shapes=[pltpu.VMEM((8, 128), F32)],
        compiler_params=_params(("parallel", "arbitrary")),
        name="attn_prep",
    )(f, bf, z, z)


def _flash_kernel(q_ref, k_ref, v_ref, o_ref):
    qi = pl.program_id(2)
    tq = q_ref.shape[2]
    tk = ATT_TILE
    q = q_ref[0, 0]

    def chunk(c, carry, masked):
        m, l, acc = carry
        off = pl.multiple_of(c * tk, tk)
        kc = k_ref[0, 0, pl.ds(off, tk), :]
        vc = v_ref[pl.ds(off, tk), :]
        s = lax.dot_general(kc, q, (((1,), (1,)), ((), ())), preferred_element_type=F32)
        if masked:
            kr = lax.broadcasted_iota(jnp.int32, (tk, tq), 0)
            qc = lax.broadcasted_iota(jnp.int32, (tk, tq), 1)
            s = jnp.where(kr > qc, NEG, s)
        m_new = jnp.maximum(m, jnp.max(s, axis=0, keepdims=True))
        alpha = jnp.exp(m - m_new)
        p = jnp.exp(s - m_new)
        l = alpha * l + jnp.sum(p, axis=0, keepdims=True)
        pv = lax.dot_general(vc, p.astype(BF16), (((0,), (0,)), ((), ())),
                             preferred_element_type=F32)
        return m_new, l, alpha * acc + pv

    init = (jnp.full((1, tq), NEG, F32), jnp.zeros((1, tq), F32), jnp.zeros((HEAD_DIM, tq), F32))
    carry = lax.fori_loop(0, qi, lambda c, cr: chunk(c, cr, False), init)
    _, l, acc = chunk(qi, carry, True)
    o_ref[...] = (acc / l).T.astype(o_ref.dtype)


def _flash(qa, ka, z, bsz, lp):
    nt = lp // ATT_TILE
    v_col0 = 2 * D_ATTN // HEAD_DIM
    return pl.pallas_call(
        _flash_kernel,
        grid=(bsz, HEADS, nt),
        in_specs=[
            pl.BlockSpec((1, 1, ATT_TILE, QK_AUG), lambda b, h, i: (b, h, i, 0)),
            pl.BlockSpec((1, 1, lp, QK_AUG), lambda b, h, i: (b, h, 0, 0)),
            pl.BlockSpec((lp, HEAD_DIM), lambda b, h, i: (b, v_col0 + h)),
        ],
        out_specs=pl.BlockSpec((ATT_TILE, HEAD_DIM), lambda b, h, i: (b * nt + i, h)),
        out_shape=jax.ShapeDtypeStruct((bsz * lp, D_ATTN), BF16),
        compiler_params=_params(("parallel", "parallel", "arbitrary")),
        name="flash",
    )(qa, ka, z)


def _ssm_weights_kernel(lr_ref, li_ref, ldt_ref, btr_ref, bti_ref, cr_ref, ci_ref,
                        w1_ref, e_ref, coef_ref):
    t = SSM_CHUNK
    p = SSM_STATE
    dt = jnp.exp(ldt_ref[0])
    lr = lr_ref[0]
    li = li_ref[0]
    d = lax.broadcasted_iota(jnp.int32, (2 * t, p), 0).astype(F32)
    mag = jnp.exp(d * (lr * dt))
    ang = d * (li * dt)
    pr = mag * jnp.cos(ang)
    pi = mag * jnp.sin(ang)
    a_re = pr[1:2]
    a_im = pi[1:2]
    den = lr * lr + li * li
    nr = a_re - 1.0
    z_re = (nr * lr + a_im * li) / den
    z_im = (a_im * lr - nr * li) / den
    azr = pr * z_re - pi * z_im
    azi = pr * z_im + pi * z_re

    btr = btr_ref[0]
    bti = bti_ref[0]
    cr = cr_ref[0]
    ci = ci_ref[0]

    caz_re, caz_im, ms, ms_sw, e = [], [], [], [], []
    for i in range(t):
        zr = azr[i:i + 1]
        zi = azi[i:i + 1]
        caz_re.append(cr * zr - ci * zi)
        caz_im.append(cr * zi + ci * zr)
        wr = azr[t - 1 - i:t - i]
        wi = azi[t - 1 - i:t - i]
        sr = btr * wr - bti * wi
        si = btr * wi + bti * wr
        ms.append(jnp.concatenate([sr, si], axis=1))
        ms_sw.append(jnp.concatenate([si, sr], axis=1))
        er = cr * pr[i + 1:i + 2] - ci * pi[i + 1:i + 2]
        ei = cr * pi[i + 1:i + 2] + ci * pr[i + 1:i + 2]
        e.append(jnp.concatenate([er, -ei], axis=1))
    caz_re = jnp.concatenate(caz_re, axis=0)
    caz_im = jnp.concatenate(caz_im, axis=0)
    nt_dims = (((1,), (1,)), ((), ()))
    r0 = (lax.dot_general(btr, caz_re, nt_dims, precision=lax.Precision.HIGHEST, preferred_element_type=F32)
          - lax.dot_general(bti, caz_im, nt_dims, precision=lax.Precision.HIGHEST, preferred_element_type=F32))
    width = t * SSM_GROUP
    lane = lax.broadcasted_iota(jnp.int32, (SSM_GROUP, width), 1)
    for i in range(t):
        blk = r0 if i == 0 else jnp.where(lane >= i * SSM_GROUP, pltpu.roll(r0, i * SSM_GROUP, 1), 0.0)
        w1_ref[0, i * SSM_GROUP:(i + 1) * SSM_GROUP, 0:width] = blk.astype(w1_ref.dtype)
    w1_ref[0, :, width:width + 2 * p] = jnp.concatenate(ms, axis=0).astype(w1_ref.dtype)
    w1_ref[0, :, width + 2 * p:width + 4 * p] = jnp.concatenate(ms_sw, axis=0).astype(w1_ref.dtype)
    e_ref[0] = jnp.concatenate(e, axis=0).astype(e_ref.dtype)
    ar = pr[t:t + 1]
    ai = pi[t:t + 1]
    coef_ref[0] = jnp.concatenate(
        [jnp.concatenate([ar, ar], axis=1), jnp.concatenate([-ai, ai], axis=1),
         jnp.zeros((6, 2 * p), F32)], axis=0)


def _ssm_weights(lam_re, lam_im, log_dt, b_re, b_im, c_re, c_im):
    g = lam_re.shape[0]
    t, p, c = SSM_CHUNK, SSM_STATE, SSM_GROUP
    vec = pl.BlockSpec((1, 1, p), lambda i: (i, 0, 0))
    mat = pl.BlockSpec((1, c, p), lambda i: (i, 0, 0))
    return pl.pallas_call(
        _ssm_weights_kernel,
        grid=(g,),
        in_specs=[vec, vec, pl.BlockSpec((1, 1, 1), lambda i: (i, 0, 0)), mat, mat, mat, mat],
        out_specs=[
            pl.BlockSpec((1, t * c, t * c + 4 * p), lambda i: (i, 0, 0)),
            pl.BlockSpec((1, t * c, 2 * p), lambda i: (i, 0, 0)),
            pl.BlockSpec((1, 8, 2 * p), lambda i: (i, 0, 0)),
        ],
        out_shape=[
            jax.ShapeDtypeStruct((g, t * c, t * c + 4 * p), BF16),
            jax.ShapeDtypeStruct((g, t * c, 2 * p), BF16),
            jax.ShapeDtypeStruct((g, 8, 2 * p), F32),
        ],
        compiler_params=_params(("parallel",)),
        name="ssm_weights",
    )(lam_re.reshape(g, 1, p), lam_im.reshape(g, 1, p), log_dt.reshape(g, 1, 1),
      jnp.swapaxes(b_re, 1, 2), jnp.swapaxes(b_im, 1, 2), c_re, c_im)


def _ssm_kernel(u_ref, w1_ref, e_ref, coef_ref, y_ref, yi_scr, s_scr, ssw_scr, hp_scr, *, bsz, nc):
    gb = u_ref.shape[0]
    width = SSM_CHUNK * SSM_GROUP
    sw = 2 * SSM_STATE
    for gi in range(gb):
        r = jnp.dot(u_ref[gi], w1_ref[gi], preferred_element_type=F32)
        yi_scr[gi] = r[:, 0:width]
        s_scr[:, gi * sw:(gi + 1) * sw] = r[:, width:width + sw]
        ssw_scr[:, gi * sw:(gi + 1) * sw] = r[:, width + sw:width + 2 * sw]

    a1 = jnp.concatenate([coef_ref[gi, 0:1, :] for gi in range(gb)], axis=1)
    a2 = jnp.concatenate([coef_ref[gi, 1:2, :] for gi in range(gb)], axis=1)

    def step(n, carry):
        new = []
        for b in range(bsz):
            h, hs = carry[b]
            row = b * nc + n
            hp_scr[pl.ds(row, 1), :] = h
            hn = h * a1 + hs * a2 + s_scr[pl.ds(row, 1), :]
            hsn = hs * a1 - h * a2 + ssw_scr[pl.ds(row, 1), :]
            new.append((hn, hsn))
        return tuple(new)

    zero = jnp.zeros((1, gb * sw), F32)
    lax.fori_loop(0, nc, step, tuple((zero, zero) for _ in range(bsz)))

    for gi in range(gb):
        hp = hp_scr[:, gi * sw:(gi + 1) * sw].astype(BF16)
        yo = lax.dot_general(hp, e_ref[gi], (((1,), (1,)), ((), ())), preferred_element_type=F32)
        y_ref[gi] = (yi_scr[gi] + yo).astype(y_ref.dtype)


def _ssm(ug, w1, e, coef, bsz, nc, gb=4):
    g, rows, width = ug.shape
    sw = 2 * SSM_STATE
    return pl.pallas_call(
        functools.partial(_ssm_kernel, bsz=bsz, nc=nc),
        grid=(g // gb,),
        in_specs=[
            pl.BlockSpec((gb, rows, width), lambda i: (i, 0, 0)),
            pl.BlockSpec((gb, width, width + 2 * sw), lambda i: (i, 0, 0)),
            pl.BlockSpec((gb, width, sw), lambda i: (i, 0, 0)),
            pl.BlockSpec((gb, 8, sw), lambda i: (i, 0, 0)),
        ],
        out_specs=pl.BlockSpec((gb, rows, width), lambda i: (i, 0, 0)),
        out_shape=jax.ShapeDtypeStruct((g, rows, width), BF16),
        scratch_shapes=[
            pltpu.VMEM((gb, rows, width), F32),
            pltpu.VMEM((rows, gb * sw), F32),
            pltpu.VMEM((rows, gb * sw), F32),
            pltpu.VMEM((rows, gb * sw), F32),
        ],
        compiler_params=_params(("parallel",)),
        name="ssm_scan",
    )(ug, w1, e, coef)


def _mix_kernel(y_ref, u_ref, a_ref, ga_ref, gb_ref, x_ref, dsk_ref, gffn_ref,
                wglu_ref, wao_ref, wout_ref, h_ref, n_ref):
    d = x_ref.shape[1]
    y = y_ref[...].astype(F32) + dsk_ref[...] * u_ref[...].astype(F32)
    gy = _gelu_tanh(y).astype(BF16)
    yab = jnp.dot(gy, wglu_ref[...], preferred_element_type=F32)
    ssm_out = yab[:, 0:d] * _sigmoid(yab[:, d:2 * d])
    ao = jnp.dot(a_ref[...], wao_ref[...], preferred_element_type=F32)
    merged = (_sigmoid(ga_ref[...].astype(F32)) * ssm_out
              + _sigmoid(gb_ref[...].astype(F32)) * ao).astype(BF16)
    h = x_ref[...] + jnp.dot(merged, wout_ref[...], preferred_element_type=F32)
    h_ref[...] = h
    ms = jnp.mean(h * h, axis=-1, keepdims=True)
    n_ref[...] = (h * lax.rsqrt(ms + EPS) * gffn_ref[...]).astype(n_ref.dtype)


def _mix(y, z, attn, xp, d_skip, g_ffn, w_glu, w_ao, w_out, tm=256):
    rows, d = xp.shape
    ds = y.shape[1]
    u_col = 3 * D_ATTN // ds
    ga_col = (3 * D_ATTN + ds) // d
    assert (3 * D_ATTN) % ds == 0 and (3 * D_ATTN + ds) % d == 0
    return pl.pallas_call(
        _mix_kernel,
        grid=(rows // tm,),
        in_specs=[
            pl.BlockSpec((tm, ds), lambda i: (i, 0)),
            pl.BlockSpec((tm, ds), lambda i: (i, u_col)),
            pl.BlockSpec((tm, D_ATTN), lambda i: (i, 0)),
            pl.BlockSpec((tm, d), lambda i: (i, ga_col)),
            pl.BlockSpec((tm, d), lambda i: (i, ga_col + 1)),
            pl.BlockSpec((tm, d), lambda i: (i, 0)),
            _resident((1, ds), lambda i: (0, 0)),
            _resident((1, d), lambda i: (0, 0)),
            _resident(w_glu.shape, lambda i: (0, 0)),
            _resident(w_ao.shape, lambda i: (0, 0)),
            _resident(w_out.shape, lambda i: (0, 0)),
        ],
        out_specs=[
            pl.BlockSpec((tm, d), lambda i: (i, 0)),
            pl.BlockSpec((tm, d), lambda i: (i, 0)),
        ],
        out_shape=[
            jax.ShapeDtypeStruct((rows, d), F32),
            jax.ShapeDtypeStruct((rows, d), BF16),
        ],
        compiler_params=_params(("parallel",)),
        name="mix_out",
    )(y, z, attn, z, z, xp, d_skip, g_ffn, w_glu, w_ao, w_out)


def _ffn_up_kernel(n_ref, wg_ref, wu_ref, cw_ref, cb_ref, a_ref, tail):
    tm = n_ref.shape[0]

    @pl.when(pl.program_id(1) == 0)
    def _():
        tail[...] = jnp.zeros_like(tail)

    n = n_ref[...]
    g = jnp.dot(n, wg_ref[...], preferred_element_type=F32)
    u = jnp.dot(n, wu_ref[...], preferred_element_type=F32)
    row = lax.broadcasted_iota(jnp.int32, g.shape, 0)
    prev1 = tail[CONV_WIDTH - 2:CONV_WIDTH - 1, :]
    prev2 = tail[CONV_WIDTH - 3:CONV_WIDTH - 2, :]
    g1 = jnp.where(row == 0, prev1, pltpu.roll(g, 1, 0))
    g2 = jnp.where(row == 0, prev2, jnp.where(row == 1, prev1, pltpu.roll(g, 2, 0)))
    gc = cb_ref[...] + cw_ref[0:1, :] * g2 + cw_ref[1:2, :] * g1 + cw_ref[2:3, :] * g
    tail[0:CONV_WIDTH - 1, :] = g[tm - (CONV_WIDTH - 1):tm, :]
    a_ref[...] = (gc * _sigmoid(gc) * u).astype(a_ref.dtype)


def _ffn_up(n2, w_up, conv_w, conv_b, tm=512, tn=512):
    rows, d = n2.shape
    dff = conv_w.shape[1]
    nj = dff // tn
    return pl.pallas_call(
        _ffn_up_kernel,
        grid=(nj, rows // tm),
        in_specs=[
            pl.BlockSpec((tm, d), lambda j, i: (i, 0)),
            pl.BlockSpec((d, tn), lambda j, i: (0, j)),
            pl.BlockSpec((d, tn), lambda j, i: (0, nj + j)),
            pl.BlockSpec((CONV_WIDTH, tn), lambda j, i: (0, j)),
            pl.BlockSpec((1, tn), lambda j, i: (0, j)),
        ],
        out_specs=pl.BlockSpec((tm, tn), lambda j, i: (i, j)),
        out_shape=jax.ShapeDtypeStruct((rows, dff), BF16),
        scratch_shapes=[pltpu.VMEM((8, tn), F32)],
        compiler_params=_params(("parallel", "arbitrary")),
        name="ffn_up",
    )(n2, w_up, w_up, conv_w, conv_b)


def _ffn_down_kernel(a_ref, w_ref, h_ref, g_ref, o_ref):
    h = h_ref[...] + jnp.dot(a_ref[...], w_ref[...], preferred_element_type=F32)
    ms = jnp.mean(h * h, axis=-1, keepdims=True)
    o_ref[...] = (h * lax.rsqrt(ms + EPS) * g_ref[...]).astype(o_ref.dtype)


def _ffn_down(act, w_down, h1, g_final, bsz, lp, tm=ROW_ALIGN):
    dff, d = w_down.shape
    nt_in = lp // tm
    nt_out = nt_in - ROW_ALIGN // tm
    skip = ROW_ALIGN // tm
    return pl.pallas_call(
        _ffn_down_kernel,
        grid=(bsz, nt_out),
        in_specs=[
            pl.BlockSpec((tm, dff), lambda b, t: (b * nt_in + skip + t, 0)),
            _resident((dff, d), lambda b, t: (0, 0)),
            pl.BlockSpec((tm, d), lambda b, t: (b * nt_in + skip + t, 0)),
            _resident((1, d), lambda b, t: (0, 0)),
        ],
        out_specs=pl.BlockSpec((tm, d), lambda b, t: (b * nt_out + t, 0)),
        out_shape=jax.ShapeDtypeStruct((bsz * nt_out * tm, d), F32),
        compiler_params=_params(("parallel", "parallel")),
        name="ffn_down",
    )(act, w_down, h1, g_final)


def _layer(xp, bsz, lp, g_mix, w_in, b_f, lam_re, lam_im, log_dt, b_re, b_im, c_re, c_im, d_skip,
           w_glu, w_attn_o, w_out, g_ffn, w_up, conv_w, conv_b, w_down):
    d = xp.shape[1]
    ds = d_skip.shape[0]
    o_q, o_k, o_v, o_f, o_u, o_ga, o_gb = (0, D_ATTN, 2 * D_ATTN, 3 * D_ATTN, 3 * D_ATTN + HEADS,
                                            3 * D_ATTN + HEADS + ds, 3 * D_ATTN + HEADS + ds + d)
    scale = HEAD_DIM ** -0.5
    w_main = jnp.concatenate([w_in[:, o_q:o_k] * scale, w_in[:, o_k:o_f], w_in[:, o_u:]], axis=1).astype(BF16)
    w_f = jnp.pad(w_in[:, o_f:o_u], ((0, 0), (0, 128 - HEADS))).astype(BF16)
    bf = jnp.pad(b_f.astype(F32), (0, 128 - HEADS)).reshape(1, 128)

    z, f = _inproj(xp, g_mix.reshape(1, d), w_main, w_f)

    qa, ka = _prep(f, bf, z, bsz, lp)
    attn = _flash(qa, ka, z, bsz, lp)

    w1, e, coef = _ssm_weights(lam_re, lam_im, log_dt, b_re, b_im, c_re, c_im)
    groups = ds // SSM_GROUP
    nc = lp // SSM_CHUNK
    u = z[:, 3 * D_ATTN:3 * D_ATTN + ds]
    ug = u.reshape(bsz * nc, SSM_CHUNK, groups, SSM_GROUP).transpose(2, 0, 1, 3)
    ug = ug.reshape(groups, bsz * nc, SSM_CHUNK * SSM_GROUP)
    yg = _ssm(ug, w1, e, coef, bsz, nc)
    y = yg.reshape(groups, bsz * nc, SSM_CHUNK, SSM_GROUP).transpose(1, 2, 0, 3).reshape(bsz * lp, ds)

    h1, n2 = _mix(y, z, attn, xp, d_skip.reshape(1, ds).astype(F32), g_ffn.reshape(1, d).astype(F32),
                  w_glu.astype(BF16), w_attn_o.astype(BF16), w_out.astype(BF16))
    act = _ffn_up(n2, w_up.astype(BF16), conv_w.astype(F32), conv_b.reshape(1, -1).astype(F32))
    return act, h1


def kernel(x, meta, g_mix, w_in, b_f, lam_re, lam_im, log_dt, b_re, b_im, c_re, c_im, d_skip,
           w_glu, w_attn_o, w_out, g_ffn, w_up, conv_w, conv_b, w_down, g_final):
    bsz, seq, d = x.shape
    depth = g_mix.shape[0]
    assert depth == 1 and meta.shape[0] == N_META and seq % ROW_ALIGN == 0
    lp = seq + ROW_ALIGN
    assert (bsz * lp) % 512 == 0
    head = jnp.concatenate([jnp.zeros((PAD, d), x.dtype), meta.astype(x.dtype)], axis=0)
    xp = jnp.concatenate([jnp.broadcast_to(head[None], (bsz, ROW_ALIGN, d)), x], axis=1).reshape(bsz * lp, d)
    act, h1 = _layer(xp, bsz, lp, g_mix[0], w_in[0], b_f[0], lam_re[0], lam_im[0], log_dt[0],
                     b_re[0], b_im[0], c_re[0], c_im[0], d_skip[0], w_glu[0], w_attn_o[0], w_out[0],
                     g_ffn[0], w_up[0], conv_w[0], conv_b[0], w_down[0])
    out = _ffn_down(act, w_down[0].astype(BF16), h1, g_final.reshape(1, d).astype(F32), bsz, lp)
    return out.reshape(bsz, seq, d)
```

```python
import functools
import math

import jax
import jax.numpy as jnp
from jax import lax
from jax.experimental import pallas as pl
from jax.experimental.pallas import tpu as pltpu

N_META = 16
HEADS = 8
HEAD_DIM = 128
D_ATTN = HEADS * HEAD_DIM
SSM_GROUP = 16
SSM_STATE = 64
SSM_CHUNK = 16
CONV_WIDTH = 3
EPS = 1e-6

ROW_ALIGN = 256
PAD = ROW_ALIGN - N_META
ATT_TQ = 1024
ATT_TK = 512
QK_AUG = 2 * HEAD_DIM
LOG2E = math.log2(math.e)
MASK_BIG = 1e30
NEG = -3e38
VMEM_LIMIT = 56 * 1024 * 1024

F32 = jnp.float32
BF16 = jnp.bfloat16


def _sigmoid(x):
    return 1.0 / (1.0 + jnp.exp(-x))


def _gelu_tanh(x):
    c = math.sqrt(2.0 / math.pi)
    return 0.5 * x * (1.0 + jnp.tanh(c * (x + 0.044715 * (x * x * x))))


def _params(sem, limit=VMEM_LIMIT):
    return pltpu.CompilerParams(dimension_semantics=sem, vmem_limit_bytes=limit)


def _resident(shape, index_map):
    return pl.BlockSpec(shape, index_map, pipeline_mode=pl.Buffered(1))


def _inproj_kernel(x_ref, g_ref, w_ref, wf_ref, z_ref, f_ref, n_scr):
    @pl.when(pl.program_id(1) == 0)
    def _():
        x = x_ref[...]
        ms = jnp.mean(x * x, axis=-1, keepdims=True)
        n = (x * lax.rsqrt(ms + EPS) * g_ref[...]).astype(BF16)
        n_scr[...] = n
        f_ref[...] = jnp.dot(n, wf_ref[...], preferred_element_type=F32)

    z_ref[...] = jnp.dot(n_scr[...], w_ref[...], preferred_element_type=F32).astype(z_ref.dtype)


def _inproj(xp, g, w, wf, tm=512, tn=2048):
    rows, d = xp.shape
    n_out = w.shape[1]
    return pl.pallas_call(
        _inproj_kernel,
        grid=(rows // tm, n_out // tn),
        in_specs=[
            pl.BlockSpec((tm, d), lambda i, j: (i, 0)),
            pl.BlockSpec((1, d), lambda i, j: (0, 0)),
            pl.BlockSpec((d, tn), lambda i, j: (0, j)),
            pl.BlockSpec((d, 128), lambda i, j: (0, 0)),
        ],
        out_specs=[
            pl.BlockSpec((tm, tn), lambda i, j: (i, j)),
            pl.BlockSpec((tm, 128), lambda i, j: (i, 0)),
        ],
        out_shape=[
            jax.ShapeDtypeStruct((rows, n_out), BF16),
            jax.ShapeDtypeStruct((rows, 128), F32),
        ],
        scratch_shapes=[pltpu.VMEM((tm, d), BF16)],
        compiler_params=_params(("parallel", "arbitrary")),
        name="inproj",
    )(xp, g, w, wf)


def _split3(x):
    hi = x.astype(BF16)
    r1 = x - hi.astype(F32)
    mid = r1.astype(BF16)
    lo = (r1 - mid.astype(F32)).astype(BF16)
    return hi, mid, lo


def _prep_kernel(f_ref, bf_ref, q_ref, k_ref, qa_ref, ka_ref, carry):
    t = pl.program_id(1)
    tm = f_ref.shape[0]

    @pl.when(t == 0)
    def _():
        carry[...] = jnp.zeros_like(carry)

    x = f_ref[...] + bf_ref[...]
    logf = jnp.minimum(x, 0.0) - jnp.log1p(jnp.exp(-jnp.abs(x)))
    row = lax.broadcasted_iota(jnp.int32, (tm, 1), 0)
    valid = jnp.logical_or(t > 0, row >= PAD)
    logf = jnp.where(valid, logf, 0.0)

    hi, mid, lo = _split3(logf)
    r = lax.broadcasted_iota(jnp.int32, (tm, tm), 0)
    c = lax.broadcasted_iota(jnp.int32, (tm, tm), 1)
    tri = (r >= c).astype(BF16)
    cs = (jnp.dot(tri, hi, preferred_element_type=F32)
          + jnp.dot(tri, mid, preferred_element_type=F32)
          + jnp.dot(tri, lo, preferred_element_type=F32))
    fcum = cs + carry[0:1, :]
    carry[0:1, :] = fcum[tm - 1:tm, :]

    lane = lax.broadcasted_iota(jnp.int32, (tm, HEAD_DIM), 1)
    one = jnp.ones((tm, HEAD_DIM), F32)
    zero = jnp.zeros((tm, HEAD_DIM), F32)
    for h in range(HEADS):
        fh = fcum[:, h:h + 1] * LOG2E
        hi, mid, lo = _split3(fh)
        hi, mid, lo = hi.astype(F32), mid.astype(F32), lo.astype(F32)
        qb = jnp.where(lane == 0, hi, jnp.where(lane == 1, mid, jnp.where(lane == 2, lo,
             jnp.where(lane < 6, one, zero))))
        khi = jnp.where(valid, hi, MASK_BIG)
        kmid = jnp.where(valid, mid, 0.0)
        klo = jnp.where(valid, lo, 0.0)
        kb = jnp.where(lane < 3, one, jnp.where(lane == 3, -khi, jnp.where(lane == 4, -kmid,
             jnp.where(lane == 5, -klo, zero))))
        qa_ref[0, h, :, 0:HEAD_DIM] = q_ref[:, h * HEAD_DIM:(h + 1) * HEAD_DIM]
        qa_ref[0, h, :, HEAD_DIM:QK_AUG] = qb.astype(BF16)
        ka_ref[0, h, :, 0:HEAD_DIM] = k_ref[:, h * HEAD_DIM:(h + 1) * HEAD_DIM]
        ka_ref[0, h, :, HEAD_DIM:QK_AUG] = kb.astype(BF16)


def _prep(f, bf, z, bsz, lp, tm=ROW_ALIGN):
    nt = lp // tm
    shape = jax.ShapeDtypeStruct((bsz, HEADS, lp, QK_AUG), BF16)
    blk = pl.BlockSpec((1, HEADS, tm, QK_AUG), lambda b, t: (b, 0, t, 0))
    return pl.pallas_call(
        _prep_kernel,
        grid=(bsz, nt),
        in_specs=[
            pl.BlockSpec((tm, 128), lambda b, t: (b * nt + t, 0)),
            pl.BlockSpec((1, 128), lambda b, t: (0, 0)),
            pl.BlockSpec((tm, D_ATTN), lambda b, t: (b * nt + t, 0)),
            pl.BlockSpec((tm, D_ATTN), lambda b, t: (b * nt + t, 1)),
        ],
        out_specs=[blk, blk],
        out_shape=[shape, shape],
        scratch_shapes=[pltpu.VMEM((8, 128), F32)],
        compiler_params=_params(("parallel", "arbitrary")),
        name="attn_prep",
    )(f, bf, z, z)


def _flash_kernel(q_ref, k_ref, v_ref, o_ref, sa_ref, sb_ref, m_ref, l_ref, acc_ref):
    lp = v_ref.shape[0]
    tq, tk = ATT_TQ, ATT_TK
    n_main = (lp - ROW_ALIGN) // tq
    nt_dims = (((1,), (1,)), ((), ()))
    tn_dims = (((0,), (0,)), ((), ()))

    def scores(s_ref, q_off, nq, k_off, nk, col0):
        kc = k_ref[0, 0, pl.ds(k_off, nk), :]
        qc = q_ref[0, 0, pl.ds(q_off, nq), :]
        s_ref[0:nk, col0:col0 + nq] = lax.dot_general(kc, qc, nt_dims, preferred_element_type=F32)

    def absorb(s_ref, k_off, nk, lo, hi, mask_shift=None):
        s = s_ref[0:nk, lo:hi]
        if mask_shift is not None:
            kr = lax.broadcasted_iota(jnp.int32, s.shape, 0) + mask_shift
            qc = lax.broadcasted_iota(jnp.int32, s.shape, 1) + lo
            s = jnp.where(kr > qc, NEG, s)
        m = m_ref[0:1, lo:hi]
        m_new = jnp.maximum(m, jnp.max(s, axis=0, keepdims=True))
        alpha = jnp.exp2(m - m_new)
        p = jnp.exp2(s - m_new)
        l_ref[0:1, lo:hi] = alpha * l_ref[0:1, lo:hi] + jnp.sum(p, axis=0, keepdims=True)
        m_ref[0:1, lo:hi] = m_new
        vc = v_ref[pl.ds(k_off, nk), :]
        pv = lax.dot_general(vc, p.astype(BF16), tn_dims, preferred_element_type=F32)
        acc_ref[:, lo:hi] = alpha * acc_ref[:, lo:hi] + pv

    def reset(n):
        m_ref[0:1, 0:n] = jnp.full((1, n), NEG, F32)
        l_ref[0:1, 0:n] = jnp.zeros((1, n), F32)
        acc_ref[:, 0:n] = jnp.zeros((HEAD_DIM, n), F32)

    def finish(q_off, n):
        o_ref[pl.ds(q_off, n), :] = (acc_ref[:, 0:n] / l_ref[0:1, 0:n]).T.astype(o_ref.dtype)

    reset(ROW_ALIGN)
    scores(sa_ref, 0, ROW_ALIGN, 0, ROW_ALIGN, 0)
    absorb(sa_ref, 0, ROW_ALIGN, 0, ROW_ALIGN, mask_shift=0)
    finish(0, ROW_ALIGN)

    def tile(j, _):
        q_off = pl.multiple_of(ROW_ALIGN + j * tq, ROW_ALIGN)
        reset(tq)
        scores(sa_ref, q_off, tq, 0, ROW_ALIGN, 0)
        scores(sb_ref, q_off, tq, ROW_ALIGN, tk, 0)
        absorb(sa_ref, 0, ROW_ALIGN, 0, tq)

        def pair(p, _):
            kb = pl.multiple_of(ROW_ALIGN + p * (2 * tk), ROW_ALIGN)
            scores(sa_ref, q_off, tq, kb + tk, tk, 0)
            absorb(sb_ref, kb, tk, 0, tq)
            scores(sb_ref, q_off, tq, kb + 2 * tk, tk, 0)
            absorb(sa_ref, kb + tk, tk, 0, tq)
            return 0

        lax.fori_loop(0, j * (tq // (2 * tk)), pair, 0)
        scores(sa_ref, q_off + tk, tq - tk, q_off + tk, tk, tk)
        absorb(sb_ref, q_off, tk, 0, tq, mask_shift=0)
        absorb(sa_ref, q_off + tk, tk, tk, tq, mask_shift=tk)
        finish(q_off, tq)
        return 0

    lax.fori_loop(0, n_main, tile, 0)


def _flash(qa, ka, z, bsz, lp):
    assert (lp - ROW_ALIGN) % ATT_TQ == 0 and ATT_TQ == 2 * ATT_TK
    v_col0 = 2 * D_ATTN // HEAD_DIM
    return pl.pallas_call(
        _flash_kernel,
        grid=(bsz, HEADS),
        in_specs=[
            pl.BlockSpec((1, 1, lp, QK_AUG), lambda b, h: (b, h, 0, 0)),
            pl.BlockSpec((1, 1, lp, QK_AUG), lambda b, h: (b, h, 0, 0)),
            pl.BlockSpec((lp, HEAD_DIM), lambda b, h: (b, v_col0 + h)),
        ],
        out_specs=pl.BlockSpec((lp, HEAD_DIM), lambda b, h: (b, h)),
        out_shape=jax.ShapeDtypeStruct((bsz * lp, D_ATTN), BF16),
        scratch_shapes=[
            pltpu.VMEM((ATT_TK, ATT_TQ), F32),
            pltpu.VMEM((ATT_TK, ATT_TQ), F32),
            pltpu.VMEM((8, ATT_TQ), F32),
            pltpu.VMEM((8, ATT_TQ), F32),
            pltpu.VMEM((HEAD_DIM, ATT_TQ), F32),
        ],
        compiler_params=_params(("parallel", "parallel")),
        name="flash",
    )(qa, ka, z)


def _ssm_weights_kernel(lr_ref, li_ref, ldt_ref, btr_ref, bti_ref, cr_ref, ci_ref,
                        w1_ref, e_ref, coef_ref):
    t = SSM_CHUNK
    p = SSM_STATE
    dt = jnp.exp(ldt_ref[0])
    lr = lr_ref[0]
    li = li_ref[0]
    d = lax.broadcasted_iota(jnp.int32, (2 * t, p), 0).astype(F32)
    mag = jnp.exp(d * (lr * dt))
    ang = d * (li * dt)
    pr = mag * jnp.cos(ang)
    pi = mag * jnp.sin(ang)
    a_re = pr[1:2]
    a_im = pi[1:2]
    den = lr * lr + li * li
    nr = a_re - 1.0
    z_re = (nr * lr + a_im * li) / den
    z_im = (a_im * lr - nr * li) / den
    azr = pr * z_re - pi * z_im
    azi = pr * z_im + pi * z_re

    btr = btr_ref[0]
    bti = bti_ref[0]
    cr = cr_ref[0]
    ci = ci_ref[0]

    caz_re, caz_im, ms, ms_sw, e = [], [], [], [], []
    for i in range(t):
        zr = azr[i:i + 1]
        zi = azi[i:i + 1]
        caz_re.append(cr * zr - ci * zi)
        caz_im.append(cr * zi + ci * zr)
        wr = azr[t - 1 - i:t - i]
        wi = azi[t - 1 - i:t - i]
        sr = btr * wr - bti * wi
        si = btr * wi + bti * wr
        ms.append(jnp.concatenate([sr, si], axis=1))
        ms_sw.append(jnp.concatenate([si, sr], axis=1))
        er = cr * pr[i + 1:i + 2] - ci * pi[i + 1:i + 2]
        ei = cr * pi[i + 1:i + 2] + ci * pr[i + 1:i + 2]
        e.append(jnp.concatenate([er, -ei], axis=1))
    caz_re = jnp.concatenate(caz_re, axis=0)
    caz_im = jnp.concatenate(caz_im, axis=0)
    nt_dims = (((1,), (1,)), ((), ()))
    r0 = (lax.dot_general(btr, caz_re, nt_dims, precision=lax.Precision.HIGHEST, preferred_element_type=F32)
          - lax.dot_general(bti, caz_im, nt_dims, precision=lax.Precision.HIGHEST, preferred_element_type=F32))
    width = t * SSM_GROUP
    lane = lax.broadcasted_iota(jnp.int32, (SSM_GROUP, width), 1)
    for i in range(t):
        blk = r0 if i == 0 else jnp.where(lane >= i * SSM_GROUP, pltpu.roll(r0, i * SSM_GROUP, 1), 0.0)
        w1_ref[0, i * SSM_GROUP:(i + 1) * SSM_GROUP, 0:width] = blk.astype(w1_ref.dtype)
    w1_ref[0, :, width:width + 2 * p] = jnp.concatenate(ms, axis=0).astype(w1_ref.dtype)
    w1_ref[0, :, width + 2 * p:width + 4 * p] = jnp.concatenate(ms_sw, axis=0).astype(w1_ref.dtype)
    e_ref[0] = jnp.concatenate(e, axis=0).astype(e_ref.dtype)
    ar = pr[t:t + 1]
    ai = pi[t:t + 1]
    coef_ref[0] = jnp.concatenate(
        [jnp.concatenate([ar, ar], axis=1), jnp.concatenate([-ai, ai], axis=1),
         jnp.zeros((6, 2 * p), F32)], axis=0)


def _ssm_weights(lam_re, lam_im, log_dt, b_re, b_im, c_re, c_im):
    g = lam_re.shape[0]
    t, p, c = SSM_CHUNK, SSM_STATE, SSM_GROUP
    vec = pl.BlockSpec((1, 1, p), lambda i: (i, 0, 0))
    mat = pl.BlockSpec((1, c, p), lambda i: (i, 0, 0))
    return pl.pallas_call(
        _ssm_weights_kernel,
        grid=(g,),
        in_specs=[vec, vec, pl.BlockSpec((1, 1, 1), lambda i: (i, 0, 0)), mat, mat, mat, mat],
        out_specs=[
            pl.BlockSpec((1, t * c, t * c + 4 * p), lambda i: (i, 0, 0)),
            pl.BlockSpec((1, t * c, 2 * p), lambda i: (i, 0, 0)),
            pl.BlockSpec((1, 8, 2 * p), lambda i: (i, 0, 0)),
        ],
        out_shape=[
            jax.ShapeDtypeStruct((g, t * c, t * c + 4 * p), BF16),
            jax.ShapeDtypeStruct((g, t * c, 2 * p), BF16),
            jax.ShapeDtypeStruct((g, 8, 2 * p), F32),
        ],
        compiler_params=_params(("parallel",)),
        name="ssm_weights",
    )(lam_re.reshape(g, 1, p), lam_im.reshape(g, 1, p), log_dt.reshape(g, 1, 1),
      jnp.swapaxes(b_re, 1, 2), jnp.swapaxes(b_im, 1, 2), c_re, c_im)


def _ssm_kernel(u_ref, w1_ref, e_ref, coef_ref, y_ref, yi_scr, s_scr, ssw_scr, hp_scr, *, bsz, nc):
    gb = u_ref.shape[0]
    width = SSM_CHUNK * SSM_GROUP
    sw = 2 * SSM_STATE
    for gi in range(gb):
        r = jnp.dot(u_ref[gi], w1_ref[gi], preferred_element_type=F32)
        yi_scr[gi] = r[:, 0:width]
        s_scr[:, gi * sw:(gi + 1) * sw] = r[:, width:width + sw]
        ssw_scr[:, gi * sw:(gi + 1) * sw] = r[:, width + sw:width + 2 * sw]

    a1 = jnp.concatenate([coef_ref[gi, 0:1, :] for gi in range(gb)], axis=1)
    a2 = jnp.concatenate([coef_ref[gi, 1:2, :] for gi in range(gb)], axis=1)

    def step(n, carry):
        new = []
        for b in range(bsz):
            h, hs = carry[b]
            row = b * nc + n
            hp_scr[pl.ds(row, 1), :] = h
            hn = h * a1 + hs * a2 + s_scr[pl.ds(row, 1), :]
            hsn = hs * a1 - h * a2 + ssw_scr[pl.ds(row, 1), :]
            new.append((hn, hsn))
        return tuple(new)

    zero = jnp.zeros((1, gb * sw), F32)
    lax.fori_loop(0, nc, step, tuple((zero, zero) for _ in range(bsz)))

    for gi in range(gb):
        hp = hp_scr[:, gi * sw:(gi + 1) * sw].astype(BF16)
        yo = lax.dot_general(hp, e_ref[gi], (((1,), (1,)), ((), ())), preferred_element_type=F32)
        y_ref[gi] = (yi_scr[gi] + yo).astype(y_ref.dtype)


def _ssm(ug, w1, e, coef, bsz, nc, gb=4):
    g, rows, width = ug.shape
    sw = 2 * SSM_STATE
    return pl.pallas_call(
        functools.partial(_ssm_kernel, bsz=bsz, nc=nc),
        grid=(g // gb,),
        in_specs=[
            pl.BlockSpec((gb, rows, width), lambda i: (i, 0, 0)),
            pl.BlockSpec((gb, width, width + 2 * sw), lambda i: (i, 0, 0)),
            pl.BlockSpec((gb, width, sw), lambda i: (i, 0, 0)),
            pl.BlockSpec((gb, 8, sw), lambda i: (i, 0, 0)),
        ],
        out_specs=pl.BlockSpec((gb, rows, width), lambda i: (i, 0, 0)),
        out_shape=jax.ShapeDtypeStruct((g, rows, width), BF16),
        scratch_shapes=[
            pltpu.VMEM((gb, rows, width), F32),
            pltpu.VMEM((rows, gb * sw), F32),
            pltpu.VMEM((rows, gb * sw), F32),
            pltpu.VMEM((rows, gb * sw), F32),
        ],
        compiler_params=_params(("parallel",)),
        name="ssm_scan",
    )(ug, w1, e, coef)


def _mix_kernel(y_ref, u_ref, a_ref, ga_ref, gb_ref, x_ref, dsk_ref, gffn_ref,
                wglu_ref, wao_ref, wout_ref, h_ref, n_ref):
    d = x_ref.shape[1]
    y = y_ref[...].astype(F32) + dsk_ref[...] * u_ref[...].astype(F32)
    gy = _gelu_tanh(y).astype(BF16)
    yab = jnp.dot(gy, wglu_ref[...], preferred_element_type=F32)
    ssm_out = yab[:, 0:d] * _sigmoid(yab[:, d:2 * d])
    ao = jnp.dot(a_ref[...], wao_ref[...], preferred_element_type=F32)
    merged = (_sigmoid(ga_ref[...].astype(F32)) * ssm_out
              + _sigmoid(gb_ref[...].astype(F32)) * ao).astype(BF16)
    h = x_ref[...] + jnp.dot(merged, wout_ref[...], preferred_element_type=F32)
    h_ref[...] = h
    ms = jnp.mean(h * h, axis=-1, keepdims=True)
    n_ref[...] = (h * lax.rsqrt(ms + EPS) * gffn_ref[...]).astype(n_ref.dtype)


def _mix(y, z, attn, xp, d_skip, g_ffn, w_glu, w_ao, w_out, tm=256):
    rows, d = xp.shape
    ds = y.shape[1]
    u_col = 3 * D_ATTN // ds
    ga_col = (3 * D_ATTN + ds) // d
    assert (3 * D_ATTN) % ds == 0 and (3 * D_ATTN + ds) % d == 0
    return pl.pallas_call(
        _mix_kernel,
        grid=(rows // tm,),
        in_specs=[
            pl.BlockSpec((tm, ds), lambda i: (i, 0)),
            pl.BlockSpec((tm, ds), lambda i: (i, u_col)),
            pl.BlockSpec((tm, D_ATTN), lambda i: (i, 0)),
            pl.BlockSpec((tm, d), lambda i: (i, ga_col)),
            pl.BlockSpec((tm, d), lambda i: (i, ga_col + 1)),
            pl.BlockSpec((tm, d), lambda i: (i, 0)),
            _resident((1, ds), lambda i: (0, 0)),
            _resident((1, d), lambda i: (0, 0)),
            _resident(w_glu.shape, lambda i: (0, 0)),
            _resident(w_ao.shape, lambda i: (0, 0)),
            _resident(w_out.shape, lambda i: (0, 0)),
        ],
        out_specs=[
            pl.BlockSpec((tm, d), lambda i: (i, 0)),
            pl.BlockSpec((tm, d), lambda i: (i, 0)),
        ],
        out_shape=[
            jax.ShapeDtypeStruct((rows, d), F32),
            jax.ShapeDtypeStruct((rows, d), BF16),
        ],
        compiler_params=_params(("parallel",)),
        name="mix_out",
    )(y, z, attn, z, z, xp, d_skip, g_ffn, w_glu, w_ao, w_out)


def _ffn_up_kernel(n_ref, wg_ref, wu_ref, cw_ref, cb_ref, a_ref, tail):
    tm = n_ref.shape[0]

    @pl.when(pl.program_id(1) == 0)
    def _():
        tail[...] = jnp.zeros_like(tail)

    n = n_ref[...]
    g = jnp.dot(n, wg_ref[...], preferred_element_type=F32)
    u = jnp.dot(n, wu_ref[...], preferred_element_type=F32)
    row = lax.broadcasted_iota(jnp.int32, g.shape, 0)
    prev1 = tail[CONV_WIDTH - 2:CONV_WIDTH - 1, :]
    prev2 = tail[CONV_WIDTH - 3:CONV_WIDTH - 2, :]
    g1 = jnp.where(row == 0, prev1, pltpu.roll(g, 1, 0))
    g2 = jnp.where(row == 0, prev2, jnp.where(row == 1, prev1, pltpu.roll(g, 2, 0)))
    gc = cb_ref[...] + cw_ref[0:1, :] * g2 + cw_ref[1:2, :] * g1 + cw_ref[2:3, :] * g
    tail[0:CONV_WIDTH - 1, :] = g[tm - (CONV_WIDTH - 1):tm, :]
    a_ref[...] = (gc * _sigmoid(gc) * u).astype(a_ref.dtype)


def _ffn_up(n2, w_up, conv_w, conv_b, tm=512, tn=512):
    rows, d = n2.shape
    dff = conv_w.shape[1]
    nj = dff // tn
    return pl.pallas_call(
        _ffn_up_kernel,
        grid=(nj, rows // tm),
        in_specs=[
            pl.BlockSpec((tm, d), lambda j, i: (i, 0)),
            pl.BlockSpec((d, tn), lambda j, i: (0, j)),
            pl.BlockSpec((d, tn), lambda j, i: (0, nj + j)),
            pl.BlockSpec((CONV_WIDTH, tn), lambda j, i: (0, j)),
            pl.BlockSpec((1, tn), lambda j, i: (0, j)),
        ],
        out_specs=pl.BlockSpec((tm, tn), lambda j, i: (i, j)),
        out_shape=jax.ShapeDtypeStruct((rows, dff), BF16),
        scratch_shapes=[pltpu.VMEM((8, tn), F32)],
        compiler_params=_params(("parallel", "arbitrary")),
        name="ffn_up",
    )(n2, w_up, w_up, conv_w, conv_b)


def _ffn_down_kernel(a_ref, w_ref, h_ref, g_ref, o_ref):
    h = h_ref[...] + jnp.dot(a_ref[...], w_ref[...], preferred_element_type=F32)
    ms = jnp.mean(h * h, axis=-1, keepdims=True)
    o_ref[...] = (h * lax.rsqrt(ms + EPS) * g_ref[...]).astype(o_ref.dtype)


def _ffn_down(act, w_down, h1, g_final, bsz, lp, tm=ROW_ALIGN):
    dff, d = w_down.shape
    nt_in = lp // tm
    nt_out = nt_in - ROW_ALIGN // tm
    skip = ROW_ALIGN // tm
    return pl.pallas_call(
        _ffn_down_kernel,
        grid=(bsz, nt_out),
        in_specs=[
            pl.BlockSpec((tm, dff), lambda b, t: (b * nt_in + skip + t, 0)),
            _resident((dff, d), lambda b, t: (0, 0)),
            pl.BlockSpec((tm, d), lambda b, t: (b * nt_in + skip + t, 0)),
            _resident((1, d), lambda b, t: (0, 0)),
        ],
        out_specs=pl.BlockSpec((tm, d), lambda b, t: (b * nt_out + t, 0)),
        out_shape=jax.ShapeDtypeStruct((bsz * nt_out * tm, d), F32),
        compiler_params=_params(("parallel", "parallel")),
        name="ffn_down",
    )(act, w_down, h1, g_final)


def _layer(xp, bsz, lp, g_mix, w_in, b_f, lam_re, lam_im, log_dt, b_re, b_im, c_re, c_im, d_skip,
           w_glu, w_attn_o, w_out, g_ffn, w_up, conv_w, conv_b, w_down):
    d = xp.shape[1]
    ds = d_skip.shape[0]
    o_q, o_k, o_v, o_f, o_u, o_ga, o_gb = (0, D_ATTN, 2 * D_ATTN, 3 * D_ATTN, 3 * D_ATTN + HEADS,
                                            3 * D_ATTN + HEADS + ds, 3 * D_ATTN + HEADS + ds + d)
    scale = HEAD_DIM ** -0.5 * LOG2E
    w_main = jnp.concatenate([w_in[:, o_q:o_k] * scale, w_in[:, o_k:o_f], w_in[:, o_u:]], axis=1).astype(BF16)
    w_f = jnp.pad(w_in[:, o_f:o_u], ((0, 0), (0, 128 - HEADS))).astype(BF16)
    bf = jnp.pad(b_f.astype(F32), (0, 128 - HEADS)).reshape(1, 128)

    z, f = _inproj(xp, g_mix.reshape(1, d), w_main, w_f)

    qa, ka = _prep(f, bf, z, bsz, lp)
    attn = _flash(qa, ka, z, bsz, lp)

    w1, e, coef = _ssm_weights(lam_re, lam_im, log_dt, b_re, b_im, c_re, c_im)
    groups = ds // SSM_GROUP
    nc = lp // SSM_CHUNK
    u = z[:, 3 * D_ATTN:3 * D_ATTN + ds]
    ug = u.reshape(bsz * nc, SSM_CHUNK, groups, SSM_GROUP).transpose(2, 0, 1, 3)
    ug = ug.reshape(groups, bsz * nc, SSM_CHUNK * SSM_GROUP)
    yg = _ssm(ug, w1, e, coef, bsz, nc)
    y = yg.reshape(groups, bsz * nc, SSM_CHUNK, SSM_GROUP).transpose(1, 2, 0, 3).reshape(bsz * lp, ds)

    h1, n2 = _mix(y, z, attn, xp, d_skip.reshape(1, ds).astype(F32), g_ffn.reshape(1, d).astype(F32),
                  w_glu.astype(BF16), w_attn_o.astype(BF16), w_out.astype(BF16))
    act = _ffn_up(n2, w_up.astype(BF16), conv_w.astype(F32), conv_b.reshape(1, -1).astype(F32))
    return act, h1


def kernel(x, meta, g_mix, w_in, b_f, lam_re, lam_im, log_dt, b_re, b_im, c_re, c_im, d_skip,
           w_glu, w_attn_o, w_out, g_ffn, w_up, conv_w, conv_b, w_down, g_final):
    bsz, seq, d = x.shape
    depth = g_mix.shape[0]
    assert depth == 1 and meta.shape[0] == N_META and seq % ROW_ALIGN == 0
    lp = seq + ROW_ALIGN
    assert (bsz * lp) % 512 == 0
    head = jnp.concatenate([jnp.zeros((PAD, d), x.dtype), meta.astype(x.dtype)], axis=0)
    xp = jnp.concatenate([jnp.broadcast_to(head[None], (bsz, ROW_ALIGN, d)), x], axis=1).reshape(bsz * lp, d)
    act, h1 = _layer(xp, bsz, lp, g_mix[0], w_in[0], b_f[0], lam_re[0], lam_im[0], log_dt[0],
                     b_re[0], b_im[0], c_re[0], c_im[0], d_skip[0], w_glu[0], w_attn_o[0], w_out[0],
                     g_ffn[0], w_up[0], conv_w[0], conv_b[0], w_down[0])
    out = _ffn_down(act, w_down[0].astype(BF16), h1, g_final.reshape(1, d).astype(F32), bsz, lp)
    return out.reshape(bsz, seq, d)
```

```python
import functools
import math

import jax
import jax.numpy as jnp
from jax import lax
from jax.experimental import pallas as pl
from jax.experimental.pallas import tpu as pltpu

N_META = 16
HEADS = 8
HEAD_DIM = 128
D_ATTN = HEADS * HEAD_DIM
SSM_GROUP = 16
SSM_STATE = 64
SSM_CHUNK = 16
SSM_SLAB = 8
CONV_WIDTH = 3
EPS = 1e-6

ROW_ALIGN = 256
PAD = ROW_ALIGN - N_META
ATT_TQ = 1024
ATT_TK = 512
QK_AUG = 2 * HEAD_DIM
LOG2E = math.log2(math.e)
MASK_BIG = 1e30
NEG = -3e38
VMEM_LIMIT = 56 * 1024 * 1024

F32 = jnp.float32
BF16 = jnp.bfloat16


def _sigmoid(x):
    return 1.0 / (1.0 + jnp.exp(-x))


def _gelu_tanh(x):
    c = math.sqrt(2.0 / math.pi)
    return 0.5 * x * (1.0 + jnp.tanh(c * (x + 0.044715 * (x * x * x))))


def _params(sem, limit=VMEM_LIMIT):
    return pltpu.CompilerParams(dimension_semantics=sem, vmem_limit_bytes=limit)


def _resident(shape, index_map):
    return pl.BlockSpec(shape, index_map, pipeline_mode=pl.Buffered(1))


def _seq_block(t, nt):
    return (t // nt) * (nt - 1) + jnp.maximum(t % nt - 1, 0)


def _padded_rows(t, nt, head_ref, x_ref):
    return jnp.where(t % nt == 0, head_ref[...], x_ref[...])


def _inproj_kernel(head_ref, xa_ref, xb_ref, g_ref, w_ref, wf_ref, z_ref, f_ref, u_ref, n_scr, *, nt, u_blk, u_lo):
    i = pl.program_id(0)
    j = pl.program_id(1)

    @pl.when(j == 0)
    def _():
        x = jnp.concatenate([_padded_rows(2 * i, nt, head_ref, xa_ref),
                             _padded_rows(2 * i + 1, nt, head_ref, xb_ref)], axis=0)
        ms = jnp.mean(x * x, axis=-1, keepdims=True)
        n = (x * lax.rsqrt(ms + EPS) * g_ref[...]).astype(BF16)
        n_scr[...] = n
        f_ref[...] = jnp.dot(n, wf_ref[...], preferred_element_type=F32)

    acc = jnp.dot(n_scr[...], w_ref[...], preferred_element_type=F32)
    z_ref[...] = acc.astype(z_ref.dtype)

    @pl.when(j == u_blk)
    def _():
        u_ref[...] = acc[:, u_lo:u_lo + u_ref.shape[1]]


def _inproj(head, x2, g, w, wf, rows, nt, u_off, u_width, tn=2048):
    d = x2.shape[1]
    tm = 2 * ROW_ALIGN
    n_out = w.shape[1]
    u_blk, u_lo = divmod(u_off, tn)
    assert u_lo + u_width <= tn and rows % tm == 0
    return pl.pallas_call(
        functools.partial(_inproj_kernel, nt=nt, u_blk=u_blk, u_lo=u_lo),
        grid=(rows // tm, n_out // tn),
        in_specs=[
            _resident((ROW_ALIGN, d), lambda i, j: (0, 0)),
            pl.BlockSpec((ROW_ALIGN, d), lambda i, j: (_seq_block(2 * i, nt), 0)),
            pl.BlockSpec((ROW_ALIGN, d), lambda i, j: (_seq_block(2 * i + 1, nt), 0)),
            pl.BlockSpec((1, d), lambda i, j: (0, 0)),
            pl.BlockSpec((d, tn), lambda i, j: (0, j)),
            pl.BlockSpec((d, 128), lambda i, j: (0, 0)),
        ],
        out_specs=[
            pl.BlockSpec((tm, tn), lambda i, j: (i, j)),
            pl.BlockSpec((tm, 128), lambda i, j: (i, 0)),
            pl.BlockSpec((tm, u_width), lambda i, j: (i, 0)),
        ],
        out_shape=[
            jax.ShapeDtypeStruct((rows, n_out), BF16),
            jax.ShapeDtypeStruct((rows, 128), F32),
            jax.ShapeDtypeStruct((rows, u_width), F32),
        ],
        scratch_shapes=[pltpu.VMEM((tm, d), BF16)],
        compiler_params=_params(("parallel", "arbitrary")),
        name="inproj",
    )(head, x2, x2, g, w, wf)


def _split3(x):
    hi = x.astype(BF16)
    r1 = x - hi.astype(F32)
    mid = r1.astype(BF16)
    lo = (r1 - mid.astype(F32)).astype(BF16)
    return hi, mid, lo


def _prep_kernel(f_ref, bf_ref, q_ref, k_ref, qa_ref, ka_ref, carry):
    t = pl.program_id(1)
    tm = f_ref.shape[0]

    @pl.when(t == 0)
    def _():
        carry[...] = jnp.zeros_like(carry)

    x = f_ref[...] + bf_ref[...]
    logf = jnp.minimum(x, 0.0) - jnp.log1p(jnp.exp(-jnp.abs(x)))
    row = lax.broadcasted_iota(jnp.int32, (tm, 1), 0)
    valid = jnp.logical_or(t > 0, row >= PAD)
    logf = jnp.where(valid, logf, 0.0)

    hi, mid, lo = _split3(logf)
    r = lax.broadcasted_iota(jnp.int32, (tm, tm), 0)
    c = lax.broadcasted_iota(jnp.int32, (tm, tm), 1)
    tri = (r >= c).astype(BF16)
    cs = (jnp.dot(tri, hi, preferred_element_type=F32)
          + jnp.dot(tri, mid, preferred_element_type=F32)
          + jnp.dot(tri, lo, preferred_element_type=F32))
    fcum = cs + carry[0:1, :]
    carry[0:1, :] = fcum[tm - 1:tm, :]

    lane = lax.broadcasted_iota(jnp.int32, (tm, HEAD_DIM), 1)
    one = jnp.ones((tm, HEAD_DIM), F32)
    zero = jnp.zeros((tm, HEAD_DIM), F32)
    for h in range(HEADS):
        fh = fcum[:, h:h + 1] * LOG2E
        hi, mid, lo = _split3(fh)
        hi, mid, lo = hi.astype(F32), mid.astype(F32), lo.astype(F32)
        qb = jnp.where(lane == 0, hi, jnp.where(lane == 1, mid, jnp.where(lane == 2, lo,
             jnp.where(lane < 6, one, zero))))
        khi = jnp.where(valid, hi, MASK_BIG)
        kmid = jnp.where(valid, mid, 0.0)
        klo = jnp.where(valid, lo, 0.0)
        kb = jnp.where(lane < 3, one, jnp.where(lane == 3, -khi, jnp.where(lane == 4, -kmid,
             jnp.where(lane == 5, -klo, zero))))
        qa_ref[0, h, :, 0:HEAD_DIM] = q_ref[:, h * HEAD_DIM:(h + 1) * HEAD_DIM]
        qa_ref[0, h, :, HEAD_DIM:QK_AUG] = qb.astype(BF16)
        ka_ref[0, h, :, 0:HEAD_DIM] = k_ref[:, h * HEAD_DIM:(h + 1) * HEAD_DIM]
        ka_ref[0, h, :, HEAD_DIM:QK_AUG] = kb.astype(BF16)


def _prep(f, bf, z, bsz, lp, tm=ROW_ALIGN):
    nt = lp // tm
    shape = jax.ShapeDtypeStruct((bsz, HEADS, lp, QK_AUG), BF16)
    blk = pl.BlockSpec((1, HEADS, tm, QK_AUG), lambda b, t: (b, 0, t, 0))
    return pl.pallas_call(
        _prep_kernel,
        grid=(bsz, nt),
        in_specs=[
            pl.BlockSpec((tm, 128), lambda b, t: (b * nt + t, 0)),
            pl.BlockSpec((1, 128), lambda b, t: (0, 0)),
            pl.BlockSpec((tm, D_ATTN), lambda b, t: (b * nt + t, 0)),
            pl.BlockSpec((tm, D_ATTN), lambda b, t: (b * nt + t, 1)),
        ],
        out_specs=[blk, blk],
        out_shape=[shape, shape],
        scratch_shapes=[pltpu.VMEM((8, 128), F32)],
        compiler_params=_params(("parallel", "arbitrary")),
        name="attn_prep",
    )(f, bf, z, z)


def _flash_kernel(q_ref, k_ref, v_ref, o_ref, sa_ref, sb_ref, m_ref, l_ref, acc_ref):
    lp = v_ref.shape[0]
    tq, tk = ATT_TQ, ATT_TK
    n_main = (lp - ROW_ALIGN) // tq
    nt_dims = (((1,), (1,)), ((), ()))
    tn_dims = (((0,), (0,)), ((), ()))

    def scores(s_ref, q_off, nq, k_off, nk, col0):
        kc = k_ref[0, 0, pl.ds(k_off, nk), :]
        qc = q_ref[0, 0, pl.ds(q_off, nq), :]
        s_ref[0:nk, col0:col0 + nq] = lax.dot_general(kc, qc, nt_dims, preferred_element_type=F32)

    def absorb(s_ref, k_off, nk, lo, hi, mask_shift=None):
        s = s_ref[0:nk, lo:hi]
        if mask_shift is not None:
            kr = lax.broadcasted_iota(jnp.int32, s.shape, 0) + mask_shift
            qc = lax.broadcasted_iota(jnp.int32, s.shape, 1) + lo
            s = jnp.where(kr > qc, NEG, s)
        m = m_ref[0:1, lo:hi]
        m_new = jnp.maximum(m, jnp.max(s, axis=0, keepdims=True))
        alpha = jnp.exp2(m - m_new)
        p = jnp.exp2(s - m_new)
        l_ref[0:1, lo:hi] = alpha * l_ref[0:1, lo:hi] + jnp.sum(p, axis=0, keepdims=True)
        m_ref[0:1, lo:hi] = m_new
        vc = v_ref[pl.ds(k_off, nk), :]
        pv = lax.dot_general(vc, p.astype(BF16), tn_dims, preferred_element_type=F32)
        acc_ref[:, lo:hi] = alpha * acc_ref[:, lo:hi] + pv

    def reset(n):
        m_ref[0:1, 0:n] = jnp.full((1, n), NEG, F32)
        l_ref[0:1, 0:n] = jnp.zeros((1, n), F32)
        acc_ref[:, 0:n] = jnp.zeros((HEAD_DIM, n), F32)

    def finish(q_off, n):
        o_ref[pl.ds(q_off, n), :] = (acc_ref[:, 0:n] / l_ref[0:1, 0:n]).T.astype(o_ref.dtype)

    reset(ROW_ALIGN)
    scores(sa_ref, 0, ROW_ALIGN, 0, ROW_ALIGN, 0)
    absorb(sa_ref, 0, ROW_ALIGN, 0, ROW_ALIGN, mask_shift=0)
    finish(0, ROW_ALIGN)

    def tile(j, _):
        q_off = pl.multiple_of(ROW_ALIGN + j * tq, ROW_ALIGN)
        reset(tq)
        scores(sa_ref, q_off, tq, 0, ROW_ALIGN, 0)
        scores(sb_ref, q_off, tq, ROW_ALIGN, tk, 0)
        absorb(sa_ref, 0, ROW_ALIGN, 0, tq)

        def pair(p, _):
            kb = pl.multiple_of(ROW_ALIGN + p * (2 * tk), ROW_ALIGN)
            scores(sa_ref, q_off, tq, kb + tk, tk, 0)
            absorb(sb_ref, kb, tk, 0, tq)
            scores(sb_ref, q_off, tq, kb + 2 * tk, tk, 0)
            absorb(sa_ref, kb + tk, tk, 0, tq)
            return 0

        lax.fori_loop(0, j * (tq // (2 * tk)), pair, 0)
        scores(sa_ref, q_off + tk, tq - tk, q_off + tk, tk, tk)
        absorb(sb_ref, q_off, tk, 0, tq, mask_shift=0)
        absorb(sa_ref, q_off + tk, tk, tk, tq, mask_shift=tk)
        finish(q_off, tq)
        return 0

    lax.fori_loop(0, n_main, tile, 0)


def _flash(qa, ka, z, bsz, lp):
    assert (lp - ROW_ALIGN) % ATT_TQ == 0 and ATT_TQ == 2 * ATT_TK
    v_col0 = 2 * D_ATTN // HEAD_DIM
    return pl.pallas_call(
        _flash_kernel,
        grid=(bsz, HEADS),
        in_specs=[
            pl.BlockSpec((1, 1, lp, QK_AUG), lambda b, h: (b, h, 0, 0)),
            pl.BlockSpec((1, 1, lp, QK_AUG), lambda b, h: (b, h, 0, 0)),
            pl.BlockSpec((lp, HEAD_DIM), lambda b, h: (b, v_col0 + h)),
        ],
        out_specs=pl.BlockSpec((lp, HEAD_DIM), lambda b, h: (b, h)),
        out_shape=jax.ShapeDtypeStruct((bsz * lp, D_ATTN), BF16),
        scratch_shapes=[
            pltpu.VMEM((ATT_TK, ATT_TQ), F32),
            pltpu.VMEM((ATT_TK, ATT_TQ), F32),
            pltpu.VMEM((8, ATT_TQ), F32),
            pltpu.VMEM((8, ATT_TQ), F32),
            pltpu.VMEM((HEAD_DIM, ATT_TQ), F32),
        ],
        compiler_params=_params(("parallel", "parallel")),
        name="flash",
    )(qa, ka, z)


def _tile_lanes(x, reps):
    return jnp.concatenate([x] * reps, axis=1)


def _ssm_kernel(u_ref, lr_ref, li_ref, dt_ref, lrl_ref, lil_ref, dtl_ref, btr_ref, bti_ref, cr_ref, ci_ref,
                y_ref, lhs_scr, toep_scr, inj_scr, out_scr, s_scr, hp_scr):
    t = SSM_CHUNK
    lanes = SSM_SLAB * SSM_GROUP
    ns = SSM_SLAB * SSM_STATE
    nc = u_ref.shape[0] // t

    @pl.when(pl.program_id(1) == 0)
    def _build_weights():
        lr = lr_ref[0]
        li = li_ref[0]
        dt = jnp.exp(dt_ref[0])
        btr, bti = btr_ref[0], bti_ref[0]
        cr, ci = cr_ref[0], ci_ref[0]

        mag = jnp.exp(lr * dt)
        a_re = mag * jnp.cos(li * dt)
        a_im = mag * jnp.sin(li * dt)
        powers = [(jnp.ones_like(a_re), jnp.zeros_like(a_re))]
        for _ in range(t):
            pr, pi = powers[-1]
            powers.append((pr * a_re - pi * a_im, pr * a_im + pi * a_re))
        den = lr * lr + li * li
        nr = a_re - 1.0
        z_re = (nr * lr + a_im * li) / den
        z_im = (a_im * lr - nr * li) / den
        row_g = lax.broadcasted_iota(jnp.int32, (lanes, ns), 0) // SSM_GROUP
        col_g = lax.broadcasted_iota(jnp.int32, (lanes, ns), 1) // SSM_STATE
        same = row_g == col_g

        def spread(x):
            return jnp.where(same, _tile_lanes(x, SSM_SLAB), 0.0)

        caz_re, caz_im = [], []
        for d in range(t):
            pr, pi = powers[d]
            azr = pr * z_re - pi * z_im
            azi = pr * z_im + pi * z_re
            caz_re.append(cr * azr - ci * azi)
            caz_im.append(cr * azi + ci * azr)
            i = t - 1 - d
            sr = btr * azr - bti * azi
            si = btr * azi + bti * azr
            inj_scr[i * lanes:(i + 1) * lanes, 0:ns] = spread(sr).astype(BF16)
            inj_scr[i * lanes:(i + 1) * lanes, ns:2 * ns] = spread(si).astype(BF16)
            qr, qi = powers[d + 1]
            er = cr * qr - ci * qi
            ei = cr * qi + ci * qr
            out_scr[d * lanes:(d + 1) * lanes, 0:ns] = spread(er).astype(BF16)
            out_scr[d * lanes:(d + 1) * lanes, ns:2 * ns] = spread(-ei).astype(BF16)
        nt_dims = (((1,), (1,)), ((), ()))
        hp = lax.Precision.HIGHEST
        r0 = (lax.dot_general(btr, jnp.concatenate(caz_re, axis=0), nt_dims, precision=hp, preferred_element_type=F32)
              - lax.dot_general(bti, jnp.concatenate(caz_im, axis=0), nt_dims, precision=hp, preferred_element_type=F32))
        rg = lax.broadcasted_iota(jnp.int32, r0.shape, 0) // SSM_GROUP
        cg = (lax.broadcasted_iota(jnp.int32, r0.shape, 1) % lanes) // SSM_GROUP
        r0 = jnp.where(rg == cg, r0, 0.0).astype(BF16)
        toep_scr[...] = jnp.zeros_like(toep_scr)
        for i in range(t):
            toep_scr[i * lanes:(i + 1) * lanes, i * lanes:t * lanes] = r0[:, 0:(t - i) * lanes]

    for i in range(t):
        lhs_scr[:, i * lanes:(i + 1) * lanes] = u_ref[pl.ds(i, nc, stride=t), :].astype(BF16)

    s_scr[...] = jnp.dot(lhs_scr[...], inj_scr[...], preferred_element_type=F32)

    dtl = jnp.exp(dtl_ref[0])
    mag = jnp.exp(float(t) * (lrl_ref[0] * dtl))
    ang = float(t) * (lil_ref[0] * dtl)
    ar = mag * jnp.cos(ang)
    ai = mag * jnp.sin(ang)

    def step(n, carry):
        hr, hi = carry
        hp_scr[pl.ds(n, 1), 0:ns] = hr
        hp_scr[pl.ds(n, 1), ns:2 * ns] = hi
        sr = s_scr[pl.ds(n, 1), 0:ns]
        si = s_scr[pl.ds(n, 1), ns:2 * ns]
        return ar * hr - ai * hi + sr, ar * hi + ai * hr + si

    zero = jnp.zeros((1, ns), F32)
    lax.fori_loop(0, nc, step, (zero, zero), unroll=8)

    hprev = hp_scr[...].astype(BF16)
    nt_dims = (((1,), (1,)), ((), ()))
    pair = 2 * lanes
    for jj in range(t // 2):
        kdim = (jj + 1) * pair
        yj = (jnp.dot(lhs_scr[:, 0:kdim], toep_scr[0:kdim, jj * pair:(jj + 1) * pair], preferred_element_type=F32)
              + lax.dot_general(hprev, out_scr[jj * pair:(jj + 1) * pair, :], nt_dims, preferred_element_type=F32))
        y_ref[pl.ds(2 * jj, nc, stride=t), :] = yj[:, 0:lanes]
        y_ref[pl.ds(2 * jj + 1, nc, stride=t), :] = yj[:, lanes:pair]


def _ssm(u, lam_re, lam_im, log_dt, b_re, b_im, c_re, c_im, bsz, lp):
    rows, ds = u.shape
    g, p = lam_re.shape
    t, c = SSM_CHUNK, SSM_GROUP
    lanes = SSM_SLAB * c
    ns = SSM_SLAB * p
    nslab = g // SSM_SLAB
    nc = lp // t
    assert g % SSM_SLAB == 0 and lp % t == 0 and lanes == 128
    rep = lambda x: jnp.repeat(x, c, axis=0).reshape(nslab, lanes, -1)
    lane = lambda x: x.reshape(nslab, 1, ns)
    args = (
        u,
        rep(lam_re), rep(lam_im), rep(log_dt.reshape(g, 1)),
        lane(lam_re), lane(lam_im), lane(jnp.repeat(log_dt, p)),
        jnp.swapaxes(b_re, 1, 2).reshape(nslab, lanes, p), jnp.swapaxes(b_im, 1, 2).reshape(nslab, lanes, p),
        c_re.reshape(nslab, lanes, p), c_im.reshape(nslab, lanes, p),
    )
    rows_spec = pl.BlockSpec((1, lanes, p), lambda s, b: (s, 0, 0))
    lane_spec = pl.BlockSpec((1, 1, ns), lambda s, b: (s, 0, 0))
    y = pl.pallas_call(
        _ssm_kernel,
        grid=(nslab, bsz),
        in_specs=[
            pl.BlockSpec((lp, lanes), lambda s, b: (b, s)),
            rows_spec, rows_spec, pl.BlockSpec((1, lanes, 1), lambda s, b: (s, 0, 0)),
            lane_spec, lane_spec, lane_spec,
            rows_spec, rows_spec, rows_spec, rows_spec,
        ],
        out_specs=pl.BlockSpec((lp, lanes), lambda s, b: (b, s)),
        out_shape=jax.ShapeDtypeStruct((rows, ds), F32),
        scratch_shapes=[
            pltpu.VMEM((nc, t * lanes), BF16),
            pltpu.VMEM((t * lanes, t * lanes), BF16),
            pltpu.VMEM((t * lanes, 2 * ns), BF16),
            pltpu.VMEM((t * lanes, 2 * ns), BF16),
            pltpu.VMEM((nc, 2 * ns), F32),
            pltpu.VMEM((nc, 2 * ns), F32),
        ],
        compiler_params=_params(("parallel", "arbitrary")),
        name="ssm",
    )(*args)
    return y


def _mix_kernel(y_ref, u_ref, a_ref, ga_ref, gb_ref, head_ref, x_ref, dsk_ref, gffn_ref,
                wglu_ref, wao_ref, wout_ref, h_ref, n_ref, *, nt):
    d = x_ref.shape[1]
    y = y_ref[...].astype(F32) + dsk_ref[...] * u_ref[...].astype(F32)
    gy = _gelu_tanh(y).astype(BF16)
    yab = jnp.dot(gy, wglu_ref[...], preferred_element_type=F32)
    ssm_out = yab[:, 0:d] * _sigmoid(yab[:, d:2 * d])
    ao = jnp.dot(a_ref[...], wao_ref[...], preferred_element_type=F32)
    merged = (_sigmoid(ga_ref[...].astype(F32)) * ssm_out
              + _sigmoid(gb_ref[...].astype(F32)) * ao).astype(BF16)
    x = _padded_rows(pl.program_id(0), nt, head_ref, x_ref)
    h = x + jnp.dot(merged, wout_ref[...], preferred_element_type=F32)
    h_ref[...] = h
    ms = jnp.mean(h * h, axis=-1, keepdims=True)
    n_ref[...] = (h * lax.rsqrt(ms + EPS) * gffn_ref[...]).astype(n_ref.dtype)


def _mix(y, z, attn, head, x2, nt, d_skip, g_ffn, w_glu, w_ao, w_out):
    rows, ds = y.shape
    d = x2.shape[1]
    tm = ROW_ALIGN
    u_col = 3 * D_ATTN // ds
    ga_col = (3 * D_ATTN + ds) // d
    assert (3 * D_ATTN) % ds == 0 and (3 * D_ATTN + ds) % d == 0
    return pl.pallas_call(
        functools.partial(_mix_kernel, nt=nt),
        grid=(rows // tm,),
        in_specs=[
            pl.BlockSpec((tm, ds), lambda i: (i, 0)),
            pl.BlockSpec((tm, ds), lambda i: (i, u_col)),
            pl.BlockSpec((tm, D_ATTN), lambda i: (i, 0)),
            pl.BlockSpec((tm, d), lambda i: (i, ga_col)),
            pl.BlockSpec((tm, d), lambda i: (i, ga_col + 1)),
            _resident((tm, d), lambda i: (0, 0)),
            pl.BlockSpec((tm, d), lambda i: (_seq_block(i, nt), 0)),
            _resident((1, ds), lambda i: (0, 0)),
            _resident((1, d), lambda i: (0, 0)),
            _resident(w_glu.shape, lambda i: (0, 0)),
            _resident(w_ao.shape, lambda i: (0, 0)),
            _resident(w_out.shape, lambda i: (0, 0)),
        ],
        out_specs=[
            pl.BlockSpec((tm, d), lambda i: (i, 0)),
            pl.BlockSpec((tm, d), lambda i: (i, 0)),
        ],
        out_shape=[
            jax.ShapeDtypeStruct((rows, d), F32),
            jax.ShapeDtypeStruct((rows, d), BF16),
        ],
        compiler_params=_params(("parallel",)),
        name="mix_out",
    )(y, z, attn, z, z, head, x2, d_skip, g_ffn, w_glu, w_ao, w_out)


def _ffn_up_kernel(n_ref, wg_ref, wu_ref, cw_ref, cb_ref, a_ref, tail):
    tm = n_ref.shape[0]

    @pl.when(pl.program_id(1) == 0)
    def _():
        tail[...] = jnp.zeros_like(tail)

    n = n_ref[...]
    g = jnp.dot(n, wg_ref[...], preferred_element_type=F32)
    u = jnp.dot(n, wu_ref[...], preferred_element_type=F32)
    row = lax.broadcasted_iota(jnp.int32, g.shape, 0)
    prev1 = tail[CONV_WIDTH - 2:CONV_WIDTH - 1, :]
    prev2 = tail[CONV_WIDTH - 3:CONV_WIDTH - 2, :]
    g1 = jnp.where(row == 0, prev1, pltpu.roll(g, 1, 0))
    g2 = jnp.where(row == 0, prev2, jnp.where(row == 1, prev1, pltpu.roll(g, 2, 0)))
    gc = cb_ref[...] + cw_ref[0:1, :] * g2 + cw_ref[1:2, :] * g1 + cw_ref[2:3, :] * g
    tail[0:CONV_WIDTH - 1, :] = g[tm - (CONV_WIDTH - 1):tm, :]
    a_ref[...] = (gc * _sigmoid(gc) * u).astype(a_ref.dtype)


def _ffn_up(n2, w_up, conv_w, conv_b, tm=512, tn=512):
    rows, d = n2.shape
    dff = conv_w.shape[1]
    nj = dff // tn
    return pl.pallas_call(
        _ffn_up_kernel,
        grid=(nj, rows // tm),
        in_specs=[
            pl.BlockSpec((tm, d), lambda j, i: (i, 0)),
            pl.BlockSpec((d, tn), lambda j, i: (0, j)),
            pl.BlockSpec((d, tn), lambda j, i: (0, nj + j)),
            pl.BlockSpec((CONV_WIDTH, tn), lambda j, i: (0, j)),
            pl.BlockSpec((1, tn), lambda j, i: (0, j)),
        ],
        out_specs=pl.BlockSpec((tm, tn), lambda j, i: (i, j)),
        out_shape=jax.ShapeDtypeStruct((rows, dff), BF16),
        scratch_shapes=[pltpu.VMEM((8, tn), F32)],
        compiler_params=_params(("parallel", "arbitrary")),
        name="ffn_up",
    )(n2, w_up, w_up, conv_w, conv_b)


def _ffn_down_kernel(a_ref, w_ref, h_ref, g_ref, o_ref):
    h = h_ref[...] + jnp.dot(a_ref[...], w_ref[...], preferred_element_type=F32)
    ms = jnp.mean(h * h, axis=-1, keepdims=True)
    o_ref[...] = (h * lax.rsqrt(ms + EPS) * g_ref[...]).astype(o_ref.dtype)


def _ffn_down(act, w_down, h1, g_final, bsz, lp, tm=ROW_ALIGN):
    dff, d = w_down.shape
    nt_in = lp // tm
    nt_out = nt_in - ROW_ALIGN // tm
    skip = ROW_ALIGN // tm
    return pl.pallas_call(
        _ffn_down_kernel,
        grid=(bsz, nt_out),
        in_specs=[
            pl.BlockSpec((tm, dff), lambda b, t: (b * nt_in + skip + t, 0)),
            _resident((dff, d), lambda b, t: (0, 0)),
            pl.BlockSpec((tm, d), lambda b, t: (b * nt_in + skip + t, 0)),
            _resident((1, d), lambda b, t: (0, 0)),
        ],
        out_specs=pl.BlockSpec((tm, d), lambda b, t: (b * nt_out + t, 0)),
        out_shape=jax.ShapeDtypeStruct((bsz * nt_out * tm, d), F32),
        compiler_params=_params(("parallel", "parallel")),
        name="ffn_down",
    )(act, w_down, h1, g_final)


def _layer(head, x2, bsz, lp, g_mix, w_in, b_f, lam_re, lam_im, log_dt, b_re, b_im, c_re, c_im, d_skip,
           w_glu, w_attn_o, w_out, g_ffn, w_up, conv_w, conv_b, w_down):
    d = x2.shape[1]
    ds = d_skip.shape[0]
    nt = lp // ROW_ALIGN
    o_q, o_k, o_v, o_f, o_u, o_ga, o_gb = (0, D_ATTN, 2 * D_ATTN, 3 * D_ATTN, 3 * D_ATTN + HEADS,
                                            3 * D_ATTN + HEADS + ds, 3 * D_ATTN + HEADS + ds + d)
    scale = HEAD_DIM ** -0.5 * LOG2E
    w_main = jnp.concatenate([w_in[:, o_q:o_k] * scale, w_in[:, o_k:o_f], w_in[:, o_u:]], axis=1).astype(BF16)
    w_f = jnp.pad(w_in[:, o_f:o_u], ((0, 0), (0, 128 - HEADS))).astype(BF16)
    bf = jnp.pad(b_f.astype(F32), (0, 128 - HEADS)).reshape(1, 128)

    z, f, u = _inproj(head, x2, g_mix.reshape(1, d), w_main, w_f, bsz * lp, nt, 3 * D_ATTN, ds)

    qa, ka = _prep(f, bf, z, bsz, lp)
    attn = _flash(qa, ka, z, bsz, lp)
    y = _ssm(u, lam_re, lam_im, log_dt, b_re, b_im, c_re, c_im, bsz, lp)

    h1, n2 = _mix(y, z, attn, head, x2, nt, d_skip.reshape(1, ds).astype(F32), g_ffn.reshape(1, d).astype(F32),
                  w_glu.astype(BF16), w_attn_o.astype(BF16), w_out.astype(BF16))
    act = _ffn_up(n2, w_up.astype(BF16), conv_w.astype(F32), conv_b.reshape(1, -1).astype(F32))
    return act, h1


def kernel(x, meta, g_mix, w_in, b_f, lam_re, lam_im, log_dt, b_re, b_im, c_re, c_im, d_skip,
           w_glu, w_attn_o, w_out, g_ffn, w_up, conv_w, conv_b, w_down, g_final):
    bsz, seq, d = x.shape
    depth = g_mix.shape[0]
    assert depth == 1 and meta.shape[0] == N_META and seq % ROW_ALIGN == 0
    lp = seq + ROW_ALIGN
    assert (bsz * lp) % 512 == 0
    head = jnp.concatenate([jnp.zeros((PAD, d), x.dtype), meta.astype(x.dtype)], axis=0)
    act, h1 = _layer(head, x.reshape(bsz * seq, d), bsz, lp, g_mix[0], w_in[0], b_f[0], lam_re[0], lam_im[0], log_dt[0],
                     b_re[0], b_im[0], c_re[0], c_im[0], d_skip[0], w_glu[0], w_attn_o[0], w_out[0],
                     g_ffn[0], w_up[0], conv_w[0], conv_b[0], w_down[0])
    out = _ffn_down(act, w_down[0].astype(BF16), h1, g_final.reshape(1, d).astype(F32), bsz, lp)
    return out.reshape(bsz, seq, d)
```

```python
import functools
import math

import jax
import jax.numpy as jnp
from jax import lax
from jax.experimental import pallas as pl
from jax.experimental.pallas import tpu as pltpu

N_META = 16
HEADS = 8
HEAD_DIM = 128
D_ATTN = HEADS * HEAD_DIM
SSM_GROUP = 16
SSM_STATE = 64
SSM_CHUNK = 16
SSM_SLAB = 8
CONV_WIDTH = 3
EPS = 1e-6

ROW_ALIGN = 256
PAD = ROW_ALIGN - N_META
ATT_TQ = 1024
ATT_TK = 512
QK_AUG = 2 * HEAD_DIM
LOG2E = math.log2(math.e)
MASK_BIG = 1e30
NEG = -3e38
VMEM_LIMIT = 56 * 1024 * 1024

F32 = jnp.float32
BF16 = jnp.bfloat16


def _sigmoid(x):
    return 1.0 / (1.0 + jnp.exp(-x))


def _gelu_tanh(x):
    c = math.sqrt(2.0 / math.pi)
    return 0.5 * x * (1.0 + jnp.tanh(c * (x + 0.044715 * (x * x * x))))


def _params(sem, limit=VMEM_LIMIT):
    return pltpu.CompilerParams(dimension_semantics=sem, vmem_limit_bytes=limit)


def _resident(shape, index_map):
    return pl.BlockSpec(shape, index_map, pipeline_mode=pl.Buffered(1))


def _win_pack_kernel(w_ref, o_ref, f_ref, *, o_f, o_u, q_scale):
    n_in = w_ref.shape[1]
    rows = w_ref.shape[0]
    o_ref[:, 0:D_ATTN] = (w_ref[:, 0:D_ATTN] * q_scale).astype(o_ref.dtype)
    o_ref[:, D_ATTN:o_f] = w_ref[:, D_ATTN:o_f].astype(o_ref.dtype)
    o_ref[:, o_f:o_f + (n_in - o_u)] = w_ref[:, o_u:n_in].astype(o_ref.dtype)
    f_ref[...] = jnp.concatenate(
        [w_ref[:, o_f:o_u], jnp.zeros((rows, f_ref.shape[1] - (o_u - o_f)), F32)], axis=1).astype(f_ref.dtype)


def _win_pack(w_in, o_f, o_u, q_scale, tr=256):
    d, n_in = w_in.shape
    n_out = n_in - (o_u - o_f)
    return pl.pallas_call(
        functools.partial(_win_pack_kernel, o_f=o_f, o_u=o_u, q_scale=q_scale),
        grid=(d // tr,),
        in_specs=[pl.BlockSpec((tr, n_in), lambda i: (i, 0))],
        out_specs=[pl.BlockSpec((tr, n_out), lambda i: (i, 0)), pl.BlockSpec((tr, 128), lambda i: (i, 0))],
        out_shape=[jax.ShapeDtypeStruct((d, n_out), BF16), jax.ShapeDtypeStruct((d, 128), BF16)],
        compiler_params=_params(("parallel",)),
        name="win_pack",
    )(w_in)


def _seq_block(t, nt):
    return (t // nt) * (nt - 1) + jnp.maximum(t % nt - 1, 0)


def _padded_rows(t, nt, head_ref, x_ref):
    return jnp.where(t % nt == 0, head_ref[...], x_ref[...])


def _split3(x):
    hi = x.astype(BF16)
    r1 = x - hi.astype(F32)
    mid = r1.astype(BF16)
    lo = (r1 - mid.astype(F32)).astype(BF16)
    return hi, mid, lo


def _forget_bias(f, bf_ref, t, nt, carry, qb_ref, kb_ref, row0):
    tm = f.shape[0]
    first = t % nt == 0
    x = f + bf_ref[...]
    logf = jnp.minimum(x, 0.0) - jnp.log1p(jnp.exp(-jnp.abs(x)))
    row = lax.broadcasted_iota(jnp.int32, (tm, 1), 0)
    valid = jnp.logical_or(jnp.logical_not(first), row >= PAD)
    logf = jnp.where(valid, logf, 0.0)
    r = lax.broadcasted_iota(jnp.int32, (tm, tm), 0)
    c = lax.broadcasted_iota(jnp.int32, (tm, tm), 1)
    tri = (r >= c).astype(BF16)
    cs3 = jnp.dot(tri, jnp.concatenate(_split3(logf), axis=1), preferred_element_type=F32)
    w = f.shape[1]
    fcum = cs3[:, 0:w] + cs3[:, w:2 * w] + cs3[:, 2 * w:3 * w] + jnp.where(first, 0.0, carry[0:1, :])
    carry[0:1, :] = fcum[tm - 1:tm, :]

    lane = lax.broadcasted_iota(jnp.int32, (tm, HEAD_DIM), 1)
    one = jnp.ones((tm, HEAD_DIM), F32)
    zero = jnp.zeros((tm, HEAD_DIM), F32)
    for h in range(HEADS):
        hi, mid, lo = (p.astype(F32) for p in _split3(fcum[:, h:h + 1] * LOG2E))
        qb = jnp.where(lane == 0, hi, jnp.where(lane == 1, mid, jnp.where(lane == 2, lo,
             jnp.where(lane < 6, one, zero))))
        khi = jnp.where(valid, hi, MASK_BIG)
        kmid = jnp.where(valid, mid, 0.0)
        klo = jnp.where(valid, lo, 0.0)
        kb = jnp.where(lane < 3, one, jnp.where(lane == 3, -khi, jnp.where(lane == 4, -kmid,
             jnp.where(lane == 5, -klo, zero))))
        qb_ref[h, row0:row0 + tm, :] = qb.astype(qb_ref.dtype)
        kb_ref[h, row0:row0 + tm, :] = kb.astype(kb_ref.dtype)


def _inproj_kernel(head_ref, xa_ref, xb_ref, g_ref, w_ref, wf_ref, bf_ref, z_ref, u_ref, qb_ref, kb_ref,
                   n_scr, carry, *, nt, u_blk, u_lo):
    i = pl.program_id(0)
    j = pl.program_id(1)

    @pl.when(j == 0)
    def _():
        x = jnp.concatenate([_padded_rows(2 * i, nt, head_ref, xa_ref),
                             _padded_rows(2 * i + 1, nt, head_ref, xb_ref)], axis=0)
        ms = jnp.mean(x * x, axis=-1, keepdims=True)
        n = (x * lax.rsqrt(ms + EPS) * g_ref[...]).astype(BF16)
        n_scr[...] = n
        f = jnp.dot(n, wf_ref[...], preferred_element_type=F32)

        @pl.when(i == 0)
        def _():
            carry[...] = jnp.zeros_like(carry)

        for blk in range(2):
            _forget_bias(f[blk * ROW_ALIGN:(blk + 1) * ROW_ALIGN], bf_ref, 2 * i + blk, nt, carry,
                         qb_ref, kb_ref, blk * ROW_ALIGN)

    acc = jnp.dot(n_scr[...], w_ref[...], preferred_element_type=F32)
    z_ref[...] = acc.astype(z_ref.dtype)

    @pl.when(j == u_blk)
    def _():
        u_ref[...] = acc[:, u_lo:u_lo + u_ref.shape[1]]


def _inproj(head, x2, g, w, wf, bf, rows, nt, u_off, u_width, tn=2048):
    d = x2.shape[1]
    tm = 2 * ROW_ALIGN
    n_out = w.shape[1]
    u_blk, u_lo = divmod(u_off, tn)
    assert u_lo + u_width <= tn and rows % tm == 0
    bias_spec = pl.BlockSpec((HEADS, tm, HEAD_DIM), lambda i, j: (0, i, 0))
    bias_shape = jax.ShapeDtypeStruct((HEADS, rows, HEAD_DIM), BF16)
    return pl.pallas_call(
        functools.partial(_inproj_kernel, nt=nt, u_blk=u_blk, u_lo=u_lo),
        grid=(rows // tm, n_out // tn),
        in_specs=[
            _resident((ROW_ALIGN, d), lambda i, j: (0, 0)),
            pl.BlockSpec((ROW_ALIGN, d), lambda i, j: (_seq_block(2 * i, nt), 0)),
            pl.BlockSpec((ROW_ALIGN, d), lambda i, j: (_seq_block(2 * i + 1, nt), 0)),
            pl.BlockSpec((1, d), lambda i, j: (0, 0)),
            pl.BlockSpec((d, tn), lambda i, j: (0, j)),
            pl.BlockSpec((d, 128), lambda i, j: (0, 0)),
            pl.BlockSpec((1, 128), lambda i, j: (0, 0)),
        ],
        out_specs=[
            pl.BlockSpec((tm, tn), lambda i, j: (i, j)),
            pl.BlockSpec((tm, u_width), lambda i, j: (i, 0)),
            bias_spec, bias_spec,
        ],
        out_shape=[
            jax.ShapeDtypeStruct((rows, n_out), BF16),
            jax.ShapeDtypeStruct((rows, u_width), F32),
            bias_shape, bias_shape,
        ],
        scratch_shapes=[pltpu.VMEM((tm, d), BF16), pltpu.VMEM((8, 128), F32)],
        compiler_params=_params(("arbitrary", "arbitrary")),
        name="inproj",
    )(head, x2, x2, g, w, wf, bf)


def _flash_kernel(q_ref, k_ref, v_ref, qb_ref, kb_ref, o_ref, qa_ref, ka_ref, sa_ref, sb_ref, m_ref, l_ref, acc_ref):
    lp = v_ref.shape[0]
    tq, tk = ATT_TQ, ATT_TK
    n_main = (lp - ROW_ALIGN) // tq
    nt_dims = (((1,), (1,)), ((), ()))
    tn_dims = (((0,), (0,)), ((), ()))

    qa_ref[:, 0:HEAD_DIM] = q_ref[...]
    qa_ref[:, HEAD_DIM:QK_AUG] = qb_ref[0]
    ka_ref[:, 0:HEAD_DIM] = k_ref[...]
    ka_ref[:, HEAD_DIM:QK_AUG] = kb_ref[0]

    def scores(s_ref, q_off, nq, k_off, nk, col0):
        kc = ka_ref[pl.ds(k_off, nk), :]
        qc = qa_ref[pl.ds(q_off, nq), :]
        s_ref[0:nk, col0:col0 + nq] = lax.dot_general(kc, qc, nt_dims, preferred_element_type=F32)

    def absorb(s_ref, k_off, nk, lo, hi, mask_shift=None):
        s = s_ref[0:nk, lo:hi]
        if mask_shift is not None:
            kr = lax.broadcasted_iota(jnp.int32, s.shape, 0) + mask_shift
            qc = lax.broadcasted_iota(jnp.int32, s.shape, 1) + lo
            s = jnp.where(kr > qc, NEG, s)
        m = m_ref[0:1, lo:hi]
        m_new = jnp.maximum(m, jnp.max(s, axis=0, keepdims=True))
        alpha = jnp.exp2(m - m_new)
        p = jnp.exp2(s - m_new)
        l_ref[0:1, lo:hi] = alpha * l_ref[0:1, lo:hi] + jnp.sum(p, axis=0, keepdims=True)
        m_ref[0:1, lo:hi] = m_new
        vc = v_ref[pl.ds(k_off, nk), :]
        pv = lax.dot_general(vc, p.astype(BF16), tn_dims, preferred_element_type=F32)
        acc_ref[:, lo:hi] = alpha * acc_ref[:, lo:hi] + pv

    def reset(n):
        m_ref[0:1, 0:n] = jnp.full((1, n), NEG, F32)
        l_ref[0:1, 0:n] = jnp.zeros((1, n), F32)
        acc_ref[:, 0:n] = jnp.zeros((HEAD_DIM, n), F32)

    def finish(q_off, n):
        o_ref[pl.ds(q_off, n), :] = (acc_ref[:, 0:n] / l_ref[0:1, 0:n]).T.astype(o_ref.dtype)

    reset(ROW_ALIGN)
    scores(sa_ref, 0, ROW_ALIGN, 0, ROW_ALIGN, 0)
    absorb(sa_ref, 0, ROW_ALIGN, 0, ROW_ALIGN, mask_shift=0)
    finish(0, ROW_ALIGN)

    def tile(j, _):
        q_off = pl.multiple_of(ROW_ALIGN + j * tq, ROW_ALIGN)
        reset(tq)
        scores(sa_ref, q_off, tq, 0, ROW_ALIGN, 0)
        scores(sb_ref, q_off, tq, ROW_ALIGN, tk, 0)
        absorb(sa_ref, 0, ROW_ALIGN, 0, tq)

        def pair(p, _):
            kb = pl.multiple_of(ROW_ALIGN + p * (2 * tk), ROW_ALIGN)
            scores(sa_ref, q_off, tq, kb + tk, tk, 0)
            absorb(sb_ref, kb, tk, 0, tq)
            scores(sb_ref, q_off, tq, kb + 2 * tk, tk, 0)
            absorb(sa_ref, kb + tk, tk, 0, tq)
            return 0

        lax.fori_loop(0, j * (tq // (2 * tk)), pair, 0)
        scores(sa_ref, q_off + tk, tq - tk, q_off + tk, tk, tk)
        absorb(sb_ref, q_off, tk, 0, tq, mask_shift=0)
        absorb(sa_ref, q_off + tk, tk, tk, tq, mask_shift=tk)
        finish(q_off, tq)
        return 0

    lax.fori_loop(0, n_main, tile, 0)


def _flash(z, qb, kb, bsz, lp):
    assert (lp - ROW_ALIGN) % ATT_TQ == 0 and ATT_TQ == 2 * ATT_TK
    bias_spec = pl.BlockSpec((1, lp, HEAD_DIM), lambda b, h: (h, b, 0))
    return pl.pallas_call(
        _flash_kernel,
        grid=(bsz, HEADS),
        in_specs=[
            pl.BlockSpec((lp, HEAD_DIM), lambda b, h: (b, h)),
            pl.BlockSpec((lp, HEAD_DIM), lambda b, h: (b, HEADS + h)),
            pl.BlockSpec((lp, HEAD_DIM), lambda b, h: (b, 2 * HEADS + h)),
            bias_spec, bias_spec,
        ],
        out_specs=pl.BlockSpec((lp, HEAD_DIM), lambda b, h: (b, h)),
        out_shape=jax.ShapeDtypeStruct((bsz * lp, D_ATTN), BF16),
        scratch_shapes=[
            pltpu.VMEM((lp, QK_AUG), BF16),
            pltpu.VMEM((lp, QK_AUG), BF16),
            pltpu.VMEM((ATT_TK, ATT_TQ), F32),
            pltpu.VMEM((ATT_TK, ATT_TQ), F32),
            pltpu.VMEM((8, ATT_TQ), F32),
            pltpu.VMEM((8, ATT_TQ), F32),
            pltpu.VMEM((HEAD_DIM, ATT_TQ), F32),
        ],
        compiler_params=_params(("parallel", "parallel")),
        name="flash",
    )(z, z, z, qb, kb)


def _tile_lanes(x, reps):
    return jnp.concatenate([x] * reps, axis=1)


def _ssm_kernel(u_ref, lr_ref, li_ref, dt_ref, lrl_ref, lil_ref, dtl_ref, btr_ref, bti_ref, cr_ref, ci_ref,
                y_ref, lhs_scr, toep_scr, inj_scr, out_scr, s_scr, hp_scr):
    t = SSM_CHUNK
    lanes = SSM_SLAB * SSM_GROUP
    ns = SSM_SLAB * SSM_STATE
    nc = u_ref.shape[0] // t

    @pl.when(pl.program_id(1) == 0)
    def _build_weights():
        lr = lr_ref[0]
        li = li_ref[0]
        dt = jnp.exp(dt_ref[0])
        btr, bti = btr_ref[0], bti_ref[0]
        cr, ci = cr_ref[0], ci_ref[0]

        mag = jnp.exp(lr * dt)
        a_re = mag * jnp.cos(li * dt)
        a_im = mag * jnp.sin(li * dt)
        powers = [(jnp.ones_like(a_re), jnp.zeros_like(a_re))]
        for _ in range(t):
            pr, pi = powers[-1]
            powers.append((pr * a_re - pi * a_im, pr * a_im + pi * a_re))
        den = lr * lr + li * li
        nr = a_re - 1.0
        z_re = (nr * lr + a_im * li) / den
        z_im = (a_im * lr - nr * li) / den
        row_g = lax.broadcasted_iota(jnp.int32, (lanes, ns), 0) // SSM_GROUP
        col_g = lax.broadcasted_iota(jnp.int32, (lanes, ns), 1) // SSM_STATE
        same = row_g == col_g

        def spread(x):
            return jnp.where(same, _tile_lanes(x, SSM_SLAB), 0.0)

        caz_re, caz_im = [], []
        for d in range(t):
            pr, pi = powers[d]
            azr = pr * z_re - pi * z_im
            azi = pr * z_im + pi * z_re
            caz_re.append(cr * azr - ci * azi)
            caz_im.append(cr * azi + ci * azr)
            i = t - 1 - d
            sr = btr * azr - bti * azi
            si = btr * azi + bti * azr
            inj_scr[i * lanes:(i + 1) * lanes, 0:ns] = spread(sr).astype(BF16)
            inj_scr[i * lanes:(i + 1) * lanes, ns:2 * ns] = spread(si).astype(BF16)
            qr, qi = powers[d + 1]
            er = cr * qr - ci * qi
            ei = cr * qi + ci * qr
            out_scr[d * lanes:(d + 1) * lanes, 0:ns] = spread(er).astype(BF16)
            out_scr[d * lanes:(d + 1) * lanes, ns:2 * ns] = spread(-ei).astype(BF16)
        nt_dims = (((1,), (1,)), ((), ()))
        hp = lax.Precision.HIGHEST
        r0 = (lax.dot_general(btr, jnp.concatenate(caz_re, axis=0), nt_dims, precision=hp, preferred_element_type=F32)
              - lax.dot_general(bti, jnp.concatenate(caz_im, axis=0), nt_dims, precision=hp, preferred_element_type=F32))
        rg = lax.broadcasted_iota(jnp.int32, r0.shape, 0) // SSM_GROUP
        cg = (lax.broadcasted_iota(jnp.int32, r0.shape, 1) % lanes) // SSM_GROUP
        r0 = jnp.where(rg == cg, r0, 0.0).astype(BF16)
        toep_scr[...] = jnp.zeros_like(toep_scr)
        for i in range(t):
            toep_scr[i * lanes:(i + 1) * lanes, i * lanes:t * lanes] = r0[:, 0:(t - i) * lanes]

    for i in range(t):
        lhs_scr[:, i * lanes:(i + 1) * lanes] = u_ref[pl.ds(i, nc, stride=t), :].astype(BF16)

    s_scr[...] = jnp.dot(lhs_scr[...], inj_scr[...], preferred_element_type=F32)

    dtl = jnp.exp(dtl_ref[0])
    mag = jnp.exp(float(t) * (lrl_ref[0] * dtl))
    ang = float(t) * (lil_ref[0] * dtl)
    ar = mag * jnp.cos(ang)
    ai = mag * jnp.sin(ang)

    def step(n, carry):
        hr, hi = carry
        hp_scr[pl.ds(n, 1), 0:ns] = hr
        hp_scr[pl.ds(n, 1), ns:2 * ns] = hi
        sr = s_scr[pl.ds(n, 1), 0:ns]
        si = s_scr[pl.ds(n, 1), ns:2 * ns]
        return ar * hr - ai * hi + sr, ar * hi + ai * hr + si

    zero = jnp.zeros((1, ns), F32)
    lax.fori_loop(0, nc, step, (zero, zero), unroll=8)

    hprev = hp_scr[...].astype(BF16)
    nt_dims = (((1,), (1,)), ((), ()))
    pair = 2 * lanes
    for jj in range(t // 2):
        kdim = (jj + 1) * pair
        yj = (jnp.dot(lhs_scr[:, 0:kdim], toep_scr[0:kdim, jj * pair:(jj + 1) * pair], preferred_element_type=F32)
              + lax.dot_general(hprev, out_scr[jj * pair:(jj + 1) * pair, :], nt_dims, preferred_element_type=F32))
        y_ref[pl.ds(2 * jj, nc, stride=t), :] = yj[:, 0:lanes]
        y_ref[pl.ds(2 * jj + 1, nc, stride=t), :] = yj[:, lanes:pair]


def _ssm(u, lam_re, lam_im, log_dt, b_re, b_im, c_re, c_im, bsz, lp):
    rows, ds = u.shape
    g, p = lam_re.shape
    t, c = SSM_CHUNK, SSM_GROUP
    lanes = SSM_SLAB * c
    ns = SSM_SLAB * p
    nslab = g // SSM_SLAB
    nc = lp // t
    assert g % SSM_SLAB == 0 and lp % t == 0 and lanes == 128
    rep = lambda x: jnp.repeat(x, c, axis=0).reshape(nslab, lanes, -1)
    lane = lambda x: x.reshape(nslab, 1, ns)
    args = (
        u,
        rep(lam_re), rep(lam_im), rep(log_dt.reshape(g, 1)),
        lane(lam_re), lane(lam_im), lane(jnp.repeat(log_dt, p)),
        jnp.swapaxes(b_re, 1, 2).reshape(nslab, lanes, p), jnp.swapaxes(b_im, 1, 2).reshape(nslab, lanes, p),
        c_re.reshape(nslab, lanes, p), c_im.reshape(nslab, lanes, p),
    )
    rows_spec = pl.BlockSpec((1, lanes, p), lambda s, b: (s, 0, 0))
    lane_spec = pl.BlockSpec((1, 1, ns), lambda s, b: (s, 0, 0))
    y = pl.pallas_call(
        _ssm_kernel,
        grid=(nslab, bsz),
        in_specs=[
            pl.BlockSpec((lp, lanes), lambda s, b: (b, s)),
            rows_spec, rows_spec, pl.BlockSpec((1, lanes, 1), lambda s, b: (s, 0, 0)),
            lane_spec, lane_spec, lane_spec,
            rows_spec, rows_spec, rows_spec, rows_spec,
        ],
        out_specs=pl.BlockSpec((lp, lanes), lambda s, b: (b, s)),
        out_shape=jax.ShapeDtypeStruct((rows, ds), F32),
        scratch_shapes=[
            pltpu.VMEM((nc, t * lanes), BF16),
            pltpu.VMEM((t * lanes, t * lanes), BF16),
            pltpu.VMEM((t * lanes, 2 * ns), BF16),
            pltpu.VMEM((t * lanes, 2 * ns), BF16),
            pltpu.VMEM((nc, 2 * ns), F32),
            pltpu.VMEM((nc, 2 * ns), F32),
        ],
        compiler_params=_params(("parallel", "arbitrary")),
        name="ssm",
    )(*args)
    return y


def _mix_kernel(y_ref, u_ref, a_ref, ga_ref, gb_ref, head_ref, x_ref, dsk_ref, gffn_ref,
                wglu_ref, wao_ref, wout_ref, h_ref, n_ref, *, nt):
    d = x_ref.shape[1]
    y = y_ref[...].astype(F32) + dsk_ref[...] * u_ref[...].astype(F32)
    gy = _gelu_tanh(y).astype(BF16)
    yab = jnp.dot(gy, wglu_ref[...], preferred_element_type=F32)
    ssm_out = yab[:, 0:d] * _sigmoid(yab[:, d:2 * d])
    ao = jnp.dot(a_ref[...], wao_ref[...], preferred_element_type=F32)
    merged = (_sigmoid(ga_ref[...].astype(F32)) * ssm_out
              + _sigmoid(gb_ref[...].astype(F32)) * ao).astype(BF16)
    x = _padded_rows(pl.program_id(0), nt, head_ref, x_ref)
    h = x + jnp.dot(merged, wout_ref[...], preferred_element_type=F32)
    h_ref[...] = h
    ms = jnp.mean(h * h, axis=-1, keepdims=True)
    n_ref[...] = (h * lax.rsqrt(ms + EPS) * gffn_ref[...]).astype(n_ref.dtype)


def _mix(y, z, attn, head, x2, nt, d_skip, g_ffn, w_glu, w_ao, w_out):
    rows, ds = y.shape
    d = x2.shape[1]
    tm = ROW_ALIGN
    u_col = 3 * D_ATTN // ds
    ga_col = (3 * D_ATTN + ds) // d
    assert (3 * D_ATTN) % ds == 0 and (3 * D_ATTN + ds) % d == 0
    return pl.pallas_call(
        functools.partial(_mix_kernel, nt=nt),
        grid=(rows // tm,),
        in_specs=[
            pl.BlockSpec((tm, ds), lambda i: (i, 0)),
            pl.BlockSpec((tm, ds), lambda i: (i, u_col)),
            pl.BlockSpec((tm, D_ATTN), lambda i: (i, 0)),
            pl.BlockSpec((tm, d), lambda i: (i, ga_col)),
            pl.BlockSpec((tm, d), lambda i: (i, ga_col + 1)),
            _resident((tm, d), lambda i: (0, 0)),
            pl.BlockSpec((tm, d), lambda i: (_seq_block(i, nt), 0)),
            _resident((1, ds), lambda i: (0, 0)),
            _resident((1, d), lambda i: (0, 0)),
            _resident(w_glu.shape, lambda i: (0, 0)),
            _resident(w_ao.shape, lambda i: (0, 0)),
            _resident(w_out.shape, lambda i: (0, 0)),
        ],
        out_specs=[
            pl.BlockSpec((tm, d), lambda i: (i, 0)),
            pl.BlockSpec((tm, d), lambda i: (i, 0)),
        ],
        out_shape=[
            jax.ShapeDtypeStruct((rows, d), F32),
            jax.ShapeDtypeStruct((rows, d), BF16),
        ],
        compiler_params=_params(("parallel",)),
        name="mix_out",
    )(y, z, attn, z, z, head, x2, d_skip, g_ffn, w_glu, w_ao, w_out)


def _ffn_up_kernel(n_ref, wg_ref, wu_ref, cw_ref, cb_ref, a_ref, tail, wg_scr, wu_scr):
    tm = n_ref.shape[0]

    @pl.when(pl.program_id(1) == 0)
    def _():
        tail[...] = jnp.zeros_like(tail)
        wg_scr[...] = wg_ref[...].astype(wg_scr.dtype)
        wu_scr[...] = wu_ref[...].astype(wu_scr.dtype)

    n = n_ref[...]
    g = jnp.dot(n, wg_scr[...], preferred_element_type=F32)
    u = jnp.dot(n, wu_scr[...], preferred_element_type=F32)
    row = lax.broadcasted_iota(jnp.int32, g.shape, 0)
    prev1 = tail[CONV_WIDTH - 2:CONV_WIDTH - 1, :]
    prev2 = tail[CONV_WIDTH - 3:CONV_WIDTH - 2, :]
    g1 = jnp.where(row == 0, prev1, pltpu.roll(g, 1, 0))
    g2 = jnp.where(row == 0, prev2, jnp.where(row == 1, prev1, pltpu.roll(g, 2, 0)))
    gc = cb_ref[...] + cw_ref[0:1, :] * g2 + cw_ref[1:2, :] * g1 + cw_ref[2:3, :] * g
    tail[0:CONV_WIDTH - 1, :] = g[tm - (CONV_WIDTH - 1):tm, :]
    a_ref[...] = (gc * _sigmoid(gc) * u).astype(a_ref.dtype)


def _ffn_up(n2, w_up, conv_w, conv_b, tm=512, tn=512):
    rows, d = n2.shape
    dff = conv_w.shape[1]
    nj = dff // tn
    return pl.pallas_call(
        _ffn_up_kernel,
        grid=(nj, rows // tm),
        in_specs=[
            pl.BlockSpec((tm, d), lambda j, i: (i, 0)),
            pl.BlockSpec((d, tn), lambda j, i: (0, j)),
            pl.BlockSpec((d, tn), lambda j, i: (0, nj + j)),
            pl.BlockSpec((CONV_WIDTH, tn), lambda j, i: (0, j)),
            pl.BlockSpec((1, tn), lambda j, i: (0, j)),
        ],
        out_specs=pl.BlockSpec((tm, tn), lambda j, i: (i, j)),
        out_shape=jax.ShapeDtypeStruct((rows, dff), BF16),
        scratch_shapes=[pltpu.VMEM((8, tn), F32), pltpu.VMEM((d, tn), BF16), pltpu.VMEM((d, tn), BF16)],
        compiler_params=_params(("parallel", "arbitrary")),
        name="ffn_up",
    )(n2, w_up, w_up, conv_w, conv_b)


def _ffn_down_kernel(a_ref, w_ref, h_ref, g_ref, o_ref):
    h = h_ref[...] + jnp.dot(a_ref[...], w_ref[...], preferred_element_type=F32)
    ms = jnp.mean(h * h, axis=-1, keepdims=True)
    o_ref[...] = (h * lax.rsqrt(ms + EPS) * g_ref[...]).astype(o_ref.dtype)


def _ffn_down(act, w_down, h1, g_final, bsz, lp, tm=ROW_ALIGN):
    dff, d = w_down.shape
    nt_in = lp // tm
    nt_out = nt_in - ROW_ALIGN // tm
    skip = ROW_ALIGN // tm
    return pl.pallas_call(
        _ffn_down_kernel,
        grid=(bsz, nt_out),
        in_specs=[
            pl.BlockSpec((tm, dff), lambda b, t: (b * nt_in + skip + t, 0)),
            _resident((dff, d), lambda b, t: (0, 0)),
            pl.BlockSpec((tm, d), lambda b, t: (b * nt_in + skip + t, 0)),
            _resident((1, d), lambda b, t: (0, 0)),
        ],
        out_specs=pl.BlockSpec((tm, d), lambda b, t: (b * nt_out + t, 0)),
        out_shape=jax.ShapeDtypeStruct((bsz * nt_out * tm, d), F32),
        compiler_params=_params(("parallel", "parallel")),
        name="ffn_down",
    )(act, w_down, h1, g_final)


def _layer(head, x2, bsz, lp, g_mix, w_in, b_f, lam_re, lam_im, log_dt, b_re, b_im, c_re, c_im, d_skip,
           w_glu, w_attn_o, w_out, g_ffn, w_up, conv_w, conv_b, w_down):
    d = x2.shape[1]
    ds = d_skip.shape[0]
    nt = lp // ROW_ALIGN
    o_q, o_k, o_v, o_f, o_u, o_ga, o_gb = (0, D_ATTN, 2 * D_ATTN, 3 * D_ATTN, 3 * D_ATTN + HEADS,
                                            3 * D_ATTN + HEADS + ds, 3 * D_ATTN + HEADS + ds + d)
    w_main, w_f = _win_pack(w_in, o_f, o_u, HEAD_DIM ** -0.5 * LOG2E)
    bf = jnp.pad(b_f.astype(F32), (0, 128 - HEADS)).reshape(1, 128)

    z, u, qb, kb = _inproj(head, x2, g_mix.reshape(1, d), w_main, w_f, bf, bsz * lp, nt, 3 * D_ATTN, ds)
    attn = _flash(z, qb, kb, bsz, lp)
    y = _ssm(u, lam_re, lam_im, log_dt, b_re, b_im, c_re, c_im, bsz, lp)

    h1, n2 = _mix(y, z, attn, head, x2, nt, d_skip.reshape(1, ds).astype(F32), g_ffn.reshape(1, d).astype(F32),
                  w_glu.astype(BF16), w_attn_o.astype(BF16), w_out.astype(BF16))
    act = _ffn_up(n2, w_up.astype(F32), conv_w.astype(F32), conv_b.reshape(1, -1).astype(F32))
    return act, h1


def kernel(x, meta, g_mix, w_in, b_f, lam_re, lam_im, log_dt, b_re, b_im, c_re, c_im, d_skip,
           w_glu, w_attn_o, w_out, g_ffn, w_up, conv_w, conv_b, w_down, g_final):
    bsz, seq, d = x.shape
    depth = g_mix.shape[0]
    assert depth == 1 and meta.shape[0] == N_META and seq % ROW_ALIGN == 0
    lp = seq + ROW_ALIGN
    assert (bsz * lp) % 512 == 0
    head = jnp.concatenate([jnp.zeros((PAD, d), x.dtype), meta.astype(x.dtype)], axis=0)
    act, h1 = _layer(head, x.reshape(bsz * seq, d), bsz, lp, g_mix[0], w_in[0], b_f[0], lam_re[0], lam_im[0], log_dt[0],
                     b_re[0], b_im[0], c_re[0], c_im[0], d_skip[0], w_glu[0], w_attn_o[0], w_out[0],
                     g_ffn[0], w_up[0], conv_w[0], conv_b[0], w_down[0])
    out = _ffn_down(act, w_down[0].astype(BF16), h1, g_final.reshape(1, d).astype(F32), bsz, lp)
    return out.reshape(bsz, seq, d)
```

```python
import functools
import math

import jax
import jax.numpy as jnp
from jax import lax
from jax.experimental import pallas as pl
from jax.experimental.pallas import tpu as pltpu

N_META = 16
HEADS = 8
HEAD_DIM = 128
D_ATTN = HEADS * HEAD_DIM
SSM_GROUP = 16
SSM_STATE = 64
SSM_CHUNK = 16
SSM_SLAB = 8
CONV_WIDTH = 3
EPS = 1e-6

ROW_ALIGN = 256
PAD = ROW_ALIGN - N_META
ATT_TQ = 1024
ATT_TK = 512
QK_AUG = 2 * HEAD_DIM
LOG2E = math.log2(math.e)
MASK_BIG = 1e30
NEG = -3e38
VMEM_LIMIT = 56 * 1024 * 1024

F32 = jnp.float32
BF16 = jnp.bfloat16


def _sigmoid(x):
    return 1.0 / (1.0 + jnp.exp(-x))


def _gelu_tanh(x):
    c = math.sqrt(2.0 / math.pi)
    return 0.5 * x * (1.0 + jnp.tanh(c * (x + 0.044715 * (x * x * x))))


def _params(sem, limit=VMEM_LIMIT):
    return pltpu.CompilerParams(dimension_semantics=sem, vmem_limit_bytes=limit)


def _resident(shape, index_map):
    return pl.BlockSpec(shape, index_map, pipeline_mode=pl.Buffered(1))


def _win_pack_kernel(wt_ref, o_ref, f_ref, *, o_f, o_u, q_scale):
    n_in, tr = wt_ref.shape
    o_ref[:, 0:D_ATTN] = (wt_ref[0:D_ATTN, :] * q_scale).T.astype(o_ref.dtype)
    o_ref[:, D_ATTN:o_f] = wt_ref[D_ATTN:o_f, :].T.astype(o_ref.dtype)
    o_ref[:, o_f:o_f + (n_in - o_u)] = wt_ref[o_u:n_in, :].T.astype(o_ref.dtype)
    wf = jnp.concatenate([wt_ref[o_f:o_u, :], jnp.zeros((f_ref.shape[1] - (o_u - o_f), tr), F32)], axis=0)
    f_ref[...] = wf.T.astype(f_ref.dtype)


def _win_pack(w_in, o_f, o_u, q_scale, tr=256):
    d, n_in = w_in.shape
    n_out = n_in - (o_u - o_f)
    return pl.pallas_call(
        functools.partial(_win_pack_kernel, o_f=o_f, o_u=o_u, q_scale=q_scale),
        grid=(d // tr,),
        in_specs=[pl.BlockSpec((n_in, tr), lambda i: (0, i))],
        out_specs=[pl.BlockSpec((tr, n_out), lambda i: (i, 0)), pl.BlockSpec((tr, 128), lambda i: (i, 0))],
        out_shape=[jax.ShapeDtypeStruct((d, n_out), BF16), jax.ShapeDtypeStruct((d, 128), BF16)],
        compiler_params=_params(("parallel",)),
        name="win_pack",
    )(w_in.T)


def _seq_block(t, nt):
    return (t // nt) * (nt - 1) + jnp.maximum(t % nt - 1, 0)


def _padded_rows(t, nt, head_ref, x_ref):
    return jnp.where(t % nt == 0, head_ref[...], x_ref[...])


def _split3(x):
    hi = x.astype(BF16)
    r1 = x - hi.astype(F32)
    mid = r1.astype(BF16)
    lo = (r1 - mid.astype(F32)).astype(BF16)
    return hi, mid, lo


def _forget_bias(f, bf_ref, t, nt, carry, qb_ref, kb_ref, row0):
    tm = f.shape[0]
    first = t % nt == 0
    x = f + bf_ref[...]
    logf = jnp.minimum(x, 0.0) - jnp.log1p(jnp.exp(-jnp.abs(x)))
    row = lax.broadcasted_iota(jnp.int32, (tm, 1), 0)
    valid = jnp.logical_or(jnp.logical_not(first), row >= PAD)
    logf = jnp.where(valid, logf, 0.0)
    r = lax.broadcasted_iota(jnp.int32, (tm, tm), 0)
    c = lax.broadcasted_iota(jnp.int32, (tm, tm), 1)
    tri = (r >= c).astype(BF16)
    cs3 = jnp.dot(tri, jnp.concatenate(_split3(logf), axis=1), preferred_element_type=F32)
    w = f.shape[1]
    fcum = cs3[:, 0:w] + cs3[:, w:2 * w] + cs3[:, 2 * w:3 * w] + jnp.where(first, 0.0, carry[0:1, :])
    carry[0:1, :] = fcum[tm - 1:tm, :]

    lane = lax.broadcasted_iota(jnp.int32, (tm, HEAD_DIM), 1)
    one = jnp.ones((tm, HEAD_DIM), F32)
    zero = jnp.zeros((tm, HEAD_DIM), F32)
    for h in range(HEADS):
        hi, mid, lo = (p.astype(F32) for p in _split3(fcum[:, h:h + 1] * LOG2E))
        qb = jnp.where(lane == 0, hi, jnp.where(lane == 1, mid, jnp.where(lane == 2, lo,
             jnp.where(lane < 6, one, zero))))
        khi = jnp.where(valid, hi, MASK_BIG)
        kmid = jnp.where(valid, mid, 0.0)
        klo = jnp.where(valid, lo, 0.0)
        kb = jnp.where(lane < 3, one, jnp.where(lane == 3, -khi, jnp.where(lane == 4, -kmid,
             jnp.where(lane == 5, -klo, zero))))
        qb_ref[h, row0:row0 + tm, :] = qb.astype(qb_ref.dtype)
        kb_ref[h, row0:row0 + tm, :] = kb.astype(kb_ref.dtype)


def _inproj_kernel(head_ref, xa_ref, xb_ref, g_ref, w_ref, wf_ref, bf_ref, z_ref, u_ref, qb_ref, kb_ref,
                   n_scr, carry, *, nt, u_blk, u_lo):
    i = pl.program_id(0)
    j = pl.program_id(1)

    @pl.when(j == 0)
    def _():
        x = jnp.concatenate([_padded_rows(2 * i, nt, head_ref, xa_ref),
                             _padded_rows(2 * i + 1, nt, head_ref, xb_ref)], axis=0)
        ms = jnp.mean(x * x, axis=-1, keepdims=True)
        n = (x * lax.rsqrt(ms + EPS) * g_ref[...]).astype(BF16)
        n_scr[...] = n
        f = jnp.dot(n, wf_ref[...], preferred_element_type=F32)

        @pl.when(i == 0)
        def _():
            carry[...] = jnp.zeros_like(carry)

        for blk in range(2):
            _forget_bias(f[blk * ROW_ALIGN:(blk + 1) * ROW_ALIGN], bf_ref, 2 * i + blk, nt, carry,
                         qb_ref, kb_ref, blk * ROW_ALIGN)

    acc = jnp.dot(n_scr[...], w_ref[...], preferred_element_type=F32)
    z_ref[...] = acc.astype(z_ref.dtype)

    @pl.when(j == u_blk)
    def _():
        u_ref[...] = acc[:, u_lo:u_lo + u_ref.shape[1]]


def _inproj(head, x2, g, w, wf, bf, rows, nt, u_off, u_width, tn=2048):
    d = x2.shape[1]
    tm = 2 * ROW_ALIGN
    n_out = w.shape[1]
    u_blk, u_lo = divmod(u_off, tn)
    assert u_lo + u_width <= tn and rows % tm == 0
    bias_spec = pl.BlockSpec((HEADS, tm, HEAD_DIM), lambda i, j: (0, i, 0))
    bias_shape = jax.ShapeDtypeStruct((HEADS, rows, HEAD_DIM), BF16)
    return pl.pallas_call(
        functools.partial(_inproj_kernel, nt=nt, u_blk=u_blk, u_lo=u_lo),
        grid=(rows // tm, n_out // tn),
        in_specs=[
            _resident((ROW_ALIGN, d), lambda i, j: (0, 0)),
            pl.BlockSpec((ROW_ALIGN, d), lambda i, j: (_seq_block(2 * i, nt), 0)),
            pl.BlockSpec((ROW_ALIGN, d), lambda i, j: (_seq_block(2 * i + 1, nt), 0)),
            pl.BlockSpec((1, d), lambda i, j: (0, 0)),
            pl.BlockSpec((d, tn), lambda i, j: (0, j)),
            pl.BlockSpec((d, 128), lambda i, j: (0, 0)),
            pl.BlockSpec((1, 128), lambda i, j: (0, 0)),
        ],
        out_specs=[
            pl.BlockSpec((tm, tn), lambda i, j: (i, j)),
            pl.BlockSpec((tm, u_width), lambda i, j: (i, 0)),
            bias_spec, bias_spec,
        ],
        out_shape=[
            jax.ShapeDtypeStruct((rows, n_out), BF16),
            jax.ShapeDtypeStruct((rows, u_width), F32),
            bias_shape, bias_shape,
        ],
        scratch_shapes=[pltpu.VMEM((tm, d), BF16), pltpu.VMEM((8, 128), F32)],
        compiler_params=_params(("arbitrary", "arbitrary")),
        name="inproj",
    )(head, x2, x2, g, w, wf, bf)


def _flash_kernel(q_ref, k_ref, v_ref, qb_ref, kb_ref, o_ref, qa_ref, ka_ref, vt_ref, sa_ref, sb_ref, m_ref, acc_ref):
    lp = v_ref.shape[0]
    tq, tk = ATT_TQ, ATT_TK
    n_main = (lp - ROW_ALIGN) // tq
    n_chunks = (lp - ROW_ALIGN) // tk
    va = vt_ref.shape[1]
    nt_dims = (((1,), (1,)), ((), ()))

    qa_ref[:, 0:HEAD_DIM] = q_ref[...]
    qa_ref[:, HEAD_DIM:QK_AUG] = qb_ref[0]
    ka_ref[:, 0:HEAD_DIM] = k_ref[...]
    ka_ref[:, HEAD_DIM:QK_AUG] = kb_ref[0]

    ones = jnp.ones((va - HEAD_DIM, tk), vt_ref.dtype)
    vt_ref[0, 0:HEAD_DIM, 0:ROW_ALIGN] = v_ref[0:ROW_ALIGN, :].T
    vt_ref[0, HEAD_DIM:va, :] = ones

    def fill(c, _):
        vt_ref[1 + c, 0:HEAD_DIM, :] = v_ref[pl.ds(pl.multiple_of(ROW_ALIGN + c * tk, ROW_ALIGN), tk), :].T
        vt_ref[1 + c, HEAD_DIM:va, :] = ones
        return 0

    lax.fori_loop(0, n_chunks, fill, 0)

    def scores(s_ref, q_off, nq, k_off, nk, col0):
        kc = ka_ref[pl.ds(k_off, nk), :]
        qc = qa_ref[pl.ds(q_off, nq), :]
        s_ref[0:nk, col0:col0 + nq] = lax.dot_general(kc, qc, nt_dims, preferred_element_type=F32)

    def absorb(s_ref, slot, nk, lo, hi, mask_shift=None):
        s = s_ref[0:nk, lo:hi]
        if mask_shift is not None:
            kr = lax.broadcasted_iota(jnp.int32, s.shape, 0) + mask_shift
            qc = lax.broadcasted_iota(jnp.int32, s.shape, 1) + lo
            s = jnp.where(kr > qc, NEG, s)
        m = m_ref[0:1, lo:hi]
        m_new = jnp.maximum(m, jnp.max(s, axis=0, keepdims=True))
        alpha = jnp.exp2(m - m_new)
        p = jnp.exp2(s - m_new).astype(BF16)
        m_ref[0:1, lo:hi] = m_new
        pv = jnp.dot(vt_ref[slot, :, 0:nk], p, preferred_element_type=F32)
        acc_ref[:, lo:hi] = alpha * acc_ref[:, lo:hi] + pv

    def reset(n):
        m_ref[0:1, 0:n] = jnp.full((1, n), NEG, F32)
        acc_ref[:, 0:n] = jnp.zeros((va, n), F32)

    def finish(q_off, n):
        out = acc_ref[0:HEAD_DIM, 0:n] / acc_ref[HEAD_DIM:HEAD_DIM + 1, 0:n]
        o_ref[pl.ds(q_off, n), :] = out.T.astype(o_ref.dtype)

    reset(ROW_ALIGN)
    scores(sa_ref, 0, ROW_ALIGN, 0, ROW_ALIGN, 0)
    absorb(sa_ref, 0, ROW_ALIGN, 0, ROW_ALIGN, mask_shift=0)
    finish(0, ROW_ALIGN)

    per_tile = tq // tk

    def tile(j, _):
        q_off = pl.multiple_of(ROW_ALIGN + j * tq, ROW_ALIGN)
        reset(tq)
        scores(sa_ref, q_off, tq, 0, ROW_ALIGN, 0)
        scores(sb_ref, q_off, tq, ROW_ALIGN, tk, 0)
        absorb(sa_ref, 0, ROW_ALIGN, 0, tq)

        def pair(p, _):
            kb = pl.multiple_of(ROW_ALIGN + p * (2 * tk), ROW_ALIGN)
            scores(sa_ref, q_off, tq, kb + tk, tk, 0)
            absorb(sb_ref, 1 + 2 * p, tk, 0, tq)
            scores(sb_ref, q_off, tq, kb + 2 * tk, tk, 0)
            absorb(sa_ref, 2 + 2 * p, tk, 0, tq)
            return 0

        lax.fori_loop(0, j * (per_tile // 2), pair, 0)
        scores(sa_ref, q_off + tk, tq - tk, q_off + tk, tk, tk)
        absorb(sb_ref, 1 + j * per_tile, tk, 0, tq, mask_shift=0)
        absorb(sa_ref, 2 + j * per_tile, tk, tk, tq, mask_shift=tk)
        finish(q_off, tq)
        return 0

    lax.fori_loop(0, n_main, tile, 0)


def _flash(z, qb, kb, bsz, lp):
    assert (lp - ROW_ALIGN) % ATT_TQ == 0 and ATT_TQ == 2 * ATT_TK
    bias_spec = pl.BlockSpec((1, lp, HEAD_DIM), lambda b, h: (h, b, 0))
    va = HEAD_DIM + 16
    return pl.pallas_call(
        _flash_kernel,
        grid=(bsz, HEADS),
        in_specs=[
            pl.BlockSpec((lp, HEAD_DIM), lambda b, h: (b, h)),
            pl.BlockSpec((lp, HEAD_DIM), lambda b, h: (b, HEADS + h)),
            pl.BlockSpec((lp, HEAD_DIM), lambda b, h: (b, 2 * HEADS + h)),
            bias_spec, bias_spec,
        ],
        out_specs=pl.BlockSpec((lp, HEAD_DIM), lambda b, h: (b, h)),
        out_shape=jax.ShapeDtypeStruct((bsz * lp, D_ATTN), BF16),
        scratch_shapes=[
            pltpu.VMEM((lp, QK_AUG), BF16),
            pltpu.VMEM((lp, QK_AUG), BF16),
            pltpu.VMEM((1 + (lp - ROW_ALIGN) // ATT_TK, va, ATT_TK), BF16),
            pltpu.VMEM((ATT_TK, ATT_TQ), F32),
            pltpu.VMEM((ATT_TK, ATT_TQ), F32),
            pltpu.VMEM((8, ATT_TQ), F32),
            pltpu.VMEM((va, ATT_TQ), F32),
        ],
        compiler_params=_params(("parallel", "parallel")),
        name="flash",
    )(z, z, z, qb, kb)


def _tile_lanes(x, reps):
    return jnp.concatenate([x] * reps, axis=1)


def _ssm_kernel(u_ref, lr_ref, li_ref, dt_ref, lrl_ref, lil_ref, dtl_ref, btr_ref, bti_ref, cr_ref, ci_ref,
                y_ref, lhs_scr, toep_scr, inj_scr, out_scr, s_scr, hp_scr):
    t = SSM_CHUNK
    lanes = SSM_SLAB * SSM_GROUP
    ns = SSM_SLAB * SSM_STATE
    nc = u_ref.shape[0] // t

    @pl.when(pl.program_id(1) == 0)
    def _build_weights():
        lr = lr_ref[0]
        li = li_ref[0]
        dt = jnp.exp(dt_ref[0])
        btr, bti = btr_ref[0], bti_ref[0]
        cr, ci = cr_ref[0], ci_ref[0]

        mag = jnp.exp(lr * dt)
        a_re = mag * jnp.cos(li * dt)
        a_im = mag * jnp.sin(li * dt)
        powers = [(jnp.ones_like(a_re), jnp.zeros_like(a_re))]
        for _ in range(t):
            pr, pi = powers[-1]
            powers.append((pr * a_re - pi * a_im, pr * a_im + pi * a_re))
        den = lr * lr + li * li
        nr = a_re - 1.0
        z_re = (nr * lr + a_im * li) / den
        z_im = (a_im * lr - nr * li) / den
        row_g = lax.broadcasted_iota(jnp.int32, (lanes, ns), 0) // SSM_GROUP
        col_g = lax.broadcasted_iota(jnp.int32, (lanes, ns), 1) // SSM_STATE
        same = row_g == col_g

        def spread(x):
            return jnp.where(same, _tile_lanes(x, SSM_SLAB), 0.0)

        caz_re, caz_im = [], []
        for d in range(t):
            pr, pi = powers[d]
            azr = pr * z_re - pi * z_im
            azi = pr * z_im + pi * z_re
            caz_re.append(cr * azr - ci * azi)
            caz_im.append(cr * azi + ci * azr)
            i = t - 1 - d
            sr = btr * azr - bti * azi
            si = btr * azi + bti * azr
            inj_scr[i * lanes:(i + 1) * lanes, 0:ns] = spread(sr).astype(BF16)
            inj_scr[i * lanes:(i + 1) * lanes, ns:2 * ns] = spread(si).astype(BF16)
            qr, qi = powers[d + 1]
            er = cr * qr - ci * qi
            ei = cr * qi + ci * qr
            out_scr[d * lanes:(d + 1) * lanes, 0:ns] = spread(er).astype(BF16)
            out_scr[d * lanes:(d + 1) * lanes, ns:2 * ns] = spread(-ei).astype(BF16)
        nt_dims = (((1,), (1,)), ((), ()))
        hp = lax.Precision.HIGHEST
        r0 = (lax.dot_general(btr, jnp.concatenate(caz_re, axis=0), nt_dims, precision=hp, preferred_element_type=F32)
              - lax.dot_general(bti, jnp.concatenate(caz_im, axis=0), nt_dims, precision=hp, preferred_element_type=F32))
        rg = lax.broadcasted_iota(jnp.int32, r0.shape, 0) // SSM_GROUP
        cg = (lax.broadcasted_iota(jnp.int32, r0.shape, 1) % lanes) // SSM_GROUP
        r0 = jnp.where(rg == cg, r0, 0.0).astype(BF16)
        toep_scr[...] = jnp.zeros_like(toep_scr)
        for i in range(t):
            toep_scr[i * lanes:(i + 1) * lanes, i * lanes:t * lanes] = r0[:, 0:(t - i) * lanes]

    for i in range(t):
        lhs_scr[:, i * lanes:(i + 1) * lanes] = u_ref[pl.ds(i, nc, stride=t), :].astype(BF16)

    s_scr[...] = jnp.dot(lhs_scr[...], inj_scr[...], preferred_element_type=F32)

    dtl = jnp.exp(dtl_ref[0])
    mag = jnp.exp(float(t) * (lrl_ref[0] * dtl))
    ang = float(t) * (lil_ref[0] * dtl)
    ar = mag * jnp.cos(ang)
    ai = mag * jnp.sin(ang)

    def step(n, carry):
        hr, hi = carry
        hp_scr[pl.ds(n, 1), 0:ns] = hr
        hp_scr[pl.ds(n, 1), ns:2 * ns] = hi
        sr = s_scr[pl.ds(n, 1), 0:ns]
        si = s_scr[pl.ds(n, 1), ns:2 * ns]
        return ar * hr - ai * hi + sr, ar * hi + ai * hr + si

    zero = jnp.zeros((1, ns), F32)
    lax.fori_loop(0, nc, step, (zero, zero), unroll=8)

    hprev = hp_scr[...].astype(BF16)
    nt_dims = (((1,), (1,)), ((), ()))
    pair = 2 * lanes
    for jj in range(t // 2):
        kdim = (jj + 1) * pair
        yj = (jnp.dot(lhs_scr[:, 0:kdim], toep_scr[0:kdim, jj * pair:(jj + 1) * pair], preferred_element_type=F32)
              + lax.dot_general(hprev, out_scr[jj * pair:(jj + 1) * pair, :], nt_dims, preferred_element_type=F32))
        y_ref[pl.ds(2 * jj, nc, stride=t), :] = yj[:, 0:lanes]
        y_ref[pl.ds(2 * jj + 1, nc, stride=t), :] = yj[:, lanes:pair]


def _ssm(u, lam_re, lam_im, log_dt, b_re, b_im, c_re, c_im, bsz, lp):
    rows, ds = u.shape
    g, p = lam_re.shape
    t, c = SSM_CHUNK, SSM_GROUP
    lanes = SSM_SLAB * c
    ns = SSM_SLAB * p
    nslab = g // SSM_SLAB
    nc = lp // t
    assert g % SSM_SLAB == 0 and lp % t == 0 and lanes == 128
    rep = lambda x: jnp.repeat(x, c, axis=0).reshape(nslab, lanes, -1)
    lane = lambda x: x.reshape(nslab, 1, ns)
    args = (
        u,
        rep(lam_re), rep(lam_im), rep(log_dt.reshape(g, 1)),
        lane(lam_re), lane(lam_im), lane(jnp.repeat(log_dt, p)),
        jnp.swapaxes(b_re, 1, 2).reshape(nslab, lanes, p), jnp.swapaxes(b_im, 1, 2).reshape(nslab, lanes, p),
        c_re.reshape(nslab, lanes, p), c_im.reshape(nslab, lanes, p),
    )
    rows_spec = pl.BlockSpec((1, lanes, p), lambda s, b: (s, 0, 0))
    lane_spec = pl.BlockSpec((1, 1, ns), lambda s, b: (s, 0, 0))
    y = pl.pallas_call(
        _ssm_kernel,
        grid=(nslab, bsz),
        in_specs=[
            pl.BlockSpec((lp, lanes), lambda s, b: (b, s)),
            rows_spec, rows_spec, pl.BlockSpec((1, lanes, 1), lambda s, b: (s, 0, 0)),
            lane_spec, lane_spec, lane_spec,
            rows_spec, rows_spec, rows_spec, rows_spec,
        ],
        out_specs=pl.BlockSpec((lp, lanes), lambda s, b: (b, s)),
        out_shape=jax.ShapeDtypeStruct((rows, ds), F32),
        scratch_shapes=[
            pltpu.VMEM((nc, t * lanes), BF16),
            pltpu.VMEM((t * lanes, t * lanes), BF16),
            pltpu.VMEM((t * lanes, 2 * ns), BF16),
            pltpu.VMEM((t * lanes, 2 * ns), BF16),
            pltpu.VMEM((nc, 2 * ns), F32),
            pltpu.VMEM((nc, 2 * ns), F32),
        ],
        compiler_params=_params(("parallel", "arbitrary")),
        name="ssm",
    )(*args)
    return y


def _mix_kernel(y_ref, u_ref, a_ref, ga_ref, gb_ref, head_ref, x_ref, dsk_ref, gffn_ref,
                wglu_ref, wao_ref, wout_ref, h_ref, n_ref, *, nt):
    d = x_ref.shape[1]
    y = y_ref[...].astype(F32) + dsk_ref[...] * u_ref[...].astype(F32)
    gy = _gelu_tanh(y).astype(BF16)
    yab = jnp.dot(gy, wglu_ref[...], preferred_element_type=F32)
    ssm_out = yab[:, 0:d] * _sigmoid(yab[:, d:2 * d])
    ao = jnp.dot(a_ref[...], wao_ref[...], preferred_element_type=F32)
    merged = (_sigmoid(ga_ref[...].astype(F32)) * ssm_out
              + _sigmoid(gb_ref[...].astype(F32)) * ao).astype(BF16)
    x = _padded_rows(pl.program_id(0), nt, head_ref, x_ref)
    h = x + jnp.dot(merged, wout_ref[...], preferred_element_type=F32)
    h_ref[...] = h
    ms = jnp.mean(h * h, axis=-1, keepdims=True)
    n_ref[...] = (h * lax.rsqrt(ms + EPS) * gffn_ref[...]).astype(n_ref.dtype)


def _mix(y, z, attn, head, x2, nt, d_skip, g_ffn, w_glu, w_ao, w_out):
    rows, ds = y.shape
    d = x2.shape[1]
    tm = ROW_ALIGN
    u_col = 3 * D_ATTN // ds
    ga_col = (3 * D_ATTN + ds) // d
    assert (3 * D_ATTN) % ds == 0 and (3 * D_ATTN + ds) % d == 0
    return pl.pallas_call(
        functools.partial(_mix_kernel, nt=nt),
        grid=(rows // tm,),
        in_specs=[
            pl.BlockSpec((tm, ds), lambda i: (i, 0)),
            pl.BlockSpec((tm, ds), lambda i: (i, u_col)),
            pl.BlockSpec((tm, D_ATTN), lambda i: (i, 0)),
            pl.BlockSpec((tm, d), lambda i: (i, ga_col)),
            pl.BlockSpec((tm, d), lambda i: (i, ga_col + 1)),
            _resident((tm, d), lambda i: (0, 0)),
            pl.BlockSpec((tm, d), lambda i: (_seq_block(i, nt), 0)),
            _resident((1, ds), lambda i: (0, 0)),
            _resident((1, d), lambda i: (0, 0)),
            _resident(w_glu.shape, lambda i: (0, 0)),
            _resident(w_ao.shape, lambda i: (0, 0)),
            _resident(w_out.shape, lambda i: (0, 0)),
        ],
        out_specs=[
            pl.BlockSpec((tm, d), lambda i: (i, 0)),
            pl.BlockSpec((tm, d), lambda i: (i, 0)),
        ],
        out_shape=[
            jax.ShapeDtypeStruct((rows, d), F32),
            jax.ShapeDtypeStruct((rows, d), BF16),
        ],
        compiler_params=_params(("parallel",)),
        name="mix_out",
    )(y, z, attn, z, z, head, x2, d_skip, g_ffn, w_glu, w_ao, w_out)


def _ffn_up_kernel(n_ref, wg_ref, wu_ref, cw_ref, cb_ref, a_ref, tail, wg_scr, wu_scr):
    tm = n_ref.shape[0]

    @pl.when(pl.program_id(1) == 0)
    def _():
        tail[...] = jnp.zeros_like(tail)
        wg_scr[...] = wg_ref[...].astype(wg_scr.dtype)
        wu_scr[...] = wu_ref[...].astype(wu_scr.dtype)

    n = n_ref[...]
    g = jnp.dot(n, wg_scr[...], preferred_element_type=F32)
    u = jnp.dot(n, wu_scr[...], preferred_element_type=F32)
    row = lax.broadcasted_iota(jnp.int32, g.shape, 0)
    prev1 = tail[CONV_WIDTH - 2:CONV_WIDTH - 1, :]
    prev2 = tail[CONV_WIDTH - 3:CONV_WIDTH - 2, :]
    g1 = jnp.where(row == 0, prev1, pltpu.roll(g, 1, 0))
    g2 = jnp.where(row == 0, prev2, jnp.where(row == 1, prev1, pltpu.roll(g, 2, 0)))
    gc = cb_ref[...] + cw_ref[0:1, :] * g2 + cw_ref[1:2, :] * g1 + cw_ref[2:3, :] * g
    tail[0:CONV_WIDTH - 1, :] = g[tm - (CONV_WIDTH - 1):tm, :]
    a_ref[...] = (gc * _sigmoid(gc) * u).astype(a_ref.dtype)


def _largest_tile(n, candidates):
    return next(c for c in candidates if n % c == 0)


def _ffn_up(n2, w_up, conv_w, conv_b, tn=512):
    rows, d = n2.shape
    dff = conv_w.shape[1]
    nj = dff // tn
    tm = _largest_tile(rows, (1536, 1024, 768, 512, 256))
    assert dff % tn == 0
    return pl.pallas_call(
        _ffn_up_kernel,
        grid=(nj, rows // tm),
        in_specs=[
            pl.BlockSpec((tm, d), lambda j, i: (i, 0)),
            pl.BlockSpec((d, tn), lambda j, i: (0, j)),
            pl.BlockSpec((d, tn), lambda j, i: (0, nj + j)),
            pl.BlockSpec((CONV_WIDTH, tn), lambda j, i: (0, j)),
            pl.BlockSpec((1, tn), lambda j, i: (0, j)),
        ],
        out_specs=pl.BlockSpec((tm, tn), lambda j, i: (i, j)),
        out_shape=jax.ShapeDtypeStruct((rows, dff), BF16),
        scratch_shapes=[pltpu.VMEM((8, tn), F32), pltpu.VMEM((d, tn), BF16), pltpu.VMEM((d, tn), BF16)],
        compiler_params=_params(("parallel", "arbitrary")),
        name="ffn_up",
    )(n2, w_up, w_up, conv_w, conv_b)


def _ffn_down_kernel(a_ref, w_ref, h_ref, g_ref, o_ref):
    h = h_ref[...] + jnp.dot(a_ref[...], w_ref[...], preferred_element_type=F32)
    ms = jnp.mean(h * h, axis=-1, keepdims=True)
    o_ref[...] = (h * lax.rsqrt(ms + EPS) * g_ref[...]).astype(o_ref.dtype)


def _ffn_down(act, w_down, h1, g_final, bsz, lp, tm=ROW_ALIGN):
    dff, d = w_down.shape
    nt_in = lp // tm
    nt_out = nt_in - ROW_ALIGN // tm
    skip = ROW_ALIGN // tm
    return pl.pallas_call(
        _ffn_down_kernel,
        grid=(bsz, nt_out),
        in_specs=[
            pl.BlockSpec((tm, dff), lambda b, t: (b * nt_in + skip + t, 0)),
            _resident((dff, d), lambda b, t: (0, 0)),
            pl.BlockSpec((tm, d), lambda b, t: (b * nt_in + skip + t, 0)),
            _resident((1, d), lambda b, t: (0, 0)),
        ],
        out_specs=pl.BlockSpec((tm, d), lambda b, t: (b * nt_out + t, 0)),
        out_shape=jax.ShapeDtypeStruct((bsz * nt_out * tm, d), F32),
        compiler_params=_params(("parallel", "parallel")),
        name="ffn_down",
    )(act, w_down, h1, g_final)


def _layer(head, x2, bsz, lp, g_mix, w_in, b_f, lam_re, lam_im, log_dt, b_re, b_im, c_re, c_im, d_skip,
           w_glu, w_attn_o, w_out, g_ffn, w_up, conv_w, conv_b, w_down):
    d = x2.shape[1]
    ds = d_skip.shape[0]
    nt = lp // ROW_ALIGN
    o_q, o_k, o_v, o_f, o_u, o_ga, o_gb = (0, D_ATTN, 2 * D_ATTN, 3 * D_ATTN, 3 * D_ATTN + HEADS,
                                            3 * D_ATTN + HEADS + ds, 3 * D_ATTN + HEADS + ds + d)
    w_main, w_f = _win_pack(w_in, o_f, o_u, HEAD_DIM ** -0.5 * LOG2E)
    bf = jnp.pad(b_f.astype(F32), (0, 128 - HEADS)).reshape(1, 128)

    z, u, qb, kb = _inproj(head, x2, g_mix.reshape(1, d), w_main, w_f, bf, bsz * lp, nt, 3 * D_ATTN, ds)
    attn = _flash(z, qb, kb, bsz, lp)
    y = _ssm(u, lam_re, lam_im, log_dt, b_re, b_im, c_re, c_im, bsz, lp)

    h1, n2 = _mix(y, z, attn, head, x2, nt, d_skip.reshape(1, ds).astype(F32), g_ffn.reshape(1, d).astype(F32),
                  w_glu.astype(BF16), w_attn_o.astype(BF16), w_out.astype(BF16))
    act = _ffn_up(n2, w_up.astype(F32), conv_w.astype(F32), conv_b.reshape(1, -1).astype(F32))
    return act, h1


def kernel(x, meta, g_mix, w_in, b_f, lam_re, lam_im, log_dt, b_re, b_im, c_re, c_im, d_skip,
           w_glu, w_attn_o, w_out, g_ffn, w_up, conv_w, conv_b, w_down, g_final):
    bsz, seq, d = x.shape
    depth = g_mix.shape[0]
    assert depth == 1 and meta.shape[0] == N_META and seq % ROW_ALIGN == 0
    lp = seq + ROW_ALIGN
    assert (bsz * lp) % 512 == 0
    head = jnp.concatenate([jnp.zeros((PAD, d), x.dtype), meta.astype(x.dtype)], axis=0)
    act, h1 = _layer(head, x.reshape(bsz * seq, d), bsz, lp, g_mix[0], w_in[0], b_f[0], lam_re[0], lam_im[0], log_dt[0],
                     b_re[0], b_im[0], c_re[0], c_im[0], d_skip[0], w_glu[0], w_attn_o[0], w_out[0],
                     g_ffn[0], w_up[0], conv_w[0], conv_b[0], w_down[0])
    out = _ffn_down(act, w_down[0].astype(BF16), h1, g_final.reshape(1, d).astype(F32), bsz, lp)
    return out.reshape(bsz, seq, d)
```

```python
import functools
import math

import jax
import jax.numpy as jnp
from jax import lax
from jax.experimental import pallas as pl
from jax.experimental.pallas import tpu as pltpu

N_META = 16
HEADS = 8
HEAD_DIM = 128
D_ATTN = HEADS * HEAD_DIM
SSM_GROUP = 16
SSM_STATE = 64
SSM_CHUNK = 16
SSM_SLAB = 8
CONV_WIDTH = 3
EPS = 1e-6

ROW_ALIGN = 256
PAD = ROW_ALIGN - N_META
ATT_TQ = 1024
ATT_TK = 512
QK_AUG = 2 * HEAD_DIM
LOG2E = math.log2(math.e)
MASK_BIG = 1e30
NEG = -3e38
VMEM_LIMIT = 56 * 1024 * 1024

F32 = jnp.float32
BF16 = jnp.bfloat16


def _sigmoid(x):
    return 1.0 / (1.0 + jnp.exp(-x))


def _gelu_tanh(x):
    c = math.sqrt(2.0 / math.pi)
    return 0.5 * x * (1.0 + jnp.tanh(c * (x + 0.044715 * (x * x * x))))


def _params(sem, limit=VMEM_LIMIT):
    return pltpu.CompilerParams(dimension_semantics=sem, vmem_limit_bytes=limit)


def _resident(shape, index_map):
    return pl.BlockSpec(shape, index_map, pipeline_mode=pl.Buffered(1))


def _win_pack_kernel(wt_ref, o_ref, f_ref, *, o_f, o_u, q_scale):
    n_in, tr = wt_ref.shape
    o_ref[:, 0:D_ATTN] = (wt_ref[0:D_ATTN, :] * q_scale).T.astype(o_ref.dtype)
    o_ref[:, D_ATTN:o_f] = wt_ref[D_ATTN:o_f, :].T.astype(o_ref.dtype)
    o_ref[:, o_f:o_f + (n_in - o_u)] = wt_ref[o_u:n_in, :].T.astype(o_ref.dtype)
    wf = jnp.concatenate([wt_ref[o_f:o_u, :], jnp.zeros((f_ref.shape[1] - (o_u - o_f), tr), F32)], axis=0)
    f_ref[...] = wf.T.astype(f_ref.dtype)


def _win_pack(w_in, o_f, o_u, q_scale, tr=256):
    d, n_in = w_in.shape
    n_out = n_in - (o_u - o_f)
    return pl.pallas_call(
        functools.partial(_win_pack_kernel, o_f=o_f, o_u=o_u, q_scale=q_scale),
        grid=(d // tr,),
        in_specs=[pl.BlockSpec((n_in, tr), lambda i: (0, i))],
        out_specs=[pl.BlockSpec((tr, n_out), lambda i: (i, 0)), pl.BlockSpec((tr, 128), lambda i: (i, 0))],
        out_shape=[jax.ShapeDtypeStruct((d, n_out), BF16), jax.ShapeDtypeStruct((d, 128), BF16)],
        compiler_params=_params(("parallel",)),
        name="win_pack",
    )(w_in.T)


def _seq_block(t, nt):
    return (t // nt) * (nt - 1) + jnp.maximum(t % nt - 1, 0)


def _padded_rows(t, nt, head_ref, x_ref):
    return jnp.where(t % nt == 0, head_ref[...], x_ref[...])


def _split3(x):
    hi = x.astype(BF16)
    r1 = x - hi.astype(F32)
    mid = r1.astype(BF16)
    lo = (r1 - mid.astype(F32)).astype(BF16)
    return hi, mid, lo


def _forget_bias(f, bf_ref, t, nt, carry, qb_ref, kb_ref, row0):
    tm = f.shape[0]
    first = t % nt == 0
    x = f + bf_ref[...]
    logf = jnp.minimum(x, 0.0) - jnp.log1p(jnp.exp(-jnp.abs(x)))
    row = lax.broadcasted_iota(jnp.int32, (tm, 1), 0)
    valid = jnp.logical_or(jnp.logical_not(first), row >= PAD)
    logf = jnp.where(valid, logf, 0.0)
    r = lax.broadcasted_iota(jnp.int32, (tm, tm), 0)
    c = lax.broadcasted_iota(jnp.int32, (tm, tm), 1)
    tri = (r >= c).astype(BF16)
    cs3 = jnp.dot(tri, jnp.concatenate(_split3(logf), axis=1), preferred_element_type=F32)
    w = f.shape[1]
    fcum = cs3[:, 0:w] + cs3[:, w:2 * w] + cs3[:, 2 * w:3 * w] + jnp.where(first, 0.0, carry[0:1, :])
    carry[0:1, :] = fcum[tm - 1:tm, :]

    lane = lax.broadcasted_iota(jnp.int32, (tm, HEAD_DIM), 1)
    one = jnp.ones((tm, HEAD_DIM), F32)
    zero = jnp.zeros((tm, HEAD_DIM), F32)
    for h in range(HEADS):
        hi, mid, lo = (p.astype(F32) for p in _split3(fcum[:, h:h + 1] * LOG2E))
        qb = jnp.where(lane == 0, hi, jnp.where(lane == 1, mid, jnp.where(lane == 2, lo,
             jnp.where(lane < 6, one, zero))))
        khi = jnp.where(valid, hi, MASK_BIG)
        kmid = jnp.where(valid, mid, 0.0)
        klo = jnp.where(valid, lo, 0.0)
        kb = jnp.where(lane < 3, one, jnp.where(lane == 3, -khi, jnp.where(lane == 4, -kmid,
             jnp.where(lane == 5, -klo, zero))))
        qb_ref[h, row0:row0 + tm, :] = qb.astype(qb_ref.dtype)
        kb_ref[h, row0:row0 + tm, :] = kb.astype(kb_ref.dtype)


def _inproj_kernel(head_ref, xa_ref, xb_ref, g_ref, w_ref, wf_ref, bf_ref, z_ref, u_ref, qb_ref, kb_ref,
                   n_scr, carry, *, nt, u_blk, u_lo):
    i = pl.program_id(0)
    j = pl.program_id(1)
    n_i = pl.num_programs(0)
    n_j = pl.num_programs(1)

    def prepare(t):
        x = jnp.concatenate([_padded_rows(2 * t, nt, head_ref, xa_ref),
                             _padded_rows(2 * t + 1, nt, head_ref, xb_ref)], axis=0)
        ms = jnp.mean(x * x, axis=-1, keepdims=True)
        n = (x * lax.rsqrt(ms + EPS) * g_ref[...]).astype(BF16)
        n_scr[t % 2] = n
        f = jnp.dot(n, wf_ref[...], preferred_element_type=F32)
        for blk in range(2):
            _forget_bias(f[blk * ROW_ALIGN:(blk + 1) * ROW_ALIGN], bf_ref, 2 * t + blk, nt, carry,
                         qb_ref, kb_ref, blk * ROW_ALIGN)

    def project():
        acc = jnp.dot(n_scr[i % 2], w_ref[...], preferred_element_type=F32)
        z_ref[...] = acc.astype(z_ref.dtype)
        return acc

    @pl.when(jnp.logical_and(i == 0, j == 0))
    def _():
        carry[...] = jnp.zeros_like(carry)
        prepare(0)

    lookahead = jnp.logical_and(j == n_j - 1, i + 1 < n_i)

    @pl.when(lookahead)
    def _():
        prepare(i + 1)
        project()

    @pl.when(jnp.logical_not(lookahead))
    def _():
        acc = project()

        @pl.when(j == u_blk)
        def _():
            u_ref[...] = acc[:, u_lo:u_lo + u_ref.shape[1]]


def _inproj(head, x2, g, w, wf, bf, rows, nt, u_off, u_width, tn=2048):
    d = x2.shape[1]
    tm = 2 * ROW_ALIGN
    n_out = w.shape[1]
    n_i, n_j = rows // tm, n_out // tn
    u_blk, u_lo = divmod(u_off, tn)
    assert u_lo + u_width <= tn and rows % tm == 0 and u_blk < n_j - 1

    def tile(i, j):
        return jnp.minimum(i + (j == n_j - 1).astype(jnp.int32), n_i - 1)

    bias_spec = pl.BlockSpec((HEADS, tm, HEAD_DIM), lambda i, j: (0, tile(i, j), 0))
    bias_shape = jax.ShapeDtypeStruct((HEADS, rows, HEAD_DIM), BF16)
    return pl.pallas_call(
        functools.partial(_inproj_kernel, nt=nt, u_blk=u_blk, u_lo=u_lo),
        grid=(n_i, n_j),
        in_specs=[
            _resident((ROW_ALIGN, d), lambda i, j: (0, 0)),
            pl.BlockSpec((ROW_ALIGN, d), lambda i, j: (_seq_block(2 * tile(i, j), nt), 0)),
            pl.BlockSpec((ROW_ALIGN, d), lambda i, j: (_seq_block(2 * tile(i, j) + 1, nt), 0)),
            pl.BlockSpec((1, d), lambda i, j: (0, 0)),
            pl.BlockSpec((d, tn), lambda i, j: (0, j)),
            pl.BlockSpec((d, 128), lambda i, j: (0, 0)),
            pl.BlockSpec((1, 128), lambda i, j: (0, 0)),
        ],
        out_specs=[
            pl.BlockSpec((tm, tn), lambda i, j: (i, j)),
            pl.BlockSpec((tm, u_width), lambda i, j: (i, 0)),
            bias_spec, bias_spec,
        ],
        out_shape=[
            jax.ShapeDtypeStruct((rows, n_out), BF16),
            jax.ShapeDtypeStruct((rows, u_width), F32),
            bias_shape, bias_shape,
        ],
        scratch_shapes=[pltpu.VMEM((2, tm, d), BF16), pltpu.VMEM((8, 128), F32)],
        compiler_params=_params(("arbitrary", "arbitrary")),
        name="inproj",
    )(head, x2, x2, g, w, wf, bf)


def _flash_kernel(q_ref, k_ref, v_ref, qb_ref, kb_ref, o_ref,
                  qa_ref, ka_ref, vt_ref, sa_ref, sb_ref, mxa_ref, mxb_ref, m_ref, acc_ref):
    lp = v_ref.shape[0]
    tq, tk = ATT_TQ, ATT_TK
    n_main = (lp - ROW_ALIGN) // tq
    n_chunks = (lp - ROW_ALIGN) // tk
    va = vt_ref.shape[1]
    nt_dims = (((1,), (1,)), ((), ()))

    qa_ref[:, 0:HEAD_DIM] = q_ref[...]
    qa_ref[:, HEAD_DIM:QK_AUG] = qb_ref[0]
    ka_ref[:, 0:HEAD_DIM] = k_ref[...]
    ka_ref[:, HEAD_DIM:QK_AUG] = kb_ref[0]

    ones = jnp.ones((va - HEAD_DIM, tk), vt_ref.dtype)
    vt_ref[0, 0:HEAD_DIM, 0:ROW_ALIGN] = v_ref[0:ROW_ALIGN, :].T
    vt_ref[0, HEAD_DIM:va, :] = ones

    def fill(c, _):
        vt_ref[1 + c, 0:HEAD_DIM, :] = v_ref[pl.ds(pl.multiple_of(ROW_ALIGN + c * tk, ROW_ALIGN), tk), :].T
        vt_ref[1 + c, HEAD_DIM:va, :] = ones
        return 0

    lax.fori_loop(0, n_chunks, fill, 0)

    buf_a = (sa_ref, mxa_ref)
    buf_b = (sb_ref, mxb_ref)

    def scores(buf, q_off, nq, k_off, nk, col0):
        s_ref, mx_ref = buf
        kc = ka_ref[pl.ds(k_off, nk), :]
        qc = qa_ref[pl.ds(q_off, nq), :]
        s = lax.dot_general(kc, qc, nt_dims, preferred_element_type=F32)
        s_ref[0:nk, col0:col0 + nq] = s
        mx_ref[0:1, col0:col0 + nq] = jnp.max(s, axis=0, keepdims=True)

    def absorb(buf, slot, nk, lo, hi, mask_shift=None, lane0=0):
        s_ref, mx_ref = buf
        s = s_ref[0:nk, lo:hi]
        if mask_shift is None:
            smax = mx_ref[0:1, lo:hi]
        else:
            kr = lax.broadcasted_iota(jnp.int32, s.shape, 0) + mask_shift
            qc = lax.broadcasted_iota(jnp.int32, s.shape, 1) + lo
            s = jnp.where(kr > qc, NEG, s)
            smax = jnp.max(s, axis=0, keepdims=True)
        m = m_ref[0:1, lo:hi]
        m_new = jnp.maximum(m, smax)
        alpha = jnp.exp2(m - m_new)
        p = jnp.exp2(s - m_new).astype(BF16)
        m_ref[0:1, lo:hi] = m_new
        pv = jnp.dot(vt_ref[slot, :, lane0:lane0 + nk], p, preferred_element_type=F32)
        acc_ref[:, lo:hi] = alpha * acc_ref[:, lo:hi] + pv

    def reset(n):
        m_ref[0:1, 0:n] = jnp.full((1, n), NEG, F32)
        acc_ref[:, 0:n] = jnp.zeros((va, n), F32)

    def finish(q_off, n):
        out = acc_ref[0:HEAD_DIM, 0:n] / acc_ref[HEAD_DIM:HEAD_DIM + 1, 0:n]
        o_ref[pl.ds(q_off, n), :] = out.T.astype(o_ref.dtype)

    reset(ROW_ALIGN)
    scores(buf_a, 0, ROW_ALIGN, 0, ROW_ALIGN, 0)
    absorb(buf_a, 0, ROW_ALIGN, 0, ROW_ALIGN, mask_shift=0)
    finish(0, ROW_ALIGN)

    per_tile = tq // tk

    def tile(j, _):
        q_off = pl.multiple_of(ROW_ALIGN + j * tq, ROW_ALIGN)
        reset(tq)
        scores(buf_a, q_off, tq, PAD, N_META, 0)
        scores(buf_b, q_off, tq, ROW_ALIGN, tk, 0)
        absorb(buf_a, 0, N_META, 0, tq, lane0=PAD)

        def pair(p, _):
            kb = pl.multiple_of(ROW_ALIGN + p * (2 * tk), ROW_ALIGN)
            scores(buf_a, q_off, tq, kb + tk, tk, 0)
            absorb(buf_b, 1 + 2 * p, tk, 0, tq)
            scores(buf_b, q_off, tq, kb + 2 * tk, tk, 0)
            absorb(buf_a, 2 + 2 * p, tk, 0, tq)
            return 0

        lax.fori_loop(0, j * (per_tile // 2), pair, 0)
        scores(buf_a, q_off + tk, tq - tk, q_off + tk, tk, tk)
        absorb(buf_b, 1 + j * per_tile, tk, 0, tq, mask_shift=0)
        absorb(buf_a, 2 + j * per_tile, tk, tk, tq, mask_shift=tk)
        finish(q_off, tq)
        return 0

    lax.fori_loop(0, n_main, tile, 0)


def _flash(z, qb, kb, bsz, lp):
    assert (lp - ROW_ALIGN) % ATT_TQ == 0 and ATT_TQ == 2 * ATT_TK
    bias_spec = pl.BlockSpec((1, lp, HEAD_DIM), lambda b, h: (h, b, 0))
    va = HEAD_DIM + 16
    return pl.pallas_call(
        _flash_kernel,
        grid=(bsz, HEADS),
        in_specs=[
            pl.BlockSpec((lp, HEAD_DIM), lambda b, h: (b, h)),
            pl.BlockSpec((lp, HEAD_DIM), lambda b, h: (b, HEADS + h)),
            pl.BlockSpec((lp, HEAD_DIM), lambda b, h: (b, 2 * HEADS + h)),
            bias_spec, bias_spec,
        ],
        out_specs=pl.BlockSpec((lp, HEAD_DIM), lambda b, h: (b, h)),
        out_shape=jax.ShapeDtypeStruct((bsz * lp, D_ATTN), BF16),
        scratch_shapes=[
            pltpu.VMEM((lp, QK_AUG), BF16),
            pltpu.VMEM((lp, QK_AUG), BF16),
            pltpu.VMEM((1 + (lp - ROW_ALIGN) // ATT_TK, va, ATT_TK), BF16),
            pltpu.VMEM((ATT_TK, ATT_TQ), F32),
            pltpu.VMEM((ATT_TK, ATT_TQ), F32),
            pltpu.VMEM((8, ATT_TQ), F32),
            pltpu.VMEM((8, ATT_TQ), F32),
            pltpu.VMEM((8, ATT_TQ), F32),
            pltpu.VMEM((va, ATT_TQ), F32),
        ],
        compiler_params=_params(("parallel", "parallel")),
        name="flash",
    )(z, z, z, qb, kb)


def _tile_lanes(x, reps):
    return jnp.concatenate([x] * reps, axis=1)


def _ssm_kernel(u_ref, lr_ref, li_ref, dt_ref, lrl_ref, lil_ref, dtl_ref, btr_ref, bti_ref, cr_ref, ci_ref,
                y_ref, lhs_scr, toep_scr, inj_scr, out_scr, s_scr, hp_scr):
    t = SSM_CHUNK
    lanes = SSM_SLAB * SSM_GROUP
    ns = SSM_SLAB * SSM_STATE
    nc = u_ref.shape[0] // t

    @pl.when(pl.program_id(1) == 0)
    def _build_weights():
        lr = lr_ref[0]
        li = li_ref[0]
        dt = jnp.exp(dt_ref[0])
        btr, bti = btr_ref[0], bti_ref[0]
        cr, ci = cr_ref[0], ci_ref[0]

        mag = jnp.exp(lr * dt)
        a_re = mag * jnp.cos(li * dt)
        a_im = mag * jnp.sin(li * dt)
        powers = [(jnp.ones_like(a_re), jnp.zeros_like(a_re))]
        for _ in range(t):
            pr, pi = powers[-1]
            powers.append((pr * a_re - pi * a_im, pr * a_im + pi * a_re))
        den = lr * lr + li * li
        nr = a_re - 1.0
        z_re = (nr * lr + a_im * li) / den
        z_im = (a_im * lr - nr * li) / den
        row_g = lax.broadcasted_iota(jnp.int32, (lanes, ns), 0) // SSM_GROUP
        col_g = lax.broadcasted_iota(jnp.int32, (lanes, ns), 1) // SSM_STATE
        same = row_g == col_g

        def spread(x):
            return jnp.where(same, _tile_lanes(x, SSM_SLAB), 0.0)

        caz_re, caz_im = [], []
        for d in range(t):
            pr, pi = powers[d]
            azr = pr * z_re - pi * z_im
            azi = pr * z_im + pi * z_re
            caz_re.append(cr * azr - ci * azi)
            caz_im.append(cr * azi + ci * azr)
            i = t - 1 - d
            sr = btr * azr - bti * azi
            si = btr * azi + bti * azr
            inj_scr[i * lanes:(i + 1) * lanes, 0:ns] = spread(sr).astype(BF16)
            inj_scr[i * lanes:(i + 1) * lanes, ns:2 * ns] = spread(si).astype(BF16)
            qr, qi = powers[d + 1]
            er = cr * qr - ci * qi
            ei = cr * qi + ci * qr
            out_scr[d * lanes:(d + 1) * lanes, 0:ns] = spread(er).astype(BF16)
            out_scr[d * lanes:(d + 1) * lanes, ns:2 * ns] = spread(-ei).astype(BF16)
        nt_dims = (((1,), (1,)), ((), ()))
        hp = lax.Precision.HIGHEST
        r0 = (lax.dot_general(btr, jnp.concatenate(caz_re, axis=0), nt_dims, precision=hp, preferred_element_type=F32)
              - lax.dot_general(bti, jnp.concatenate(caz_im, axis=0), nt_dims, precision=hp, preferred_element_type=F32))
        rg = lax.broadcasted_iota(jnp.int32, r0.shape, 0) // SSM_GROUP
        cg = (lax.broadcasted_iota(jnp.int32, r0.shape, 1) % lanes) // SSM_GROUP
        r0 = jnp.where(rg == cg, r0, 0.0).astype(BF16)
        toep_scr[...] = jnp.zeros_like(toep_scr)
        for i in range(t):
            toep_scr[i * lanes:(i + 1) * lanes, i * lanes:t * lanes] = r0[:, 0:(t - i) * lanes]

    for i in range(t):
        lhs_scr[:, i * lanes:(i + 1) * lanes] = u_ref[pl.ds(i, nc, stride=t), :].astype(BF16)

    s_scr[...] = jnp.dot(lhs_scr[...], inj_scr[...], preferred_element_type=F32)

    dtl = jnp.exp(dtl_ref[0])
    mag = jnp.exp(float(t) * (lrl_ref[0] * dtl))
    ang = float(t) * (lil_ref[0] * dtl)
    ar = mag * jnp.cos(ang)
    ai = mag * jnp.sin(ang)

    def step(n, carry):
        hr, hi = carry
        hp_scr[pl.ds(n, 1), 0:ns] = hr
        hp_scr[pl.ds(n, 1), ns:2 * ns] = hi
        sr = s_scr[pl.ds(n, 1), 0:ns]
        si = s_scr[pl.ds(n, 1), ns:2 * ns]
        return ar * hr - ai * hi + sr, ar * hi + ai * hr + si

    zero = jnp.zeros((1, ns), F32)
    lax.fori_loop(0, nc, step, (zero, zero), unroll=8)

    hprev = hp_scr[...].astype(BF16)
    nt_dims = (((1,), (1,)), ((), ()))
    pair = 2 * lanes
    for jj in range(t // 2):
        kdim = (jj + 1) * pair
        yj = (jnp.dot(lhs_scr[:, 0:kdim], toep_scr[0:kdim, jj * pair:(jj + 1) * pair], preferred_element_type=F32)
              + lax.dot_general(hprev, out_scr[jj * pair:(jj + 1) * pair, :], nt_dims, preferred_element_type=F32))
        y_ref[pl.ds(2 * jj, nc, stride=t), :] = yj[:, 0:lanes]
        y_ref[pl.ds(2 * jj + 1, nc, stride=t), :] = yj[:, lanes:pair]


def _ssm(u, lam_re, lam_im, log_dt, b_re, b_im, c_re, c_im, bsz, lp):
    rows, ds = u.shape
    g, p = lam_re.shape
    t, c = SSM_CHUNK, SSM_GROUP
    lanes = SSM_SLAB * c
    ns = SSM_SLAB * p
    nslab = g // SSM_SLAB
    nc = lp // t
    assert g % SSM_SLAB == 0 and lp % t == 0 and lanes == 128
    rep = lambda x: jnp.repeat(x, c, axis=0).reshape(nslab, lanes, -1)
    lane = lambda x: x.reshape(nslab, 1, ns)
    args = (
        u,
        rep(lam_re), rep(lam_im), rep(log_dt.reshape(g, 1)),
        lane(lam_re), lane(lam_im), lane(jnp.repeat(log_dt, p)),
        jnp.swapaxes(b_re, 1, 2).reshape(nslab, lanes, p), jnp.swapaxes(b_im, 1, 2).reshape(nslab, lanes, p),
        c_re.reshape(nslab, lanes, p), c_im.reshape(nslab, lanes, p),
    )
    rows_spec = pl.BlockSpec((1, lanes, p), lambda s, b: (s, 0, 0))
    lane_spec = pl.BlockSpec((1, 1, ns), lambda s, b: (s, 0, 0))
    y = pl.pallas_call(
        _ssm_kernel,
        grid=(nslab, bsz),
        in_specs=[
            pl.BlockSpec((lp, lanes), lambda s, b: (b, s)),
            rows_spec, rows_spec, pl.BlockSpec((1, lanes, 1), lambda s, b: (s, 0, 0)),
            lane_spec, lane_spec, lane_spec,
            rows_spec, rows_spec, rows_spec, rows_spec,
        ],
        out_specs=pl.BlockSpec((lp, lanes), lambda s, b: (b, s)),
        out_shape=jax.ShapeDtypeStruct((rows, ds), F32),
        scratch_shapes=[
            pltpu.VMEM((nc, t * lanes), BF16),
            pltpu.VMEM((t * lanes, t * lanes), BF16),
            pltpu.VMEM((t * lanes, 2 * ns), BF16),
            pltpu.VMEM((t * lanes, 2 * ns), BF16),
            pltpu.VMEM((nc, 2 * ns), F32),
            pltpu.VMEM((nc, 2 * ns), F32),
        ],
        compiler_params=_params(("parallel", "arbitrary")),
        name="ssm",
    )(*args)
    return y


def _mix_kernel(y_ref, u_ref, a_ref, ga_ref, gb_ref, head_ref, x_ref, dsk_ref, gffn_ref,
                wglu_ref, wao_ref, wout_ref, h_ref, n_ref, *, nt):
    d = x_ref.shape[1]
    y = y_ref[...].astype(F32) + dsk_ref[...] * u_ref[...].astype(F32)
    gy = _gelu_tanh(y).astype(BF16)
    yab = jnp.dot(gy, wglu_ref[...], preferred_element_type=F32)
    ssm_out = yab[:, 0:d] * _sigmoid(yab[:, d:2 * d])
    ao = jnp.dot(a_ref[...], wao_ref[...], preferred_element_type=F32)
    merged = (_sigmoid(ga_ref[...].astype(F32)) * ssm_out
              + _sigmoid(gb_ref[...].astype(F32)) * ao).astype(BF16)
    x = _padded_rows(pl.program_id(0), nt, head_ref, x_ref)
    h = x + jnp.dot(merged, wout_ref[...], preferred_element_type=F32)
    h_ref[...] = h
    ms = jnp.mean(h * h, axis=-1, keepdims=True)
    n_ref[...] = (h * lax.rsqrt(ms + EPS) * gffn_ref[...]).astype(n_ref.dtype)


def _mix(y, z, attn, head, x2, nt, d_skip, g_ffn, w_glu, w_ao, w_out):
    rows, ds = y.shape
    d = x2.shape[1]
    tm = ROW_ALIGN
    u_col = 3 * D_ATTN // ds
    ga_col = (3 * D_ATTN + ds) // d
    assert (3 * D_ATTN) % ds == 0 and (3 * D_ATTN + ds) % d == 0
    return pl.pallas_call(
        functools.partial(_mix_kernel, nt=nt),
        grid=(rows // tm,),
        in_specs=[
            pl.BlockSpec((tm, ds), lambda i: (i, 0)),
            pl.BlockSpec((tm, ds), lambda i: (i, u_col)),
            pl.BlockSpec((tm, D_ATTN), lambda i: (i, 0)),
            pl.BlockSpec((tm, d), lambda i: (i, ga_col)),
            pl.BlockSpec((tm, d), lambda i: (i, ga_col + 1)),
            _resident((tm, d), lambda i: (0, 0)),
            pl.BlockSpec((tm, d), lambda i: (_seq_block(i, nt), 0)),
            _resident((1, ds), lambda i: (0, 0)),
            _resident((1, d), lambda i: (0, 0)),
            _resident(w_glu.shape, lambda i: (0, 0)),
            _resident(w_ao.shape, lambda i: (0, 0)),
            _resident(w_out.shape, lambda i: (0, 0)),
        ],
        out_specs=[
            pl.BlockSpec((tm, d), lambda i: (i, 0)),
            pl.BlockSpec((tm, d), lambda i: (i, 0)),
        ],
        out_shape=[
            jax.ShapeDtypeStruct((rows, d), F32),
            jax.ShapeDtypeStruct((rows, d), BF16),
        ],
        compiler_params=_params(("parallel",)),
        name="mix_out",
    )(y, z, attn, z, z, head, x2, d_skip, g_ffn, w_glu, w_ao, w_out)


def _ffn_up_kernel(n_ref, wg_ref, wu_ref, cw_ref, cb_ref, a_ref, tail, wg_scr, wu_scr):
    tm = n_ref.shape[0]

    @pl.when(pl.program_id(1) == 0)
    def _():
        tail[...] = jnp.zeros_like(tail)
        wg_scr[...] = wg_ref[...].astype(wg_scr.dtype)
        wu_scr[...] = wu_ref[...].astype(wu_scr.dtype)

    n = n_ref[...]
    g = jnp.dot(n, wg_scr[...], preferred_element_type=F32)
    u = jnp.dot(n, wu_scr[...], preferred_element_type=F32)
    row = lax.broadcasted_iota(jnp.int32, g.shape, 0)
    prev1 = tail[CONV_WIDTH - 2:CONV_WIDTH - 1, :]
    prev2 = tail[CONV_WIDTH - 3:CONV_WIDTH - 2, :]
    g1 = jnp.where(row == 0, prev1, pltpu.roll(g, 1, 0))
    g2 = jnp.where(row == 0, prev2, jnp.where(row == 1, prev1, pltpu.roll(g, 2, 0)))
    gc = cb_ref[...] + cw_ref[0:1, :] * g2 + cw_ref[1:2, :] * g1 + cw_ref[2:3, :] * g
    tail[0:CONV_WIDTH - 1, :] = g[tm - (CONV_WIDTH - 1):tm, :]
    a_ref[...] = (gc * _sigmoid(gc) * u).astype(a_ref.dtype)


def _largest_tile(n, candidates):
    return next(c for c in candidates if n % c == 0)


def _ffn_up(n2, w_up, conv_w, conv_b, tn=512):
    rows, d = n2.shape
    dff = conv_w.shape[1]
    nj = dff // tn
    tm = _largest_tile(rows, (1536, 1024, 768, 512, 256))
    assert dff % tn == 0
    return pl.pallas_call(
        _ffn_up_kernel,
        grid=(nj, rows // tm),
        in_specs=[
            pl.BlockSpec((tm, d), lambda j, i: (i, 0)),
            pl.BlockSpec((d, tn), lambda j, i: (0, j)),
            pl.BlockSpec((d, tn), lambda j, i: (0, nj + j)),
            pl.BlockSpec((CONV_WIDTH, tn), lambda j, i: (0, j)),
            pl.BlockSpec((1, tn), lambda j, i: (0, j)),
        ],
        out_specs=pl.BlockSpec((tm, tn), lambda j, i: (i, j)),
        out_shape=jax.ShapeDtypeStruct((rows, dff), BF16),
        scratch_shapes=[pltpu.VMEM((8, tn), F32), pltpu.VMEM((d, tn), BF16), pltpu.VMEM((d, tn), BF16)],
        compiler_params=_params(("parallel", "arbitrary")),
        name="ffn_up",
    )(n2, w_up, w_up, conv_w, conv_b)


def _ffn_down_kernel(a_ref, w_ref, h_ref, g_ref, o_ref):
    h = h_ref[...] + jnp.dot(a_ref[...], w_ref[...], preferred_element_type=F32)
    ms = jnp.mean(h * h, axis=-1, keepdims=True)
    o_ref[...] = (h * lax.rsqrt(ms + EPS) * g_ref[...]).astype(o_ref.dtype)


def _ffn_down(act, w_down, h1, g_final, bsz, lp, tm=ROW_ALIGN):
    dff, d = w_down.shape
    nt_in = lp // tm
    nt_out = nt_in - ROW_ALIGN // tm
    skip = ROW_ALIGN // tm
    return pl.pallas_call(
        _ffn_down_kernel,
        grid=(bsz, nt_out),
        in_specs=[
            pl.BlockSpec((tm, dff), lambda b, t: (b * nt_in + skip + t, 0)),
            _resident((dff, d), lambda b, t: (0, 0)),
            pl.BlockSpec((tm, d), lambda b, t: (b * nt_in + skip + t, 0)),
            _resident((1, d), lambda b, t: (0, 0)),
        ],
        out_specs=pl.BlockSpec((tm, d), lambda b, t: (b * nt_out + t, 0)),
        out_shape=jax.ShapeDtypeStruct((bsz * nt_out * tm, d), F32),
        compiler_params=_params(("parallel", "parallel")),
        name="ffn_down",
    )(act, w_down, h1, g_final)


def _layer(head, x2, bsz, lp, g_mix, w_in, b_f, lam_re, lam_im, log_dt, b_re, b_im, c_re, c_im, d_skip,
           w_glu, w_attn_o, w_out, g_ffn, w_up, conv_w, conv_b, w_down):
    d = x2.shape[1]
    ds = d_skip.shape[0]
    nt = lp // ROW_ALIGN
    o_q, o_k, o_v, o_f, o_u, o_ga, o_gb = (0, D_ATTN, 2 * D_ATTN, 3 * D_ATTN, 3 * D_ATTN + HEADS,
                                            3 * D_ATTN + HEADS + ds, 3 * D_ATTN + HEADS + ds + d)
    w_main, w_f = _win_pack(w_in, o_f, o_u, HEAD_DIM ** -0.5 * LOG2E)
    bf = jnp.pad(b_f.astype(F32), (0, 128 - HEADS)).reshape(1, 128)

    z, u, qb, kb = _inproj(head, x2, g_mix.reshape(1, d), w_main, w_f, bf, bsz * lp, nt, 3 * D_ATTN, ds)
    attn = _flash(z, qb, kb, bsz, lp)
    y = _ssm(u, lam_re, lam_im, log_dt, b_re, b_im, c_re, c_im, bsz, lp)

    h1, n2 = _mix(y, z, attn, head, x2, nt, d_skip.reshape(1, ds).astype(F32), g_ffn.reshape(1, d).astype(F32),
                  w_glu.astype(BF16), w_attn_o.astype(BF16), w_out.astype(BF16))
    act = _ffn_up(n2, w_up.astype(F32), conv_w.astype(F32), conv_b.reshape(1, -1).astype(F32))
    return act, h1


def kernel(x, meta, g_mix, w_in, b_f, lam_re, lam_im, log_dt, b_re, b_im, c_re, c_im, d_skip,
           w_glu, w_attn_o, w_out, g_ffn, w_up, conv_w, conv_b, w_down, g_final):
    bsz, seq, d = x.shape
    depth = g_mix.shape[0]
    assert depth == 1 and meta.shape[0] == N_META and seq % ROW_ALIGN == 0
    lp = seq + ROW_ALIGN
    assert (bsz * lp) % 512 == 0
    head = jnp.concatenate([jnp.zeros((PAD, d), x.dtype), meta.astype(x.dtype)], axis=0)
    act, h1 = _layer(head, x.reshape(bsz * seq, d), bsz, lp, g_mix[0], w_in[0], b_f[0], lam_re[0], lam_im[0], log_dt[0],
                     b_re[0], b_im[0], c_re[0], c_im[0], d_skip[0], w_glu[0], w_attn_o[0], w_out[0],
                     g_ffn[0], w_up[0], conv_w[0], conv_b[0], w_down[0])
    out = _ffn_down(act, w_down[0].astype(BF16), h1, g_final.reshape(1, d).astype(F32), bsz, lp)
    return out.reshape(bsz, seq, d)
```

```python
import functools
import math

import jax
import jax.numpy as jnp
from jax import lax
from jax.experimental import pallas as pl
from jax.experimental.pallas import tpu as pltpu

N_META = 16
HEADS = 8
HEAD_DIM = 128
D_ATTN = HEADS * HEAD_DIM
SSM_GROUP = 16
SSM_STATE = 64
SSM_CHUNK = 12
SSM_SLAB = 8
CONV_WIDTH = 3
EPS = 1e-6

ROW_ALIGN = 256
PAD = ROW_ALIGN - N_META
ATT_TQ = 1024
ATT_TK = 512
QK_AUG = 2 * HEAD_DIM
LOG2E = math.log2(math.e)
MASK_BIG = 1e30
NEG = -3e38
VMEM_LIMIT = 56 * 1024 * 1024

F32 = jnp.float32
BF16 = jnp.bfloat16


def _sigmoid(x):
    return 1.0 / (1.0 + jnp.exp(-x))


def _gelu_tanh(x):
    c = math.sqrt(2.0 / math.pi)
    return 0.5 * x * (1.0 + jnp.tanh(c * (x + 0.044715 * (x * x * x))))


def _params(sem, limit=VMEM_LIMIT):
    return pltpu.CompilerParams(dimension_semantics=sem, vmem_limit_bytes=limit)


def _resident(shape, index_map):
    return pl.BlockSpec(shape, index_map, pipeline_mode=pl.Buffered(1))


def _win_pack_kernel(wt_ref, o_ref, f_ref, *, o_f, o_u, q_scale):
    n_in, tr = wt_ref.shape
    o_ref[:, 0:D_ATTN] = (wt_ref[0:D_ATTN, :] * q_scale).T.astype(o_ref.dtype)
    o_ref[:, D_ATTN:o_f] = wt_ref[D_ATTN:o_f, :].T.astype(o_ref.dtype)
    o_ref[:, o_f:o_f + (n_in - o_u)] = wt_ref[o_u:n_in, :].T.astype(o_ref.dtype)
    wf = jnp.concatenate([wt_ref[o_f:o_u, :], jnp.zeros((f_ref.shape[1] - (o_u - o_f), tr), F32)], axis=0)
    f_ref[...] = wf.T.astype(f_ref.dtype)


def _win_pack(w_in, o_f, o_u, q_scale, tr=256):
    d, n_in = w_in.shape
    n_out = n_in - (o_u - o_f)
    return pl.pallas_call(
        functools.partial(_win_pack_kernel, o_f=o_f, o_u=o_u, q_scale=q_scale),
        grid=(d // tr,),
        in_specs=[pl.BlockSpec((n_in, tr), lambda i: (0, i))],
        out_specs=[pl.BlockSpec((tr, n_out), lambda i: (i, 0)), pl.BlockSpec((tr, 128), lambda i: (i, 0))],
        out_shape=[jax.ShapeDtypeStruct((d, n_out), BF16), jax.ShapeDtypeStruct((d, 128), BF16)],
        compiler_params=_params(("parallel",)),
        name="win_pack",
    )(w_in.T)


def _seq_block(t, nt):
    return (t // nt) * (nt - 1) + jnp.maximum(t % nt - 1, 0)


def _padded_rows(t, nt, head_ref, x_ref):
    return jnp.where(t % nt == 0, head_ref[...], x_ref[...])


def _split3(x):
    hi = x.astype(BF16)
    r1 = x - hi.astype(F32)
    mid = r1.astype(BF16)
    lo = (r1 - mid.astype(F32)).astype(BF16)
    return hi, mid, lo


def _forget_bias(f, bf_ref, t, nt, carry, qb_ref, kb_ref, row0):
    tm = f.shape[0]
    first = t % nt == 0
    x = f + bf_ref[...]
    logf = jnp.minimum(x, 0.0) - jnp.log1p(jnp.exp(-jnp.abs(x)))
    row = lax.broadcasted_iota(jnp.int32, (tm, 1), 0)
    valid = jnp.logical_or(jnp.logical_not(first), row >= PAD)
    logf = jnp.where(valid, logf, 0.0)
    r = lax.broadcasted_iota(jnp.int32, (tm, tm), 0)
    c = lax.broadcasted_iota(jnp.int32, (tm, tm), 1)
    tri = (r >= c).astype(BF16)
    cs3 = jnp.dot(tri, jnp.concatenate(_split3(logf), axis=1), preferred_element_type=F32)
    w = f.shape[1]
    fcum = cs3[:, 0:w] + cs3[:, w:2 * w] + cs3[:, 2 * w:3 * w] + jnp.where(first, 0.0, carry[0:1, :])
    carry[0:1, :] = fcum[tm - 1:tm, :]

    lane = lax.broadcasted_iota(jnp.int32, (tm, HEAD_DIM), 1)
    one = jnp.ones((tm, HEAD_DIM), F32)
    zero = jnp.zeros((tm, HEAD_DIM), F32)
    for h in range(HEADS):
        hi, mid, lo = (p.astype(F32) for p in _split3(fcum[:, h:h + 1] * LOG2E))
        qb = jnp.where(lane == 0, hi, jnp.where(lane == 1, mid, jnp.where(lane == 2, lo,
             jnp.where(lane < 6, one, zero))))
        khi = jnp.where(valid, hi, MASK_BIG)
        kmid = jnp.where(valid, mid, 0.0)
        klo = jnp.where(valid, lo, 0.0)
        kb = jnp.where(lane < 3, one, jnp.where(lane == 3, -khi, jnp.where(lane == 4, -kmid,
             jnp.where(lane == 5, -klo, zero))))
        qb_ref[h, row0:row0 + tm, :] = qb.astype(qb_ref.dtype)
        kb_ref[h, row0:row0 + tm, :] = kb.astype(kb_ref.dtype)


def _inproj_kernel(head_ref, xa_ref, xb_ref, g_ref, w_ref, wf_ref, bf_ref, z_ref, u_ref, qb_ref, kb_ref,
                   n_scr, carry, *, nt, u_blk, u_lo):
    i = pl.program_id(0)
    j = pl.program_id(1)
    n_i = pl.num_programs(0)
    n_j = pl.num_programs(1)

    def prepare(t):
        x = jnp.concatenate([_padded_rows(2 * t, nt, head_ref, xa_ref),
                             _padded_rows(2 * t + 1, nt, head_ref, xb_ref)], axis=0)
        ms = jnp.mean(x * x, axis=-1, keepdims=True)
        n = (x * lax.rsqrt(ms + EPS) * g_ref[...]).astype(BF16)
        n_scr[t % 2] = n
        f = jnp.dot(n, wf_ref[...], preferred_element_type=F32)
        for blk in range(2):
            _forget_bias(f[blk * ROW_ALIGN:(blk + 1) * ROW_ALIGN], bf_ref, 2 * t + blk, nt, carry,
                         qb_ref, kb_ref, blk * ROW_ALIGN)

    def project():
        acc = jnp.dot(n_scr[i % 2], w_ref[...], preferred_element_type=F32)
        z_ref[...] = acc.astype(z_ref.dtype)
        return acc

    @pl.when(jnp.logical_and(i == 0, j == 0))
    def _():
        carry[...] = jnp.zeros_like(carry)
        prepare(0)

    lookahead = jnp.logical_and(j == n_j - 1, i + 1 < n_i)

    @pl.when(lookahead)
    def _():
        prepare(i + 1)
        project()

    @pl.when(jnp.logical_not(lookahead))
    def _():
        acc = project()

        @pl.when(j == u_blk)
        def _():
            u_ref[...] = acc[:, u_lo:u_lo + u_ref.shape[1]]


def _inproj(head, x2, g, w, wf, bf, rows, nt, u_off, u_width, tn=2048):
    d = x2.shape[1]
    tm = 2 * ROW_ALIGN
    n_out = w.shape[1]
    n_i, n_j = rows // tm, n_out // tn
    u_blk, u_lo = divmod(u_off, tn)
    assert u_lo + u_width <= tn and rows % tm == 0 and u_blk < n_j - 1

    def tile(i, j):
        return jnp.minimum(i + (j == n_j - 1).astype(jnp.int32), n_i - 1)

    bias_spec = pl.BlockSpec((HEADS, tm, HEAD_DIM), lambda i, j: (0, tile(i, j), 0))
    bias_shape = jax.ShapeDtypeStruct((HEADS, rows, HEAD_DIM), BF16)
    return pl.pallas_call(
        functools.partial(_inproj_kernel, nt=nt, u_blk=u_blk, u_lo=u_lo),
        grid=(n_i, n_j),
        in_specs=[
            _resident((ROW_ALIGN, d), lambda i, j: (0, 0)),
            pl.BlockSpec((ROW_ALIGN, d), lambda i, j: (_seq_block(2 * tile(i, j), nt), 0)),
            pl.BlockSpec((ROW_ALIGN, d), lambda i, j: (_seq_block(2 * tile(i, j) + 1, nt), 0)),
            pl.BlockSpec((1, d), lambda i, j: (0, 0)),
            pl.BlockSpec((d, tn), lambda i, j: (0, j)),
            pl.BlockSpec((d, 128), lambda i, j: (0, 0)),
            pl.BlockSpec((1, 128), lambda i, j: (0, 0)),
        ],
        out_specs=[
            pl.BlockSpec((tm, tn), lambda i, j: (i, j)),
            pl.BlockSpec((tm, u_width), lambda i, j: (i, 0)),
            bias_spec, bias_spec,
        ],
        out_shape=[
            jax.ShapeDtypeStruct((rows, n_out), BF16),
            jax.ShapeDtypeStruct((rows, u_width), F32),
            bias_shape, bias_shape,
        ],
        scratch_shapes=[pltpu.VMEM((2, tm, d), BF16), pltpu.VMEM((8, 128), F32)],
        compiler_params=_params(("arbitrary", "arbitrary")),
        name="inproj",
    )(head, x2, x2, g, w, wf, bf)


def _flash_kernel(q_ref, k_ref, v_ref, qb_ref, kb_ref, o_ref,
                  qa_ref, ka_ref, vt_ref, sa_ref, sb_ref, mxa_ref, mxb_ref, m_ref, acc_ref):
    lp = v_ref.shape[0]
    tq, tk = ATT_TQ, ATT_TK
    n_main = (lp - ROW_ALIGN) // tq
    n_chunks = (lp - ROW_ALIGN) // tk
    va = vt_ref.shape[1]
    nt_dims = (((1,), (1,)), ((), ()))

    qa_ref[:, 0:HEAD_DIM] = q_ref[...]
    qa_ref[:, HEAD_DIM:QK_AUG] = qb_ref[0]
    ka_ref[:, 0:HEAD_DIM] = k_ref[...]
    ka_ref[:, HEAD_DIM:QK_AUG] = kb_ref[0]

    ones = jnp.ones((va - HEAD_DIM, tk), vt_ref.dtype)
    vt_ref[0, 0:HEAD_DIM, 0:ROW_ALIGN] = v_ref[0:ROW_ALIGN, :].T
    vt_ref[0, HEAD_DIM:va, :] = ones

    def fill(c, _):
        vt_ref[1 + c, 0:HEAD_DIM, :] = v_ref[pl.ds(pl.multiple_of(ROW_ALIGN + c * tk, ROW_ALIGN), tk), :].T
        vt_ref[1 + c, HEAD_DIM:va, :] = ones
        return 0

    lax.fori_loop(0, n_chunks, fill, 0)

    buf_a = (sa_ref, mxa_ref)
    buf_b = (sb_ref, mxb_ref)

    def scores(buf, q_off, nq, k_off, nk, col0):
        s_ref, mx_ref = buf
        kc = ka_ref[pl.ds(k_off, nk), :]
        qc = qa_ref[pl.ds(q_off, nq), :]
        s = lax.dot_general(kc, qc, nt_dims, preferred_element_type=F32)
        s_ref[0:nk, col0:col0 + nq] = s
        mx_ref[0:1, col0:col0 + nq] = jnp.max(s, axis=0, keepdims=True)

    def absorb(buf, slot, nk, lo, hi, mask_shift=None, lane0=0):
        s_ref, mx_ref = buf
        s = s_ref[0:nk, lo:hi]
        if mask_shift is None:
            smax = mx_ref[0:1, lo:hi]
        else:
            kr = lax.broadcasted_iota(jnp.int32, s.shape, 0) + mask_shift
            qc = lax.broadcasted_iota(jnp.int32, s.shape, 1) + lo
            s = jnp.where(kr > qc, NEG, s)
            smax = jnp.max(s, axis=0, keepdims=True)
        m = m_ref[0:1, lo:hi]
        m_new = jnp.maximum(m, smax)
        alpha = jnp.exp2(m - m_new)
        p = jnp.exp2(s - m_new).astype(BF16)
        m_ref[0:1, lo:hi] = m_new
        pv = jnp.dot(vt_ref[slot, :, lane0:lane0 + nk], p, preferred_element_type=F32)
        acc_ref[:, lo:hi] = alpha * acc_ref[:, lo:hi] + pv

    def reset(n):
        m_ref[0:1, 0:n] = jnp.full((1, n), NEG, F32)
        acc_ref[:, 0:n] = jnp.zeros((va, n), F32)

    def finish(q_off, n):
        out = acc_ref[0:HEAD_DIM, 0:n] / acc_ref[HEAD_DIM:HEAD_DIM + 1, 0:n]
        o_ref[pl.ds(q_off, n), :] = out.T.astype(o_ref.dtype)

    reset(ROW_ALIGN)
    scores(buf_a, 0, ROW_ALIGN, 0, ROW_ALIGN, 0)
    absorb(buf_a, 0, ROW_ALIGN, 0, ROW_ALIGN, mask_shift=0)
    finish(0, ROW_ALIGN)

    per_tile = tq // tk

    def tile(j, _):
        q_off = pl.multiple_of(ROW_ALIGN + j * tq, ROW_ALIGN)
        reset(tq)
        scores(buf_a, q_off, tq, PAD, N_META, 0)
        scores(buf_b, q_off, tq, ROW_ALIGN, tk, 0)
        absorb(buf_a, 0, N_META, 0, tq, lane0=PAD)

        def pair(p):
            kb = pl.multiple_of(ROW_ALIGN + p * (2 * tk), ROW_ALIGN)
            scores(buf_a, q_off, tq, kb + tk, tk, 0)
            absorb(buf_b, 1 + 2 * p, tk, 0, tq)
            scores(buf_b, q_off, tq, kb + 2 * tk, tk, 0)
            absorb(buf_a, 2 + 2 * p, tk, 0, tq)

        def two_pairs(pp, _):
            pair(2 * pp)
            pair(2 * pp + 1)
            return 0

        n_pairs = j * (per_tile // 2)
        lax.fori_loop(0, n_pairs // 2, two_pairs, 0)

        @pl.when(n_pairs % 2 == 1)
        def _():
            pair(n_pairs - 1)
        scores(buf_a, q_off + tk, tq - tk, q_off + tk, tk, tk)
        absorb(buf_b, 1 + j * per_tile, tk, 0, tq, mask_shift=0)
        absorb(buf_a, 2 + j * per_tile, tk, tk, tq, mask_shift=tk)
        finish(q_off, tq)
        return 0

    lax.fori_loop(0, n_main, tile, 0)


def _flash(z, qb, kb, bsz, lp):
    assert (lp - ROW_ALIGN) % ATT_TQ == 0 and ATT_TQ == 2 * ATT_TK
    bias_spec = pl.BlockSpec((1, lp, HEAD_DIM), lambda b, h: (h, b, 0))
    va = HEAD_DIM + 16
    return pl.pallas_call(
        _flash_kernel,
        grid=(bsz, HEADS),
        in_specs=[
            pl.BlockSpec((lp, HEAD_DIM), lambda b, h: (b, h)),
            pl.BlockSpec((lp, HEAD_DIM), lambda b, h: (b, HEADS + h)),
            pl.BlockSpec((lp, HEAD_DIM), lambda b, h: (b, 2 * HEADS + h)),
            bias_spec, bias_spec,
        ],
        out_specs=pl.BlockSpec((lp, HEAD_DIM), lambda b, h: (b, h)),
        out_shape=jax.ShapeDtypeStruct((bsz * lp, D_ATTN), BF16),
        scratch_shapes=[
            pltpu.VMEM((lp, QK_AUG), BF16),
            pltpu.VMEM((lp, QK_AUG), BF16),
            pltpu.VMEM((1 + (lp - ROW_ALIGN) // ATT_TK, va, ATT_TK), BF16),
            pltpu.VMEM((ATT_TK, ATT_TQ + 128), F32),
            pltpu.VMEM((ATT_TK, ATT_TQ + 128), F32),
            pltpu.VMEM((8, ATT_TQ), F32),
            pltpu.VMEM((8, ATT_TQ), F32),
            pltpu.VMEM((8, ATT_TQ), F32),
            pltpu.VMEM((va, ATT_TQ + 128), F32),
        ],
        compiler_params=_params(("parallel", "parallel")),
        name="flash",
    )(z, z, z, qb, kb)


def _tile_lanes(x, reps):
    return jnp.concatenate([x] * reps, axis=1)


def _ssm_kernel(u_ref, lr_ref, li_ref, dt_ref, lrl_ref, lil_ref, dtl_ref, btr_ref, bti_ref, cr_ref, ci_ref,
                y_ref, lhs_scr, toep_scr, inj_scr, out_scr, s_scr, hp_scr):
    t = SSM_CHUNK
    lanes = SSM_SLAB * SSM_GROUP
    ns = SSM_SLAB * SSM_STATE
    nc = u_ref.shape[0] // t

    @pl.when(pl.program_id(1) == 0)
    def _build_weights():
        lr = lr_ref[0]
        li = li_ref[0]
        dt = jnp.exp(dt_ref[0])
        btr, bti = btr_ref[0], bti_ref[0]
        cr, ci = cr_ref[0], ci_ref[0]

        mag = jnp.exp(lr * dt)
        a_re = mag * jnp.cos(li * dt)
        a_im = mag * jnp.sin(li * dt)
        powers = [(jnp.ones_like(a_re), jnp.zeros_like(a_re))]
        for _ in range(t):
            pr, pi = powers[-1]
            powers.append((pr * a_re - pi * a_im, pr * a_im + pi * a_re))
        den = lr * lr + li * li
        nr = a_re - 1.0
        z_re = (nr * lr + a_im * li) / den
        z_im = (a_im * lr - nr * li) / den
        row_g = lax.broadcasted_iota(jnp.int32, (lanes, ns), 0) // SSM_GROUP
        col_g = lax.broadcasted_iota(jnp.int32, (lanes, ns), 1) // SSM_STATE
        same = row_g == col_g

        def spread(x):
            return jnp.where(same, _tile_lanes(x, SSM_SLAB), 0.0)

        caz_re, caz_im = [], []
        for d in range(t):
            pr, pi = powers[d]
            azr = pr * z_re - pi * z_im
            azi = pr * z_im + pi * z_re
            caz_re.append(cr * azr - ci * azi)
            caz_im.append(cr * azi + ci * azr)
            i = t - 1 - d
            sr = btr * azr - bti * azi
            si = btr * azi + bti * azr
            inj_scr[i * lanes:(i + 1) * lanes, 0:ns] = spread(sr).astype(BF16)
            inj_scr[i * lanes:(i + 1) * lanes, ns:2 * ns] = spread(si).astype(BF16)
            qr, qi = powers[d + 1]
            er = cr * qr - ci * qi
            ei = cr * qi + ci * qr
            out_scr[d * lanes:(d + 1) * lanes, 0:ns] = spread(er).astype(BF16)
            out_scr[d * lanes:(d + 1) * lanes, ns:2 * ns] = spread(-ei).astype(BF16)
        nt_dims = (((1,), (1,)), ((), ()))
        hp = lax.Precision.HIGHEST
        r0 = (lax.dot_general(btr, jnp.concatenate(caz_re, axis=0), nt_dims, precision=hp, preferred_element_type=F32)
              - lax.dot_general(bti, jnp.concatenate(caz_im, axis=0), nt_dims, precision=hp, preferred_element_type=F32))
        rg = lax.broadcasted_iota(jnp.int32, r0.shape, 0) // SSM_GROUP
        cg = (lax.broadcasted_iota(jnp.int32, r0.shape, 1) % lanes) // SSM_GROUP
        r0 = jnp.where(rg == cg, r0, 0.0).astype(BF16)
        toep_scr[...] = jnp.zeros_like(toep_scr)
        for i in range(t):
            toep_scr[i * lanes:(i + 1) * lanes, i * lanes:t * lanes] = r0[:, 0:(t - i) * lanes]

    for i in range(t):
        lhs_scr[:, i * lanes:(i + 1) * lanes] = u_ref[pl.ds(i, nc, stride=t), :].astype(BF16)

    s_scr[...] = jnp.dot(lhs_scr[...], inj_scr[...], preferred_element_type=F32)

    dtl = jnp.exp(dtl_ref[0])
    mag = jnp.exp(float(t) * (lrl_ref[0] * dtl))
    ang = float(t) * (lil_ref[0] * dtl)
    ar = mag * jnp.cos(ang)
    ai = mag * jnp.sin(ang)

    def step(n, carry):
        hr, hi = carry
        hp_scr[pl.ds(n, 1), 0:ns] = hr
        hp_scr[pl.ds(n, 1), ns:2 * ns] = hi
        sr = s_scr[pl.ds(n, 1), 0:ns]
        si = s_scr[pl.ds(n, 1), ns:2 * ns]
        return ar * hr - ai * hi + sr, ar * hi + ai * hr + si

    zero = jnp.zeros((1, ns), F32)
    lax.fori_loop(0, nc, step, (zero, zero), unroll=8)

    hprev = hp_scr[...].astype(BF16)
    nt_dims = (((1,), (1,)), ((), ()))
    pair = 2 * lanes
    for jj in range(t // 2):
        kdim = (jj + 1) * pair
        yj = (jnp.dot(lhs_scr[:, 0:kdim], toep_scr[0:kdim, jj * pair:(jj + 1) * pair], preferred_element_type=F32)
              + lax.dot_general(hprev, out_scr[jj * pair:(jj + 1) * pair, :], nt_dims, preferred_element_type=F32))
        y_ref[pl.ds(2 * jj, nc, stride=t), :] = yj[:, 0:lanes]
        y_ref[pl.ds(2 * jj + 1, nc, stride=t), :] = yj[:, lanes:pair]


def _ssm(u, lam_re, lam_im, log_dt, b_re, b_im, c_re, c_im, bsz, lp):
    rows, ds = u.shape
    g, p = lam_re.shape
    t, c = SSM_CHUNK, SSM_GROUP
    lanes = SSM_SLAB * c
    ns = SSM_SLAB * p
    nslab = g // SSM_SLAB
    nc = lp // t
    assert g % SSM_SLAB == 0 and lp % t == 0 and lanes == 128
    rep = lambda x: jnp.repeat(x, c, axis=0).reshape(nslab, lanes, -1)
    lane = lambda x: x.reshape(nslab, 1, ns)
    args = (
        u,
        rep(lam_re), rep(lam_im), rep(log_dt.reshape(g, 1)),
        lane(lam_re), lane(lam_im), lane(jnp.repeat(log_dt, p)),
        jnp.swapaxes(b_re, 1, 2).reshape(nslab, lanes, p), jnp.swapaxes(b_im, 1, 2).reshape(nslab, lanes, p),
        c_re.reshape(nslab, lanes, p), c_im.reshape(nslab, lanes, p),
    )
    rows_spec = pl.BlockSpec((1, lanes, p), lambda s, b: (s, 0, 0))
    lane_spec = pl.BlockSpec((1, 1, ns), lambda s, b: (s, 0, 0))
    y = pl.pallas_call(
        _ssm_kernel,
        grid=(nslab, bsz),
        in_specs=[
            pl.BlockSpec((lp, lanes), lambda s, b: (b, s)),
            rows_spec, rows_spec, pl.BlockSpec((1, lanes, 1), lambda s, b: (s, 0, 0)),
            lane_spec, lane_spec, lane_spec,
            rows_spec, rows_spec, rows_spec, rows_spec,
        ],
        out_specs=pl.BlockSpec((lp, lanes), lambda s, b: (b, s)),
        out_shape=jax.ShapeDtypeStruct((rows, ds), F32),
        scratch_shapes=[
            pltpu.VMEM((nc, t * lanes), BF16),
            pltpu.VMEM((t * lanes, t * lanes), BF16),
            pltpu.VMEM((t * lanes, 2 * ns), BF16),
            pltpu.VMEM((t * lanes, 2 * ns), BF16),
            pltpu.VMEM((nc, 2 * ns), F32),
            pltpu.VMEM((nc, 2 * ns), F32),
        ],
        compiler_params=_params(("parallel", "arbitrary")),
        name="ssm",
    )(*args)
    return y


def _mix_kernel(y_ref, u_ref, a_ref, ga_ref, gb_ref, head_ref, x_ref, dsk_ref, gffn_ref,
                wglu_ref, wao_ref, wout_ref, h_ref, n_ref, *, nt):
    d = x_ref.shape[1]
    y = y_ref[...].astype(F32) + dsk_ref[...] * u_ref[...].astype(F32)
    gy = _gelu_tanh(y).astype(BF16)
    yab = jnp.dot(gy, wglu_ref[...], preferred_element_type=F32)
    ssm_out = yab[:, 0:d] * _sigmoid(yab[:, d:2 * d])
    ao = jnp.dot(a_ref[...], wao_ref[...], preferred_element_type=F32)
    merged = (_sigmoid(ga_ref[...].astype(F32)) * ssm_out
              + _sigmoid(gb_ref[...].astype(F32)) * ao).astype(BF16)
    x = _padded_rows(pl.program_id(0), nt, head_ref, x_ref)
    h = x + jnp.dot(merged, wout_ref[...], preferred_element_type=F32)
    h_ref[...] = h
    ms = jnp.mean(h * h, axis=-1, keepdims=True)
    n_ref[...] = (h * lax.rsqrt(ms + EPS) * gffn_ref[...]).astype(n_ref.dtype)


def _mix(y, z, attn, head, x2, nt, d_skip, g_ffn, w_glu, w_ao, w_out):
    rows, ds = y.shape
    d = x2.shape[1]
    tm = ROW_ALIGN
    u_col = 3 * D_ATTN // ds
    ga_col = (3 * D_ATTN + ds) // d
    assert (3 * D_ATTN) % ds == 0 and (3 * D_ATTN + ds) % d == 0
    return pl.pallas_call(
        functools.partial(_mix_kernel, nt=nt),
        grid=(rows // tm,),
        in_specs=[
            pl.BlockSpec((tm, ds), lambda i: (i, 0)),
            pl.BlockSpec((tm, ds), lambda i: (i, u_col)),
            pl.BlockSpec((tm, D_ATTN), lambda i: (i, 0)),
            pl.BlockSpec((tm, d), lambda i: (i, ga_col)),
            pl.BlockSpec((tm, d), lambda i: (i, ga_col + 1)),
            _resident((tm, d), lambda i: (0, 0)),
            pl.BlockSpec((tm, d), lambda i: (_seq_block(i, nt), 0)),
            _resident((1, ds), lambda i: (0, 0)),
            _resident((1, d), lambda i: (0, 0)),
            _resident(w_glu.shape, lambda i: (0, 0)),
            _resident(w_ao.shape, lambda i: (0, 0)),
            _resident(w_out.shape, lambda i: (0, 0)),
        ],
        out_specs=[
            pl.BlockSpec((tm, d), lambda i: (i, 0)),
            pl.BlockSpec((tm, d), lambda i: (i, 0)),
        ],
        out_shape=[
            jax.ShapeDtypeStruct((rows, d), F32),
            jax.ShapeDtypeStruct((rows, d), BF16),
        ],
        compiler_params=_params(("parallel",)),
        name="mix_out",
    )(y, z, attn, z, z, head, x2, d_skip, g_ffn, w_glu, w_ao, w_out)


def _ffn_up_kernel(n_ref, wg_ref, wu_ref, cw_ref, cb_ref, a_ref, tail, wg_scr, wu_scr):
    tm = n_ref.shape[0]

    @pl.when(pl.program_id(1) == 0)
    def _():
        tail[...] = jnp.zeros_like(tail)
        wg_scr[...] = wg_ref[...].astype(wg_scr.dtype)
        wu_scr[...] = wu_ref[...].astype(wu_scr.dtype)

    n = n_ref[...]
    g = jnp.dot(n, wg_scr[...], preferred_element_type=F32)
    u = jnp.dot(n, wu_scr[...], preferred_element_type=F32)
    row = lax.broadcasted_iota(jnp.int32, g.shape, 0)
    prev1 = tail[CONV_WIDTH - 2:CONV_WIDTH - 1, :]
    prev2 = tail[CONV_WIDTH - 3:CONV_WIDTH - 2, :]
    g1 = jnp.where(row == 0, prev1, pltpu.roll(g, 1, 0))
    g2 = jnp.where(row == 0, prev2, jnp.where(row == 1, prev1, pltpu.roll(g, 2, 0)))
    gc = cb_ref[...] + cw_ref[0:1, :] * g2 + cw_ref[1:2, :] * g1 + cw_ref[2:3, :] * g
    tail[0:CONV_WIDTH - 1, :] = g[tm - (CONV_WIDTH - 1):tm, :]
    a_ref[...] = (gc * _sigmoid(gc) * u).astype(a_ref.dtype)


def _largest_tile(n, candidates):
    return next(c for c in candidates if n % c == 0)


def _ffn_up(n2, w_up, conv_w, conv_b, tn=512):
    rows, d = n2.shape
    dff = conv_w.shape[1]
    nj = dff // tn
    tm = _largest_tile(rows, (1536, 1024, 768, 512, 256))
    assert dff % tn == 0
    return pl.pallas_call(
        _ffn_up_kernel,
        grid=(nj, rows // tm),
        in_specs=[
            pl.BlockSpec((tm, d), lambda j, i: (i, 0)),
            pl.BlockSpec((d, tn), lambda j, i: (0, j)),
            pl.BlockSpec((d, tn), lambda j, i: (0, nj + j)),
            pl.BlockSpec((CONV_WIDTH, tn), lambda j, i: (0, j)),
            pl.BlockSpec((1, tn), lambda j, i: (0, j)),
        ],
        out_specs=pl.BlockSpec((tm, tn), lambda j, i: (i, j)),
        out_shape=jax.ShapeDtypeStruct((rows, dff), BF16),
        scratch_shapes=[pltpu.VMEM((8, tn), F32), pltpu.VMEM((d, tn), BF16), pltpu.VMEM((d, tn), BF16)],
        compiler_params=_params(("parallel", "arbitrary")),
        name="ffn_up",
    )(n2, w_up, w_up, conv_w, conv_b)


def _ffn_down_kernel(a_ref, w_ref, h_ref, g_ref, o_ref):
    h = h_ref[...] + jnp.dot(a_ref[...], w_ref[...], preferred_element_type=F32)
    ms = jnp.mean(h * h, axis=-1, keepdims=True)
    o_ref[...] = (h * lax.rsqrt(ms + EPS) * g_ref[...]).astype(o_ref.dtype)


def _ffn_down(act, w_down, h1, g_final, bsz, lp, tm=ROW_ALIGN):
    dff, d = w_down.shape
    nt_in = lp // tm
    nt_out = nt_in - ROW_ALIGN // tm
    skip = ROW_ALIGN // tm
    return pl.pallas_call(
        _ffn_down_kernel,
        grid=(bsz, nt_out),
        in_specs=[
            pl.BlockSpec((tm, dff), lambda b, t: (b * nt_in + skip + t, 0)),
            _resident((dff, d), lambda b, t: (0, 0)),
            pl.BlockSpec((tm, d), lambda b, t: (b * nt_in + skip + t, 0)),
            _resident((1, d), lambda b, t: (0, 0)),
        ],
        out_specs=pl.BlockSpec((tm, d), lambda b, t: (b * nt_out + t, 0)),
        out_shape=jax.ShapeDtypeStruct((bsz * nt_out * tm, d), F32),
        compiler_params=_params(("parallel", "parallel")),
        name="ffn_down",
    )(act, w_down, h1, g_final)


def _layer(head, x2, bsz, lp, g_mix, w_in, b_f, lam_re, lam_im, log_dt, b_re, b_im, c_re, c_im, d_skip,
           w_glu, w_attn_o, w_out, g_ffn, w_up, conv_w, conv_b, w_down):
    d = x2.shape[1]
    ds = d_skip.shape[0]
    nt = lp // ROW_ALIGN
    o_q, o_k, o_v, o_f, o_u, o_ga, o_gb = (0, D_ATTN, 2 * D_ATTN, 3 * D_ATTN, 3 * D_ATTN + HEADS,
                                            3 * D_ATTN + HEADS + ds, 3 * D_ATTN + HEADS + ds + d)
    w_main, w_f = _win_pack(w_in, o_f, o_u, HEAD_DIM ** -0.5 * LOG2E)
    bf = jnp.pad(b_f.astype(F32), (0, 128 - HEADS)).reshape(1, 128)

    z, u, qb, kb = _inproj(head, x2, g_mix.reshape(1, d), w_main, w_f, bf, bsz * lp, nt, 3 * D_ATTN, ds)
    attn = _flash(z, qb, kb, bsz, lp)
    y = _ssm(u, lam_re, lam_im, log_dt, b_re, b_im, c_re, c_im, bsz, lp)

    h1, n2 = _mix(y, z, attn, head, x2, nt, d_skip.reshape(1, ds).astype(F32), g_ffn.reshape(1, d).astype(F32),
                  w_glu.astype(BF16), w_attn_o.astype(BF16), w_out.astype(BF16))
    act = _ffn_up(n2, w_up.astype(F32), conv_w.astype(F32), conv_b.reshape(1, -1).astype(F32))
    return act, h1


def kernel(x, meta, g_mix, w_in, b_f, lam_re, lam_im, log_dt, b_re, b_im, c_re, c_im, d_skip,
           w_glu, w_attn_o, w_out, g_ffn, w_up, conv_w, conv_b, w_down, g_final):
    bsz, seq, d = x.shape
    depth = g_mix.shape[0]
    assert depth == 1 and meta.shape[0] == N_META and seq % ROW_ALIGN == 0
    lp = seq + ROW_ALIGN
    assert (bsz * lp) % 512 == 0
    head = jnp.concatenate([jnp.zeros((PAD, d), x.dtype), meta.astype(x.dtype)], axis=0)
    act, h1 = _layer(head, x.reshape(bsz * seq, d), bsz, lp, g_mix[0], w_in[0], b_f[0], lam_re[0], lam_im[0], log_dt[0],
                     b_re[0], b_im[0], c_re[0], c_im[0], d_skip[0], w_glu[0], w_attn_o[0], w_out[0],
                     g_ffn[0], w_up[0], conv_w[0], conv_b[0], w_down[0])
    out = _ffn_down(act, w_down[0].astype(BF16), h1, g_final.reshape(1, d).astype(F32), bsz, lp)
    return out.reshape(bsz, seq, d)
```

```python
import functools
import math

import jax
import jax.numpy as jnp
from jax import lax
from jax.experimental import pallas as pl
from jax.experimental.pallas import tpu as pltpu

N_META = 16
HEADS = 8
HEAD_DIM = 128
D_ATTN = HEADS * HEAD_DIM
SSM_GROUP = 16
SSM_STATE = 64
SSM_CHUNK = 12
SSM_SLAB = 8
CONV_WIDTH = 3
EPS = 1e-6

ROW_ALIGN = 256
PAD = ROW_ALIGN - N_META
ATT_TQ = 1024
ATT_TK = 512
QK_AUG = 2 * HEAD_DIM
LOG2E = math.log2(math.e)
MASK_BIG = 1e30
NEG = -3e38
VMEM_LIMIT = 56 * 1024 * 1024

F32 = jnp.float32
BF16 = jnp.bfloat16


def _sigmoid(x):
    return 1.0 / (1.0 + jnp.exp(-x))


def _gelu_tanh(x):
    c = math.sqrt(2.0 / math.pi)
    return 0.5 * x * (1.0 + jnp.tanh(c * (x + 0.044715 * (x * x * x))))


def _params(sem, limit=VMEM_LIMIT):
    return pltpu.CompilerParams(dimension_semantics=sem, vmem_limit_bytes=limit)


def _resident(shape, index_map):
    return pl.BlockSpec(shape, index_map, pipeline_mode=pl.Buffered(1))


def _win_pack_kernel(wt_ref, o_ref, f_ref, *, o_f, o_u, q_scale):
    n_in, tr = wt_ref.shape
    o_ref[:, 0:D_ATTN] = (wt_ref[0:D_ATTN, :] * q_scale).T.astype(o_ref.dtype)
    o_ref[:, D_ATTN:o_f] = wt_ref[D_ATTN:o_f, :].T.astype(o_ref.dtype)
    o_ref[:, o_f:o_f + (n_in - o_u)] = wt_ref[o_u:n_in, :].T.astype(o_ref.dtype)
    wf = jnp.concatenate([wt_ref[o_f:o_u, :], jnp.zeros((f_ref.shape[1] - (o_u - o_f), tr), F32)], axis=0)
    f_ref[...] = wf.T.astype(f_ref.dtype)


def _win_pack(w_in, o_f, o_u, q_scale, tr=256):
    d, n_in = w_in.shape
    n_out = n_in - (o_u - o_f)
    return pl.pallas_call(
        functools.partial(_win_pack_kernel, o_f=o_f, o_u=o_u, q_scale=q_scale),
        grid=(d // tr,),
        in_specs=[pl.BlockSpec((n_in, tr), lambda i: (0, i))],
        out_specs=[pl.BlockSpec((tr, n_out), lambda i: (i, 0)), pl.BlockSpec((tr, 128), lambda i: (i, 0))],
        out_shape=[jax.ShapeDtypeStruct((d, n_out), BF16), jax.ShapeDtypeStruct((d, 128), BF16)],
        compiler_params=_params(("parallel",)),
        name="win_pack",
    )(w_in.T)


def _seq_block(t, nt):
    return (t // nt) * (nt - 1) + jnp.maximum(t % nt - 1, 0)


def _padded_rows(t, nt, head_ref, x_ref):
    return jnp.where(t % nt == 0, head_ref[...], x_ref[...])


def _split3(x):
    hi = x.astype(BF16)
    r1 = x - hi.astype(F32)
    mid = r1.astype(BF16)
    lo = (r1 - mid.astype(F32)).astype(BF16)
    return hi, mid, lo


def _forget_bias(f, bf_ref, t, nt, carry, qb_ref, kb_ref, row0):
    tm = f.shape[0]
    first = t % nt == 0
    x = f + bf_ref[...]
    logf = jnp.minimum(x, 0.0) - jnp.log1p(jnp.exp(-jnp.abs(x)))
    row = lax.broadcasted_iota(jnp.int32, (tm, 1), 0)
    valid = jnp.logical_or(jnp.logical_not(first), row >= PAD)
    logf = jnp.where(valid, logf, 0.0)
    r = lax.broadcasted_iota(jnp.int32, (tm, tm), 0)
    c = lax.broadcasted_iota(jnp.int32, (tm, tm), 1)
    tri = (r >= c).astype(BF16)
    cs3 = jnp.dot(tri, jnp.concatenate(_split3(logf), axis=1), preferred_element_type=F32)
    w = f.shape[1]
    fcum = cs3[:, 0:w] + cs3[:, w:2 * w] + cs3[:, 2 * w:3 * w] + jnp.where(first, 0.0, carry[0:1, :])
    carry[0:1, :] = fcum[tm - 1:tm, :]

    lane = lax.broadcasted_iota(jnp.int32, (tm, HEAD_DIM), 1)
    one = jnp.ones((tm, HEAD_DIM), F32)
    zero = jnp.zeros((tm, HEAD_DIM), F32)
    for h in range(HEADS):
        hi, mid, lo = (p.astype(F32) for p in _split3(fcum[:, h:h + 1] * LOG2E))
        qb = jnp.where(lane == 0, hi, jnp.where(lane == 1, mid, jnp.where(lane == 2, lo,
             jnp.where(lane < 6, one, zero))))
        khi = jnp.where(valid, hi, MASK_BIG)
        kmid = jnp.where(valid, mid, 0.0)
        klo = jnp.where(valid, lo, 0.0)
        kb = jnp.where(lane < 3, one, jnp.where(lane == 3, -khi, jnp.where(lane == 4, -kmid,
             jnp.where(lane == 5, -klo, zero))))
        qb_ref[h, row0:row0 + tm, :] = qb.astype(qb_ref.dtype)
        kb_ref[h, row0:row0 + tm, :] = kb.astype(kb_ref.dtype)


def _inproj_kernel(head_ref, xa_ref, xb_ref, g_ref, w_ref, wf_ref, bf_ref, z_ref, u_ref, qb_ref, kb_ref,
                   n_scr, carry, *, nt, u_blk, u_lo):
    i = pl.program_id(0)
    j = pl.program_id(1)
    n_i = pl.num_programs(0)
    n_j = pl.num_programs(1)

    def prepare(t):
        x = jnp.concatenate([_padded_rows(2 * t, nt, head_ref, xa_ref),
                             _padded_rows(2 * t + 1, nt, head_ref, xb_ref)], axis=0)
        ms = jnp.mean(x * x, axis=-1, keepdims=True)
        n = (x * lax.rsqrt(ms + EPS) * g_ref[...]).astype(BF16)
        n_scr[t % 2] = n
        f = jnp.dot(n, wf_ref[...], preferred_element_type=F32)
        for blk in range(2):
            _forget_bias(f[blk * ROW_ALIGN:(blk + 1) * ROW_ALIGN], bf_ref, 2 * t + blk, nt, carry,
                         qb_ref, kb_ref, blk * ROW_ALIGN)

    def project():
        acc = jnp.dot(n_scr[i % 2], w_ref[...], preferred_element_type=F32)
        z_ref[...] = acc.astype(z_ref.dtype)
        return acc

    @pl.when(jnp.logical_and(i == 0, j == 0))
    def _():
        carry[...] = jnp.zeros_like(carry)
        prepare(0)

    lookahead = jnp.logical_and(j == n_j - 1, i + 1 < n_i)

    @pl.when(lookahead)
    def _():
        prepare(i + 1)
        project()

    @pl.when(jnp.logical_not(lookahead))
    def _():
        acc = project()

        @pl.when(j == u_blk)
        def _():
            u_ref[...] = acc[:, u_lo:u_lo + u_ref.shape[1]]


def _inproj(head, x2, g, w, wf, bf, rows, nt, u_off, u_width, tn=2048):
    d = x2.shape[1]
    tm = 2 * ROW_ALIGN
    n_out = w.shape[1]
    n_i, n_j = rows // tm, n_out // tn
    u_blk, u_lo = divmod(u_off, tn)
    assert u_lo + u_width <= tn and rows % tm == 0 and u_blk < n_j - 1

    def tile(i, j):
        return jnp.minimum(i + (j == n_j - 1).astype(jnp.int32), n_i - 1)

    bias_spec = pl.BlockSpec((HEADS, tm, HEAD_DIM), lambda i, j: (0, tile(i, j), 0))
    bias_shape = jax.ShapeDtypeStruct((HEADS, rows, HEAD_DIM), BF16)
    return pl.pallas_call(
        functools.partial(_inproj_kernel, nt=nt, u_blk=u_blk, u_lo=u_lo),
        grid=(n_i, n_j),
        in_specs=[
            _resident((ROW_ALIGN, d), lambda i, j: (0, 0)),
            pl.BlockSpec((ROW_ALIGN, d), lambda i, j: (_seq_block(2 * tile(i, j), nt), 0)),
            pl.BlockSpec((ROW_ALIGN, d), lambda i, j: (_seq_block(2 * tile(i, j) + 1, nt), 0)),
            pl.BlockSpec((1, d), lambda i, j: (0, 0)),
            pl.BlockSpec((d, tn), lambda i, j: (0, j)),
            pl.BlockSpec((d, 128), lambda i, j: (0, 0)),
            pl.BlockSpec((1, 128), lambda i, j: (0, 0)),
        ],
        out_specs=[
            pl.BlockSpec((tm, tn), lambda i, j: (i, j)),
            pl.BlockSpec((tm, u_width), lambda i, j: (i, 0)),
            bias_spec, bias_spec,
        ],
        out_shape=[
            jax.ShapeDtypeStruct((rows, n_out), BF16),
            jax.ShapeDtypeStruct((rows, u_width), F32),
            bias_shape, bias_shape,
        ],
        scratch_shapes=[pltpu.VMEM((2, tm, d), BF16), pltpu.VMEM((8, 128), F32)],
        compiler_params=_params(("arbitrary", "arbitrary")),
        name="inproj",
    )(head, x2, x2, g, w, wf, bf)


def _flash_kernel(q_ref, k_ref, v_ref, qb_ref, kb_ref, o_ref,
                  qa_ref, ka_ref, qt_ref, vt_ref, sa_ref, sb_ref, mxa_ref, mxb_ref, m_ref, acc_ref):
    lp = v_ref.shape[0]
    tq, tk = ATT_TQ, ATT_TK
    n_main = (lp - ROW_ALIGN) // tq
    n_chunks = (lp - ROW_ALIGN) // tk
    va = vt_ref.shape[1]

    qa_ref[:, 0:HEAD_DIM] = q_ref[...]
    qa_ref[:, HEAD_DIM:QK_AUG] = qb_ref[0]
    ka_ref[:, 0:HEAD_DIM] = k_ref[...]
    ka_ref[:, HEAD_DIM:QK_AUG] = kb_ref[0]

    ones = jnp.ones((va - HEAD_DIM, tk), vt_ref.dtype)
    vt_ref[0, 0:HEAD_DIM, 0:ROW_ALIGN] = v_ref[0:ROW_ALIGN, :].T
    vt_ref[0, HEAD_DIM:va, :] = ones

    def fill(c, _):
        vt_ref[1 + c, 0:HEAD_DIM, :] = v_ref[pl.ds(pl.multiple_of(ROW_ALIGN + c * tk, ROW_ALIGN), tk), :].T
        vt_ref[1 + c, HEAD_DIM:va, :] = ones
        return 0

    lax.fori_loop(0, n_chunks, fill, 0)

    buf_a = (sa_ref, mxa_ref)
    buf_b = (sb_ref, mxb_ref)

    def load_queries(q_off, n):
        qt_ref[:, 0:n] = qa_ref[pl.ds(q_off, n), :].T

    def scores(buf, nq, k_off, nk, col0):
        s_ref, mx_ref = buf
        kc = ka_ref[pl.ds(k_off, nk), :]
        s = jnp.dot(kc, qt_ref[:, col0:col0 + nq], preferred_element_type=F32)
        s_ref[0:nk, col0:col0 + nq] = s
        mx_ref[0:1, col0:col0 + nq] = jnp.max(s, axis=0, keepdims=True)

    def absorb(buf, slot, nk, lo, hi, mask_shift=None, lane0=0):
        s_ref, mx_ref = buf
        s = s_ref[0:nk, lo:hi]
        if mask_shift is None:
            smax = mx_ref[0:1, lo:hi]
        else:
            kr = lax.broadcasted_iota(jnp.int32, s.shape, 0) + mask_shift
            qc = lax.broadcasted_iota(jnp.int32, s.shape, 1) + lo
            s = jnp.where(kr > qc, NEG, s)
            smax = jnp.max(s, axis=0, keepdims=True)
        m = m_ref[0:1, lo:hi]
        m_new = jnp.maximum(m, smax)
        alpha = jnp.exp2(m - m_new)
        p = jnp.exp2(s - m_new).astype(BF16)
        m_ref[0:1, lo:hi] = m_new
        pv = jnp.dot(vt_ref[slot, :, lane0:lane0 + nk], p, preferred_element_type=F32)
        acc_ref[:, lo:hi] = alpha * acc_ref[:, lo:hi] + pv

    def reset(n):
        m_ref[0:1, 0:n] = jnp.full((1, n), NEG, F32)
        acc_ref[:, 0:n] = jnp.zeros((va, n), F32)

    def finish(q_off, n):
        out = acc_ref[0:HEAD_DIM, 0:n] / acc_ref[HEAD_DIM:HEAD_DIM + 1, 0:n]
        o_ref[pl.ds(q_off, n), :] = out.T.astype(o_ref.dtype)

    reset(ROW_ALIGN)
    load_queries(0, ROW_ALIGN)
    scores(buf_a, ROW_ALIGN, 0, ROW_ALIGN, 0)
    absorb(buf_a, 0, ROW_ALIGN, 0, ROW_ALIGN, mask_shift=0)
    finish(0, ROW_ALIGN)

    per_tile = tq // tk

    def tile(j, _):
        q_off = pl.multiple_of(ROW_ALIGN + j * tq, ROW_ALIGN)
        reset(tq)
        load_queries(q_off, tq)
        scores(buf_a, tq, PAD, N_META, 0)
        scores(buf_b, tq, ROW_ALIGN, tk, 0)
        absorb(buf_a, 0, N_META, 0, tq, lane0=PAD)

        def pair(p):
            kb = pl.multiple_of(ROW_ALIGN + p * (2 * tk), ROW_ALIGN)
            scores(buf_a, tq, kb + tk, tk, 0)
            absorb(buf_b, 1 + 2 * p, tk, 0, tq)
            scores(buf_b, tq, kb + 2 * tk, tk, 0)
            absorb(buf_a, 2 + 2 * p, tk, 0, tq)

        def two_pairs(pp, _):
            pair(2 * pp)
            pair(2 * pp + 1)
            return 0

        n_pairs = j * (per_tile // 2)
        lax.fori_loop(0, n_pairs // 2, two_pairs, 0)

        @pl.when(n_pairs % 2 == 1)
        def _():
            pair(n_pairs - 1)
        scores(buf_a, tq - tk, q_off + tk, tk, tk)
        absorb(buf_b, 1 + j * per_tile, tk, 0, tq, mask_shift=0)
        absorb(buf_a, 2 + j * per_tile, tk, tk, tq, mask_shift=tk)
        finish(q_off, tq)
        return 0

    lax.fori_loop(0, n_main, tile, 0)


def _flash(z, qb, kb, bsz, lp):
    assert (lp - ROW_ALIGN) % ATT_TQ == 0 and ATT_TQ == 2 * ATT_TK
    bias_spec = pl.BlockSpec((1, lp, HEAD_DIM), lambda b, h: (h, b, 0))
    va = HEAD_DIM + 16
    return pl.pallas_call(
        _flash_kernel,
        grid=(bsz, HEADS),
        in_specs=[
            pl.BlockSpec((lp, HEAD_DIM), lambda b, h: (b, h)),
            pl.BlockSpec((lp, HEAD_DIM), lambda b, h: (b, HEADS + h)),
            pl.BlockSpec((lp, HEAD_DIM), lambda b, h: (b, 2 * HEADS + h)),
            bias_spec, bias_spec,
        ],
        out_specs=pl.BlockSpec((lp, HEAD_DIM), lambda b, h: (b, h)),
        out_shape=jax.ShapeDtypeStruct((bsz * lp, D_ATTN), BF16),
        scratch_shapes=[
            pltpu.VMEM((lp, QK_AUG), BF16),
            pltpu.VMEM((lp, QK_AUG), BF16),
            pltpu.VMEM((QK_AUG, ATT_TQ), BF16),
            pltpu.VMEM((1 + (lp - ROW_ALIGN) // ATT_TK, va, ATT_TK), BF16),
            pltpu.VMEM((ATT_TK, ATT_TQ + 128), F32),
            pltpu.VMEM((ATT_TK, ATT_TQ + 128), F32),
            pltpu.VMEM((8, ATT_TQ), F32),
            pltpu.VMEM((8, ATT_TQ), F32),
            pltpu.VMEM((8, ATT_TQ), F32),
            pltpu.VMEM((va, ATT_TQ + 128), F32),
        ],
        compiler_params=_params(("parallel", "parallel")),
        name="flash",
    )(z, z, z, qb, kb)


def _tile_lanes(x, reps):
    return jnp.concatenate([x] * reps, axis=1)


def _ssm_kernel(u_ref, lr_ref, li_ref, dt_ref, lrl_ref, lil_ref, dtl_ref, btr_ref, bti_ref, cr_ref, ci_ref,
                y_ref, lhs_scr, toep_scr, inj_scr, out_scr, s_scr, hp_scr):
    t = SSM_CHUNK
    lanes = SSM_SLAB * SSM_GROUP
    ns = SSM_SLAB * SSM_STATE
    nc = u_ref.shape[0] // t

    @pl.when(pl.program_id(1) == 0)
    def _build_weights():
        lr = lr_ref[0]
        li = li_ref[0]
        dt = jnp.exp(dt_ref[0])
        btr, bti = btr_ref[0], bti_ref[0]
        cr, ci = cr_ref[0], ci_ref[0]

        mag = jnp.exp(lr * dt)
        a_re = mag * jnp.cos(li * dt)
        a_im = mag * jnp.sin(li * dt)
        powers = [(jnp.ones_like(a_re), jnp.zeros_like(a_re))]
        for _ in range(t):
            pr, pi = powers[-1]
            powers.append((pr * a_re - pi * a_im, pr * a_im + pi * a_re))
        den = lr * lr + li * li
        nr = a_re - 1.0
        z_re = (nr * lr + a_im * li) / den
        z_im = (a_im * lr - nr * li) / den
        row_g = lax.broadcasted_iota(jnp.int32, (lanes, ns), 0) // SSM_GROUP
        col_g = lax.broadcasted_iota(jnp.int32, (lanes, ns), 1) // SSM_STATE
        same = row_g == col_g

        def spread(x):
            return jnp.where(same, _tile_lanes(x, SSM_SLAB), 0.0)

        caz_re, caz_im = [], []
        for d in range(t):
            pr, pi = powers[d]
            azr = pr * z_re - pi * z_im
            azi = pr * z_im + pi * z_re
            caz_re.append(cr * azr - ci * azi)
            caz_im.append(cr * azi + ci * azr)
            i = t - 1 - d
            sr = btr * azr - bti * azi
            si = btr * azi + bti * azr
            inj_scr[i * lanes:(i + 1) * lanes, 0:ns] = spread(sr).astype(BF16)
            inj_scr[i * lanes:(i + 1) * lanes, ns:2 * ns] = spread(si).astype(BF16)
            qr, qi = powers[d + 1]
            er = cr * qr - ci * qi
            ei = cr * qi + ci * qr
            out_scr[d * lanes:(d + 1) * lanes, 0:ns] = spread(er).astype(BF16)
            out_scr[d * lanes:(d + 1) * lanes, ns:2 * ns] = spread(-ei).astype(BF16)
        nt_dims = (((1,), (1,)), ((), ()))
        hp = lax.Precision.HIGHEST
        r0 = (lax.dot_general(btr, jnp.concatenate(caz_re, axis=0), nt_dims, precision=hp, preferred_element_type=F32)
              - lax.dot_general(bti, jnp.concatenate(caz_im, axis=0), nt_dims, precision=hp, preferred_element_type=F32))
        rg = lax.broadcasted_iota(jnp.int32, r0.shape, 0) // SSM_GROUP
        cg = (lax.broadcasted_iota(jnp.int32, r0.shape, 1) % lanes) // SSM_GROUP
        r0 = jnp.where(rg == cg, r0, 0.0).astype(BF16)
        toep_scr[...] = jnp.zeros_like(toep_scr)
        for i in range(t):
            toep_scr[i * lanes:(i + 1) * lanes, i * lanes:t * lanes] = r0[:, 0:(t - i) * lanes]

    for i in range(t):
        lhs_scr[:, i * lanes:(i + 1) * lanes] = u_ref[pl.ds(i, nc, stride=t), :].astype(BF16)

    s_scr[...] = jnp.dot(lhs_scr[...], inj_scr[...], preferred_element_type=F32)

    dtl = jnp.exp(dtl_ref[0])
    mag = jnp.exp(float(t) * (lrl_ref[0] * dtl))
    ang = float(t) * (lil_ref[0] * dtl)
    ar = mag * jnp.cos(ang)
    ai = mag * jnp.sin(ang)

    def step(n, carry):
        hr, hi = carry
        hp_scr[pl.ds(n, 1), 0:ns] = hr
        hp_scr[pl.ds(n, 1), ns:2 * ns] = hi
        sr = s_scr[pl.ds(n, 1), 0:ns]
        si = s_scr[pl.ds(n, 1), ns:2 * ns]
        return ar * hr - ai * hi + sr, ar * hi + ai * hr + si

    zero = jnp.zeros((1, ns), F32)
    lax.fori_loop(0, nc, step, (zero, zero), unroll=8)

    hprev = hp_scr[...].astype(BF16)
    nt_dims = (((1,), (1,)), ((), ()))
    pair = 2 * lanes
    for jj in range(t // 2):
        kdim = (jj + 1) * pair
        yj = (jnp.dot(lhs_scr[:, 0:kdim], toep_scr[0:kdim, jj * pair:(jj + 1) * pair], preferred_element_type=F32)
              + lax.dot_general(hprev, out_scr[jj * pair:(jj + 1) * pair, :], nt_dims, preferred_element_type=F32))
        y_ref[pl.ds(2 * jj, nc, stride=t), :] = yj[:, 0:lanes]
        y_ref[pl.ds(2 * jj + 1, nc, stride=t), :] = yj[:, lanes:pair]


def _ssm(u, lam_re, lam_im, log_dt, b_re, b_im, c_re, c_im, bsz, lp):
    rows, ds = u.shape
    g, p = lam_re.shape
    t, c = SSM_CHUNK, SSM_GROUP
    lanes = SSM_SLAB * c
    ns = SSM_SLAB * p
    nslab = g // SSM_SLAB
    nc = lp // t
    assert g % SSM_SLAB == 0 and lp % t == 0 and lanes == 128
    rep = lambda x: jnp.repeat(x, c, axis=0).reshape(nslab, lanes, -1)
    lane = lambda x: x.reshape(nslab, 1, ns)
    args = (
        u,
        rep(lam_re), rep(lam_im), rep(log_dt.reshape(g, 1)),
        lane(lam_re), lane(lam_im), lane(jnp.repeat(log_dt, p)),
        jnp.swapaxes(b_re, 1, 2).reshape(nslab, lanes, p), jnp.swapaxes(b_im, 1, 2).reshape(nslab, lanes, p),
        c_re.reshape(nslab, lanes, p), c_im.reshape(nslab, lanes, p),
    )
    rows_spec = pl.BlockSpec((1, lanes, p), lambda s, b: (s, 0, 0))
    lane_spec = pl.BlockSpec((1, 1, ns), lambda s, b: (s, 0, 0))
    y = pl.pallas_call(
        _ssm_kernel,
        grid=(nslab, bsz),
        in_specs=[
            pl.BlockSpec((lp, lanes), lambda s, b: (b, s)),
            rows_spec, rows_spec, pl.BlockSpec((1, lanes, 1), lambda s, b: (s, 0, 0)),
            lane_spec, lane_spec, lane_spec,
            rows_spec, rows_spec, rows_spec, rows_spec,
        ],
        out_specs=pl.BlockSpec((lp, lanes), lambda s, b: (b, s)),
        out_shape=jax.ShapeDtypeStruct((rows, ds), F32),
        scratch_shapes=[
            pltpu.VMEM((nc, t * lanes), BF16),
            pltpu.VMEM((t * lanes, t * lanes), BF16),
            pltpu.VMEM((t * lanes, 2 * ns), BF16),
            pltpu.VMEM((t * lanes, 2 * ns), BF16),
            pltpu.VMEM((nc, 2 * ns), F32),
            pltpu.VMEM((nc, 2 * ns), F32),
        ],
        compiler_params=_params(("parallel", "arbitrary")),
        name="ssm",
    )(*args)
    return y


def _mix_kernel(y_ref, u_ref, a_ref, ga_ref, gb_ref, head_ref, x_ref, dsk_ref, gffn_ref,
                wglu_ref, wao_ref, wout_ref, h_ref, n_ref, *, nt):
    d = x_ref.shape[1]
    y = y_ref[...].astype(F32) + dsk_ref[...] * u_ref[...].astype(F32)
    gy = _gelu_tanh(y).astype(BF16)
    yab = jnp.dot(gy, wglu_ref[...], preferred_element_type=F32)
    ssm_out = yab[:, 0:d] * _sigmoid(yab[:, d:2 * d])
    ao = jnp.dot(a_ref[...], wao_ref[...], preferred_element_type=F32)
    merged = (_sigmoid(ga_ref[...].astype(F32)) * ssm_out
              + _sigmoid(gb_ref[...].astype(F32)) * ao).astype(BF16)
    x = _padded_rows(pl.program_id(0), nt, head_ref, x_ref)
    h = x + jnp.dot(merged, wout_ref[...], preferred_element_type=F32)
    h_ref[...] = h
    ms = jnp.mean(h * h, axis=-1, keepdims=True)
    n_ref[...] = (h * lax.rsqrt(ms + EPS) * gffn_ref[...]).astype(n_ref.dtype)


def _mix(y, z, attn, head, x2, nt, d_skip, g_ffn, w_glu, w_ao, w_out):
    rows, ds = y.shape
    d = x2.shape[1]
    tm = ROW_ALIGN
    u_col = 3 * D_ATTN // ds
    ga_col = (3 * D_ATTN + ds) // d
    assert (3 * D_ATTN) % ds == 0 and (3 * D_ATTN + ds) % d == 0
    return pl.pallas_call(
        functools.partial(_mix_kernel, nt=nt),
        grid=(rows // tm,),
        in_specs=[
            pl.BlockSpec((tm, ds), lambda i: (i, 0)),
            pl.BlockSpec((tm, ds), lambda i: (i, u_col)),
            pl.BlockSpec((tm, D_ATTN), lambda i: (i, 0)),
            pl.BlockSpec((tm, d), lambda i: (i, ga_col)),
            pl.BlockSpec((tm, d), lambda i: (i, ga_col + 1)),
            _resident((tm, d), lambda i: (0, 0)),
            pl.BlockSpec((tm, d), lambda i: (_seq_block(i, nt), 0)),
            _resident((1, ds), lambda i: (0, 0)),
            _resident((1, d), lambda i: (0, 0)),
            _resident(w_glu.shape, lambda i: (0, 0)),
            _resident(w_ao.shape, lambda i: (0, 0)),
            _resident(w_out.shape, lambda i: (0, 0)),
        ],
        out_specs=[
            pl.BlockSpec((tm, d), lambda i: (i, 0)),
            pl.BlockSpec((tm, d), lambda i: (i, 0)),
        ],
        out_shape=[
            jax.ShapeDtypeStruct((rows, d), F32),
            jax.ShapeDtypeStruct((rows, d), BF16),
        ],
        compiler_params=_params(("parallel",)),
        name="mix_out",
    )(y, z, attn, z, z, head, x2, d_skip, g_ffn, w_glu, w_ao, w_out)


def _ffn_up_kernel(n_ref, wg_ref, wu_ref, cw_ref, cb_ref, a_ref, tail, wg_scr, wu_scr):
    tm = n_ref.shape[0]

    @pl.when(pl.program_id(1) == 0)
    def _():
        tail[...] = jnp.zeros_like(tail)
        wg_scr[...] = wg_ref[...].astype(wg_scr.dtype)
        wu_scr[...] = wu_ref[...].astype(wu_scr.dtype)

    n = n_ref[...]
    g = jnp.dot(n, wg_scr[...], preferred_element_type=F32)
    u = jnp.dot(n, wu_scr[...], preferred_element_type=F32)
    row = lax.broadcasted_iota(jnp.int32, g.shape, 0)
    prev1 = tail[CONV_WIDTH - 2:CONV_WIDTH - 1, :]
    prev2 = tail[CONV_WIDTH - 3:CONV_WIDTH - 2, :]
    g1 = jnp.where(row == 0, prev1, pltpu.roll(g, 1, 0))
    g2 = jnp.where(row == 0, prev2, jnp.where(row == 1, prev1, pltpu.roll(g, 2, 0)))
    gc = cb_ref[...] + cw_ref[0:1, :] * g2 + cw_ref[1:2, :] * g1 + cw_ref[2:3, :] * g
    tail[0:CONV_WIDTH - 1, :] = g[tm - (CONV_WIDTH - 1):tm, :]
    a_ref[...] = (gc * _sigmoid(gc) * u).astype(a_ref.dtype)


def _largest_tile(n, candidates):
    return next(c for c in candidates if n % c == 0)


def _ffn_up(n2, w_up, conv_w, conv_b, tn=512):
    rows, d = n2.shape
    dff = conv_w.shape[1]
    nj = dff // tn
    tm = _largest_tile(rows, (1536, 1024, 768, 512, 256))
    assert dff % tn == 0
    return pl.pallas_call(
        _ffn_up_kernel,
        grid=(nj, rows // tm),
        in_specs=[
            pl.BlockSpec((tm, d), lambda j, i: (i, 0)),
            pl.BlockSpec((d, tn), lambda j, i: (0, j)),
            pl.BlockSpec((d, tn), lambda j, i: (0, nj + j)),
            pl.BlockSpec((CONV_WIDTH, tn), lambda j, i: (0, j)),
            pl.BlockSpec((1, tn), lambda j, i: (0, j)),
        ],
        out_specs=pl.BlockSpec((tm, tn), lambda j, i: (i, j)),
        out_shape=jax.ShapeDtypeStruct((rows, dff), BF16),
        scratch_shapes=[pltpu.VMEM((8, tn), F32), pltpu.VMEM((d, tn), BF16), pltpu.VMEM((d, tn), BF16)],
        compiler_params=_params(("parallel", "arbitrary")),
        name="ffn_up",
    )(n2, w_up, w_up, conv_w, conv_b)


def _ffn_down_kernel(a_ref, w_ref, h_ref, g_ref, o_ref):
    h = h_ref[...] + jnp.dot(a_ref[...], w_ref[...], preferred_element_type=F32)
    ms = jnp.mean(h * h, axis=-1, keepdims=True)
    o_ref[...] = (h * lax.rsqrt(ms + EPS) * g_ref[...]).astype(o_ref.dtype)


def _ffn_down(act, w_down, h1, g_final, bsz, lp, tm=ROW_ALIGN):
    dff, d = w_down.shape
    nt_in = lp // tm
    nt_out = nt_in - ROW_ALIGN // tm
    skip = ROW_ALIGN // tm
    return pl.pallas_call(
        _ffn_down_kernel,
        grid=(bsz, nt_out),
        in_specs=[
            pl.BlockSpec((tm, dff), lambda b, t: (b * nt_in + skip + t, 0)),
            _resident((dff, d), lambda b, t: (0, 0)),
            pl.BlockSpec((tm, d), lambda b, t: (b * nt_in + skip + t, 0)),
            _resident((1, d), lambda b, t: (0, 0)),
        ],
        out_specs=pl.BlockSpec((tm, d), lambda b, t: (b * nt_out + t, 0)),
        out_shape=jax.ShapeDtypeStruct((bsz * nt_out * tm, d), F32),
        compiler_params=_params(("parallel", "parallel")),
        name="ffn_down",
    )(act, w_down, h1, g_final)


def _layer(head, x2, bsz, lp, g_mix, w_in, b_f, lam_re, lam_im, log_dt, b_re, b_im, c_re, c_im, d_skip,
           w_glu, w_attn_o, w_out, g_ffn, w_up, conv_w, conv_b, w_down):
    d = x2.shape[1]
    ds = d_skip.shape[0]
    nt = lp // ROW_ALIGN
    o_q, o_k, o_v, o_f, o_u, o_ga, o_gb = (0, D_ATTN, 2 * D_ATTN, 3 * D_ATTN, 3 * D_ATTN + HEADS,
                                            3 * D_ATTN + HEADS + ds, 3 * D_ATTN + HEADS + ds + d)
    w_main, w_f = _win_pack(w_in, o_f, o_u, HEAD_DIM ** -0.5 * LOG2E)
    bf = jnp.pad(b_f.astype(F32), (0, 128 - HEADS)).reshape(1, 128)

    z, u, qb, kb = _inproj(head, x2, g_mix.reshape(1, d), w_main, w_f, bf, bsz * lp, nt, 3 * D_ATTN, ds)
    attn = _flash(z, qb, kb, bsz, lp)
    y = _ssm(u, lam_re, lam_im, log_dt, b_re, b_im, c_re, c_im, bsz, lp)

    h1, n2 = _mix(y, z, attn, head, x2, nt, d_skip.reshape(1, ds).astype(F32), g_ffn.reshape(1, d).astype(F32),
                  w_glu.astype(BF16), w_attn_o.astype(BF16), w_out.astype(BF16))
    act = _ffn_up(n2, w_up.astype(F32), conv_w.astype(F32), conv_b.reshape(1, -1).astype(F32))
    return act, h1


def kernel(x, meta, g_mix, w_in, b_f, lam_re, lam_im, log_dt, b_re, b_im, c_re, c_im, d_skip,
           w_glu, w_attn_o, w_out, g_ffn, w_up, conv_w, conv_b, w_down, g_final):
    bsz, seq, d = x.shape
    depth = g_mix.shape[0]
    assert depth == 1 and meta.shape[0] == N_META and seq % ROW_ALIGN == 0
    lp = seq + ROW_ALIGN
    assert (bsz * lp) % 512 == 0
    head = jnp.concatenate([jnp.zeros((PAD, d), x.dtype), meta.astype(x.dtype)], axis=0)
    act, h1 = _layer(head, x.reshape(bsz * seq, d), bsz, lp, g_mix[0], w_in[0], b_f[0], lam_re[0], lam_im[0], log_dt[0],
                     b_re[0], b_im[0], c_re[0], c_im[0], d_skip[0], w_glu[0], w_attn_o[0], w_out[0],
                     g_ffn[0], w_up[0], conv_w[0], conv_b[0], w_down[0])
    out = _ffn_down(act, w_down[0].astype(BF16), h1, g_final.reshape(1, d).astype(F32), bsz, lp)
    return out.reshape(bsz, seq, d)
```

```python
import functools
import math

import jax
import jax.numpy as jnp
from jax import lax
from jax.experimental import pallas as pl
from jax.experimental.pallas import tpu as pltpu

N_META = 16
HEADS = 8
HEAD_DIM = 128
D_ATTN = HEADS * HEAD_DIM
SSM_GROUP = 16
SSM_STATE = 64
SSM_CHUNK = 12
SSM_SLAB = 8
CONV_WIDTH = 3
EPS = 1e-6

ROW_ALIGN = 256
PAD = ROW_ALIGN - N_META
ATT_TQ = 1024
ATT_TK = 512
QK_AUG = 2 * HEAD_DIM
LOG2E = math.log2(math.e)
MASK_BIG = 1e30
NEG = -3e38
VMEM_LIMIT = 56 * 1024 * 1024

F32 = jnp.float32
BF16 = jnp.bfloat16


def _sigmoid(x):
    return 1.0 / (1.0 + jnp.exp(-x))


def _gelu_tanh(x):
    c = math.sqrt(2.0 / math.pi)
    return 0.5 * x * (1.0 + jnp.tanh(c * (x + 0.044715 * (x * x * x))))


def _params(sem, limit=VMEM_LIMIT):
    return pltpu.CompilerParams(dimension_semantics=sem, vmem_limit_bytes=limit)


def _resident(shape, index_map):
    return pl.BlockSpec(shape, index_map, pipeline_mode=pl.Buffered(1))


def _win_pack_kernel(wt_ref, o_ref, f_ref, *, o_f, o_u, q_scale):
    n_in, tr = wt_ref.shape
    o_ref[:, 0:D_ATTN] = (wt_ref[0:D_ATTN, :] * q_scale).T.astype(o_ref.dtype)
    o_ref[:, D_ATTN:o_f] = wt_ref[D_ATTN:o_f, :].T.astype(o_ref.dtype)
    o_ref[:, o_f:o_f + (n_in - o_u)] = wt_ref[o_u:n_in, :].T.astype(o_ref.dtype)
    wf = jnp.concatenate([wt_ref[o_f:o_u, :], jnp.zeros((f_ref.shape[1] - (o_u - o_f), tr), F32)], axis=0)
    f_ref[...] = wf.T.astype(f_ref.dtype)


def _win_pack(w_in, o_f, o_u, q_scale, tr=256):
    d, n_in = w_in.shape
    n_out = n_in - (o_u - o_f)
    return pl.pallas_call(
        functools.partial(_win_pack_kernel, o_f=o_f, o_u=o_u, q_scale=q_scale),
        grid=(d // tr,),
        in_specs=[pl.BlockSpec((n_in, tr), lambda i: (0, i))],
        out_specs=[pl.BlockSpec((tr, n_out), lambda i: (i, 0)), pl.BlockSpec((tr, 128), lambda i: (i, 0))],
        out_shape=[jax.ShapeDtypeStruct((d, n_out), BF16), jax.ShapeDtypeStruct((d, 128), BF16)],
        compiler_params=_params(("parallel",)),
        name="win_pack",
    )(w_in.T)


def _seq_block(t, nt):
    return (t // nt) * (nt - 1) + jnp.maximum(t % nt - 1, 0)


def _padded_rows(t, nt, head_ref, x_ref):
    return jnp.where(t % nt == 0, head_ref[...], x_ref[...])


def _split3(x):
    hi = x.astype(BF16)
    r1 = x - hi.astype(F32)
    mid = r1.astype(BF16)
    lo = (r1 - mid.astype(F32)).astype(BF16)
    return hi, mid, lo


def _forget_bias(f, bf_ref, t, nt, carry, qb_ref, kb_ref, row0):
    tm = f.shape[0]
    first = t % nt == 0
    x = f + bf_ref[...]
    logf = jnp.minimum(x, 0.0) - jnp.log1p(jnp.exp(-jnp.abs(x)))
    row = lax.broadcasted_iota(jnp.int32, (tm, 1), 0)
    valid = jnp.logical_or(jnp.logical_not(first), row >= PAD)
    logf = jnp.where(valid, logf, 0.0)
    r = lax.broadcasted_iota(jnp.int32, (tm, tm), 0)
    c = lax.broadcasted_iota(jnp.int32, (tm, tm), 1)
    tri = (r >= c).astype(BF16)
    cs3 = jnp.dot(tri, jnp.concatenate(_split3(logf), axis=1), preferred_element_type=F32)
    w = f.shape[1]
    fcum = cs3[:, 0:w] + cs3[:, w:2 * w] + cs3[:, 2 * w:3 * w] + jnp.where(first, 0.0, carry[0:1, :])
    carry[0:1, :] = fcum[tm - 1:tm, :]

    lane = lax.broadcasted_iota(jnp.int32, (tm, HEAD_DIM), 1)
    one = jnp.ones((tm, HEAD_DIM), F32)
    zero = jnp.zeros((tm, HEAD_DIM), F32)
    for h in range(HEADS):
        hi, mid, lo = (p.astype(F32) for p in _split3(fcum[:, h:h + 1] * LOG2E))
        qb = jnp.where(lane == 0, hi, jnp.where(lane == 1, mid, jnp.where(lane == 2, lo,
             jnp.where(lane < 6, one, zero))))
        khi = jnp.where(valid, hi, MASK_BIG)
        kmid = jnp.where(valid, mid, 0.0)
        klo = jnp.where(valid, lo, 0.0)
        kb = jnp.where(lane < 3, one, jnp.where(lane == 3, -khi, jnp.where(lane == 4, -kmid,
             jnp.where(lane == 5, -klo, zero))))
        qb_ref[h, row0:row0 + tm, :] = qb.astype(qb_ref.dtype)
        kb_ref[h, row0:row0 + tm, :] = kb.astype(kb_ref.dtype)


def _inproj_kernel(head_ref, xa_ref, xb_ref, g_ref, w_ref, wf_ref, bf_ref, z_ref, u_ref, qb_ref, kb_ref,
                   n_scr, carry, *, nt, u_blk, u_lo):
    i = pl.program_id(0)
    j = pl.program_id(1)
    n_i = pl.num_programs(0)
    n_j = pl.num_programs(1)

    def prepare(t):
        x = jnp.concatenate([_padded_rows(2 * t, nt, head_ref, xa_ref),
                             _padded_rows(2 * t + 1, nt, head_ref, xb_ref)], axis=0)
        ms = jnp.mean(x * x, axis=-1, keepdims=True)
        n = (x * lax.rsqrt(ms + EPS) * g_ref[...]).astype(BF16)
        n_scr[t % 2] = n
        f = jnp.dot(n, wf_ref[...], preferred_element_type=F32)
        for blk in range(2):
            _forget_bias(f[blk * ROW_ALIGN:(blk + 1) * ROW_ALIGN], bf_ref, 2 * t + blk, nt, carry,
                         qb_ref, kb_ref, blk * ROW_ALIGN)

    def project():
        acc = jnp.dot(n_scr[i % 2], w_ref[...], preferred_element_type=F32)
        z_ref[...] = acc.astype(z_ref.dtype)
        return acc

    @pl.when(jnp.logical_and(i == 0, j == 0))
    def _():
        carry[...] = jnp.zeros_like(carry)
        prepare(0)

    lookahead = jnp.logical_and(j == n_j - 1, i + 1 < n_i)

    @pl.when(lookahead)
    def _():
        prepare(i + 1)
        project()

    @pl.when(jnp.logical_not(lookahead))
    def _():
        acc = project()

        @pl.when(j == u_blk)
        def _():
            u_ref[...] = acc[:, u_lo:u_lo + u_ref.shape[1]]


def _inproj(head, x2, g, w, wf, bf, rows, nt, u_off, u_width, tn=2048):
    d = x2.shape[1]
    tm = 2 * ROW_ALIGN
    n_out = w.shape[1]
    n_i, n_j = rows // tm, n_out // tn
    u_blk, u_lo = divmod(u_off, tn)
    assert u_lo + u_width <= tn and rows % tm == 0 and u_blk < n_j - 1

    def tile(i, j):
        return jnp.minimum(i + (j == n_j - 1).astype(jnp.int32), n_i - 1)

    bias_spec = pl.BlockSpec((HEADS, tm, HEAD_DIM), lambda i, j: (0, tile(i, j), 0))
    bias_shape = jax.ShapeDtypeStruct((HEADS, rows, HEAD_DIM), BF16)
    return pl.pallas_call(
        functools.partial(_inproj_kernel, nt=nt, u_blk=u_blk, u_lo=u_lo),
        grid=(n_i, n_j),
        in_specs=[
            _resident((ROW_ALIGN, d), lambda i, j: (0, 0)),
            pl.BlockSpec((ROW_ALIGN, d), lambda i, j: (_seq_block(2 * tile(i, j), nt), 0)),
            pl.BlockSpec((ROW_ALIGN, d), lambda i, j: (_seq_block(2 * tile(i, j) + 1, nt), 0)),
            pl.BlockSpec((1, d), lambda i, j: (0, 0)),
            pl.BlockSpec((d, tn), lambda i, j: (0, j)),
            pl.BlockSpec((d, 128), lambda i, j: (0, 0)),
            pl.BlockSpec((1, 128), lambda i, j: (0, 0)),
        ],
        out_specs=[
            pl.BlockSpec((tm, tn), lambda i, j: (i, j)),
            pl.BlockSpec((tm, u_width), lambda i, j: (i, 0)),
            bias_spec, bias_spec,
        ],
        out_shape=[
            jax.ShapeDtypeStruct((rows, n_out), BF16),
            jax.ShapeDtypeStruct((rows, u_width), F32),
            bias_shape, bias_shape,
        ],
        scratch_shapes=[pltpu.VMEM((2, tm, d), BF16), pltpu.VMEM((8, 128), F32)],
        compiler_params=_params(("arbitrary", "arbitrary")),
        name="inproj",
    )(head, x2, x2, g, w, wf, bf)


def _flash_kernel(q_ref, k_ref, v_ref, qb_ref, kb_ref, o_ref,
                  qa_ref, ka_ref, qt_ref, vt_ref, sa_ref, sb_ref, mxa_ref, mxb_ref, m_ref, acc_ref):
    lp = v_ref.shape[0]
    tq, tk = ATT_TQ, ATT_TK
    n_main = (lp - ROW_ALIGN) // tq
    n_chunks = (lp - ROW_ALIGN) // tk
    va = vt_ref.shape[1]

    qa_ref[:, 0:HEAD_DIM] = q_ref[...]
    qa_ref[:, HEAD_DIM:QK_AUG] = qb_ref[0]
    ka_ref[:, 0:HEAD_DIM] = k_ref[...]
    ka_ref[:, HEAD_DIM:QK_AUG] = kb_ref[0]

    ones = jnp.ones((va - HEAD_DIM, tk), vt_ref.dtype)
    vt_ref[0, 0:HEAD_DIM, 0:ROW_ALIGN] = v_ref[0:ROW_ALIGN, :].T
    vt_ref[0, HEAD_DIM:va, :] = ones

    def fill(c, _):
        vt_ref[1 + c, 0:HEAD_DIM, :] = v_ref[pl.ds(pl.multiple_of(ROW_ALIGN + c * tk, ROW_ALIGN), tk), :].T
        vt_ref[1 + c, HEAD_DIM:va, :] = ones
        return 0

    lax.fori_loop(0, n_chunks, fill, 0)

    buf_a = (sa_ref, mxa_ref)
    buf_b = (sb_ref, mxb_ref)

    def load_queries(q_off, n):
        qt_ref[:, 0:n] = qa_ref[pl.ds(q_off, n), :].T

    def scores(buf, nq, k_off, nk, col0):
        s_ref, mx_ref = buf
        kc = ka_ref[pl.ds(k_off, nk), :]
        s = jnp.dot(kc, qt_ref[:, col0:col0 + nq], preferred_element_type=F32)
        s_ref[0:nk, col0:col0 + nq] = s
        mx_ref[0:1, col0:col0 + nq] = jnp.max(s, axis=0, keepdims=True)

    def absorb(buf, slot, nk, lo, hi, mask_shift=None, lane0=0):
        s_ref, mx_ref = buf
        s = s_ref[0:nk, lo:hi]
        if mask_shift is None:
            smax = mx_ref[0:1, lo:hi]
        else:
            kr = lax.broadcasted_iota(jnp.int32, s.shape, 0) + mask_shift
            qc = lax.broadcasted_iota(jnp.int32, s.shape, 1) + lo
            s = jnp.where(kr > qc, NEG, s)
            smax = jnp.max(s, axis=0, keepdims=True)
        m = m_ref[0:1, lo:hi]
        m_new = jnp.maximum(m, smax)
        alpha = jnp.exp2(m - m_new)
        p = jnp.exp2(s - m_new).astype(BF16)
        m_ref[0:1, lo:hi] = m_new
        pv = jnp.dot(vt_ref[slot, :, lane0:lane0 + nk], p, preferred_element_type=F32)
        acc_ref[:, lo:hi] = alpha * acc_ref[:, lo:hi] + pv

    def reset(n):
        m_ref[0:1, 0:n] = jnp.full((1, n), NEG, F32)
        acc_ref[:, 0:n] = jnp.zeros((va, n), F32)

    def finish(q_off, n):
        out = acc_ref[0:HEAD_DIM, 0:n] / acc_ref[HEAD_DIM:HEAD_DIM + 1, 0:n]
        o_ref[pl.ds(q_off, n), :] = out.T.astype(o_ref.dtype)

    reset(ROW_ALIGN)
    load_queries(0, ROW_ALIGN)
    scores(buf_a, ROW_ALIGN, 0, ROW_ALIGN, 0)
    absorb(buf_a, 0, ROW_ALIGN, 0, ROW_ALIGN, mask_shift=0)
    finish(0, ROW_ALIGN)

    per_tile = tq // tk

    def tile(j, _):
        q_off = pl.multiple_of(ROW_ALIGN + j * tq, ROW_ALIGN)
        reset(tq)
        load_queries(q_off, tq)
        scores(buf_a, tq, PAD, N_META, 0)
        scores(buf_b, tq, ROW_ALIGN, tk, 0)
        absorb(buf_a, 0, N_META, 0, tq, lane0=PAD)

        def pair(p):
            kb = pl.multiple_of(ROW_ALIGN + p * (2 * tk), ROW_ALIGN)
            scores(buf_a, tq, kb + tk, tk, 0)
            absorb(buf_b, 1 + 2 * p, tk, 0, tq)
            scores(buf_b, tq, kb + 2 * tk, tk, 0)
            absorb(buf_a, 2 + 2 * p, tk, 0, tq)

        def two_pairs(pp, _):
            pair(2 * pp)
            pair(2 * pp + 1)
            return 0

        n_pairs = j * (per_tile // 2)
        lax.fori_loop(0, n_pairs // 2, two_pairs, 0)

        @pl.when(n_pairs % 2 == 1)
        def _():
            pair(n_pairs - 1)
        scores(buf_a, tq - tk, q_off + tk, tk, tk)
        absorb(buf_b, 1 + j * per_tile, tk, 0, tq, mask_shift=0)
        absorb(buf_a, 2 + j * per_tile, tk, tk, tq, mask_shift=tk)
        finish(q_off, tq)
        return 0

    lax.fori_loop(0, n_main, tile, 0)


def _flash(z, qb, kb, bsz, lp):
    assert (lp - ROW_ALIGN) % ATT_TQ == 0 and ATT_TQ == 2 * ATT_TK
    bias_spec = pl.BlockSpec((1, lp, HEAD_DIM), lambda b, h: (h, b, 0))
    va = HEAD_DIM + 16
    return pl.pallas_call(
        _flash_kernel,
        grid=(bsz, HEADS),
        in_specs=[
            pl.BlockSpec((lp, HEAD_DIM), lambda b, h: (b, h)),
            pl.BlockSpec((lp, HEAD_DIM), lambda b, h: (b, HEADS + h)),
            pl.BlockSpec((lp, HEAD_DIM), lambda b, h: (b, 2 * HEADS + h)),
            bias_spec, bias_spec,
        ],
        out_specs=pl.BlockSpec((lp, HEAD_DIM), lambda b, h: (b, h)),
        out_shape=jax.ShapeDtypeStruct((bsz * lp, D_ATTN), BF16),
        scratch_shapes=[
            pltpu.VMEM((lp, QK_AUG), BF16),
            pltpu.VMEM((lp, QK_AUG), BF16),
            pltpu.VMEM((QK_AUG, ATT_TQ), BF16),
            pltpu.VMEM((1 + (lp - ROW_ALIGN) // ATT_TK, va, ATT_TK), BF16),
            pltpu.VMEM((ATT_TK, ATT_TQ + 128), F32),
            pltpu.VMEM((ATT_TK, ATT_TQ + 128), F32),
            pltpu.VMEM((8, ATT_TQ), F32),
            pltpu.VMEM((8, ATT_TQ), F32),
            pltpu.VMEM((8, ATT_TQ), F32),
            pltpu.VMEM((va, ATT_TQ + 128), F32),
        ],
        compiler_params=_params(("parallel", "parallel")),
        name="flash",
    )(z, z, z, qb, kb)


def _tile_lanes(x, reps):
    return jnp.concatenate([x] * reps, axis=1)


def _ssm_kernel(u_ref, lr_ref, li_ref, dt_ref, lrl_ref, lil_ref, dtl_ref, btr_ref, bti_ref, cr_ref, ci_ref,
                y_ref, lhs_scr, toep_scr, inj_scr, out_scr, s_scr, hp_scr):
    t = SSM_CHUNK
    lanes = SSM_SLAB * SSM_GROUP
    ns = SSM_SLAB * SSM_STATE
    nc = u_ref.shape[0] // t

    @pl.when(pl.program_id(1) == 0)
    def _build_weights():
        lr = lr_ref[0]
        li = li_ref[0]
        dt = jnp.exp(dt_ref[0])
        btr, bti = btr_ref[0], bti_ref[0]
        cr, ci = cr_ref[0], ci_ref[0]

        mag = jnp.exp(lr * dt)
        a_re = mag * jnp.cos(li * dt)
        a_im = mag * jnp.sin(li * dt)
        powers = [(jnp.ones_like(a_re), jnp.zeros_like(a_re))]
        for _ in range(t):
            pr, pi = powers[-1]
            powers.append((pr * a_re - pi * a_im, pr * a_im + pi * a_re))
        den = lr * lr + li * li
        nr = a_re - 1.0
        z_re = (nr * lr + a_im * li) / den
        z_im = (a_im * lr - nr * li) / den
        row_g = lax.broadcasted_iota(jnp.int32, (lanes, ns), 0) // SSM_GROUP
        col_g = lax.broadcasted_iota(jnp.int32, (lanes, ns), 1) // SSM_STATE
        same = row_g == col_g

        def spread(x):
            return jnp.where(same, _tile_lanes(x, SSM_SLAB), 0.0)

        caz_re, caz_im = [], []
        for d in range(t):
            pr, pi = powers[d]
            azr = pr * z_re - pi * z_im
            azi = pr * z_im + pi * z_re
            caz_re.append(cr * azr - ci * azi)
            caz_im.append(cr * azi + ci * azr)
            i = t - 1 - d
            sr = btr * azr - bti * azi
            si = btr * azi + bti * azr
            inj_scr[i * lanes:(i + 1) * lanes, 0:ns] = spread(sr).astype(BF16)
            inj_scr[i * lanes:(i + 1) * lanes, ns:2 * ns] = spread(si).astype(BF16)
            qr, qi = powers[d + 1]
            er = cr * qr - ci * qi
            ei = cr * qi + ci * qr
            out_scr[d * lanes:(d + 1) * lanes, 0:ns] = spread(er).astype(BF16)
            out_scr[d * lanes:(d + 1) * lanes, ns:2 * ns] = spread(-ei).astype(BF16)
        nt_dims = (((1,), (1,)), ((), ()))
        hp = lax.Precision.HIGHEST
        r0 = (lax.dot_general(btr, jnp.concatenate(caz_re, axis=0), nt_dims, precision=hp, preferred_element_type=F32)
              - lax.dot_general(bti, jnp.concatenate(caz_im, axis=0), nt_dims, precision=hp, preferred_element_type=F32))
        rg = lax.broadcasted_iota(jnp.int32, r0.shape, 0) // SSM_GROUP
        cg = (lax.broadcasted_iota(jnp.int32, r0.shape, 1) % lanes) // SSM_GROUP
        r0 = jnp.where(rg == cg, r0, 0.0).astype(BF16)
        toep_scr[...] = jnp.zeros_like(toep_scr)
        for i in range(t):
            toep_scr[i * lanes:(i + 1) * lanes, i * lanes:t * lanes] = r0[:, 0:(t - i) * lanes]

    for i in range(t):
        lhs_scr[:, i * lanes:(i + 1) * lanes] = u_ref[pl.ds(i, nc, stride=t), :].astype(BF16)

    s_scr[...] = jnp.dot(lhs_scr[...], inj_scr[...], preferred_element_type=F32)

    dtl = jnp.exp(dtl_ref[0])
    mag = jnp.exp(float(t) * (lrl_ref[0] * dtl))
    ang = float(t) * (lil_ref[0] * dtl)
    ar = mag * jnp.cos(ang)
    ai = mag * jnp.sin(ang)

    def step(n, carry):
        hr, hi = carry
        hp_scr[pl.ds(n, 1), 0:ns] = hr
        hp_scr[pl.ds(n, 1), ns:2 * ns] = hi
        sr = s_scr[pl.ds(n, 1), 0:ns]
        si = s_scr[pl.ds(n, 1), ns:2 * ns]
        return ar * hr - ai * hi + sr, ar * hi + ai * hr + si

    zero = jnp.zeros((1, ns), F32)
    lax.fori_loop(0, nc, step, (zero, zero), unroll=8)

    hprev = hp_scr[...].astype(BF16)
    nt_dims = (((1,), (1,)), ((), ()))
    pair = 2 * lanes
    for jj in range(t // 2):
        kdim = (jj + 1) * pair
        yj = (jnp.dot(lhs_scr[:, 0:kdim], toep_scr[0:kdim, jj * pair:(jj + 1) * pair], preferred_element_type=F32)
              + lax.dot_general(hprev, out_scr[jj * pair:(jj + 1) * pair, :], nt_dims, preferred_element_type=F32))
        y_ref[pl.ds(2 * jj, nc, stride=t), :] = yj[:, 0:lanes]
        y_ref[pl.ds(2 * jj + 1, nc, stride=t), :] = yj[:, lanes:pair]


def _ssm(u, lam_re, lam_im, log_dt, b_re, b_im, c_re, c_im, bsz, lp):
    rows, ds = u.shape
    g, p = lam_re.shape
    t, c = SSM_CHUNK, SSM_GROUP
    lanes = SSM_SLAB * c
    ns = SSM_SLAB * p
    nslab = g // SSM_SLAB
    nc = lp // t
    assert g % SSM_SLAB == 0 and lp % t == 0 and lanes == 128
    rep = lambda x: jnp.repeat(x, c, axis=0).reshape(nslab, lanes, -1)
    lane = lambda x: x.reshape(nslab, 1, ns)
    args = (
        u,
        rep(lam_re), rep(lam_im), rep(log_dt.reshape(g, 1)),
        lane(lam_re), lane(lam_im), lane(jnp.repeat(log_dt, p)),
        jnp.swapaxes(b_re, 1, 2).reshape(nslab, lanes, p), jnp.swapaxes(b_im, 1, 2).reshape(nslab, lanes, p),
        c_re.reshape(nslab, lanes, p), c_im.reshape(nslab, lanes, p),
    )
    rows_spec = pl.BlockSpec((1, lanes, p), lambda s, b: (s, 0, 0))
    lane_spec = pl.BlockSpec((1, 1, ns), lambda s, b: (s, 0, 0))
    y = pl.pallas_call(
        _ssm_kernel,
        grid=(nslab, bsz),
        in_specs=[
            pl.BlockSpec((lp, lanes), lambda s, b: (b, s)),
            rows_spec, rows_spec, pl.BlockSpec((1, lanes, 1), lambda s, b: (s, 0, 0)),
            lane_spec, lane_spec, lane_spec,
            rows_spec, rows_spec, rows_spec, rows_spec,
        ],
        out_specs=pl.BlockSpec((lp, lanes), lambda s, b: (b, s)),
        out_shape=jax.ShapeDtypeStruct((rows, ds), F32),
        scratch_shapes=[
            pltpu.VMEM((nc, t * lanes), BF16),
            pltpu.VMEM((t * lanes, t * lanes), BF16),
            pltpu.VMEM((t * lanes, 2 * ns), BF16),
            pltpu.VMEM((t * lanes, 2 * ns), BF16),
            pltpu.VMEM((nc, 2 * ns), F32),
            pltpu.VMEM((nc, 2 * ns), F32),
        ],
        compiler_params=_params(("parallel", "arbitrary")),
        name="ssm",
    )(*args)
    return y


def _mix_kernel(y_ref, u_ref, a_ref, ga_ref, gb_ref, head_ref, x_ref, dsk_ref, gffn_ref,
                wglu_ref, wao_ref, wout_ref, h_ref, n_ref, *, nt):
    d = x_ref.shape[1]
    ao = jnp.dot(a_ref[...], wao_ref[...], preferred_element_type=F32)
    y = y_ref[...].astype(F32) + dsk_ref[...] * u_ref[...].astype(F32)
    gy = _gelu_tanh(y).astype(BF16)
    yab = jnp.dot(gy, wglu_ref[...], preferred_element_type=F32)
    ssm_out = yab[:, 0:d] * _sigmoid(yab[:, d:2 * d])
    merged = (_sigmoid(ga_ref[...].astype(F32)) * ssm_out
              + _sigmoid(gb_ref[...].astype(F32)) * ao).astype(BF16)
    x = _padded_rows(pl.program_id(0), nt, head_ref, x_ref)
    h = x + jnp.dot(merged, wout_ref[...], preferred_element_type=F32)
    h_ref[...] = h
    ms = jnp.mean(h * h, axis=-1, keepdims=True)
    n_ref[...] = (h * lax.rsqrt(ms + EPS) * gffn_ref[...]).astype(n_ref.dtype)


def _mix(y, z, attn, head, x2, nt, d_skip, g_ffn, w_glu, w_ao, w_out):
    rows, ds = y.shape
    d = x2.shape[1]
    tm = ROW_ALIGN
    u_col = 3 * D_ATTN // ds
    ga_col = (3 * D_ATTN + ds) // d
    assert (3 * D_ATTN) % ds == 0 and (3 * D_ATTN + ds) % d == 0
    return pl.pallas_call(
        functools.partial(_mix_kernel, nt=nt),
        grid=(rows // tm,),
        in_specs=[
            pl.BlockSpec((tm, ds), lambda i: (i, 0)),
            pl.BlockSpec((tm, ds), lambda i: (i, u_col)),
            pl.BlockSpec((tm, D_ATTN), lambda i: (i, 0)),
            pl.BlockSpec((tm, d), lambda i: (i, ga_col)),
            pl.BlockSpec((tm, d), lambda i: (i, ga_col + 1)),
            _resident((tm, d), lambda i: (0, 0)),
            pl.BlockSpec((tm, d), lambda i: (_seq_block(i, nt), 0)),
            _resident((1, ds), lambda i: (0, 0)),
            _resident((1, d), lambda i: (0, 0)),
            _resident(w_glu.shape, lambda i: (0, 0)),
            _resident(w_ao.shape, lambda i: (0, 0)),
            _resident(w_out.shape, lambda i: (0, 0)),
        ],
        out_specs=[
            pl.BlockSpec((tm, d), lambda i: (i, 0)),
            pl.BlockSpec((tm, d), lambda i: (i, 0)),
        ],
        out_shape=[
            jax.ShapeDtypeStruct((rows, d), F32),
            jax.ShapeDtypeStruct((rows, d), BF16),
        ],
        compiler_params=_params(("parallel",)),
        name="mix_out",
    )(y, z, attn, z, z, head, x2, d_skip, g_ffn, w_glu, w_ao, w_out)


def _ffn_up_kernel(n_ref, wg_ref, wu_ref, cw_ref, cb_ref, a_ref, tail, wg_scr, wu_scr):
    tm = n_ref.shape[0]

    @pl.when(pl.program_id(1) == 0)
    def _():
        tail[...] = jnp.zeros_like(tail)
        wg_scr[...] = wg_ref[...].astype(wg_scr.dtype)
        wu_scr[...] = wu_ref[...].astype(wu_scr.dtype)

    ts = _largest_tile(tm, (512, 256))
    prev1 = tail[CONV_WIDTH - 2:CONV_WIDTH - 1, :]
    prev2 = tail[CONV_WIDTH - 3:CONV_WIDTH - 2, :]
    for r0 in range(0, tm, ts):
        n = n_ref[r0:r0 + ts, :]
        g = jnp.dot(n, wg_scr[...], preferred_element_type=F32)
        u = jnp.dot(n, wu_scr[...], preferred_element_type=F32)
        row = lax.broadcasted_iota(jnp.int32, g.shape, 0)
        g1 = jnp.where(row == 0, prev1, pltpu.roll(g, 1, 0))
        g2 = jnp.where(row == 0, prev2, jnp.where(row == 1, prev1, pltpu.roll(g, 2, 0)))
        gc = cb_ref[...] + cw_ref[0:1, :] * g2 + cw_ref[1:2, :] * g1 + cw_ref[2:3, :] * g
        a_ref[r0:r0 + ts, :] = (gc * _sigmoid(gc) * u).astype(a_ref.dtype)
        prev1 = g[ts - 1:ts, :]
        prev2 = g[ts - 2:ts - 1, :]
    tail[CONV_WIDTH - 2:CONV_WIDTH - 1, :] = prev1
    tail[CONV_WIDTH - 3:CONV_WIDTH - 2, :] = prev2


def _largest_tile(n, candidates):
    return next(c for c in candidates if n % c == 0)


def _ffn_up(n2, w_up, conv_w, conv_b, tn=512):
    rows, d = n2.shape
    dff = conv_w.shape[1]
    nj = dff // tn
    tm = _largest_tile(rows, (1536, 1024, 768, 512, 256))
    assert dff % tn == 0
    return pl.pallas_call(
        _ffn_up_kernel,
        grid=(nj, rows // tm),
        in_specs=[
            pl.BlockSpec((tm, d), lambda j, i: (i, 0)),
            pl.BlockSpec((d, tn), lambda j, i: (0, j)),
            pl.BlockSpec((d, tn), lambda j, i: (0, nj + j)),
            pl.BlockSpec((CONV_WIDTH, tn), lambda j, i: (0, j)),
            pl.BlockSpec((1, tn), lambda j, i: (0, j)),
        ],
        out_specs=pl.BlockSpec((tm, tn), lambda j, i: (i, j)),
        out_shape=jax.ShapeDtypeStruct((rows, dff), BF16),
        scratch_shapes=[pltpu.VMEM((8, tn), F32), pltpu.VMEM((d, tn), BF16), pltpu.VMEM((d, tn), BF16)],
        compiler_params=_params(("parallel", "arbitrary")),
        name="ffn_up",
    )(n2, w_up, w_up, conv_w, conv_b)


def _ffn_down_kernel(a_ref, w_ref, h_ref, g_ref, o_ref):
    h = h_ref[...] + jnp.dot(a_ref[...], w_ref[...], preferred_element_type=F32)
    ms = jnp.mean(h * h, axis=-1, keepdims=True)
    o_ref[...] = (h * lax.rsqrt(ms + EPS) * g_ref[...]).astype(o_ref.dtype)


def _ffn_down(act, w_down, h1, g_final, bsz, lp, tm=ROW_ALIGN):
    dff, d = w_down.shape
    nt_in = lp // tm
    nt_out = nt_in - ROW_ALIGN // tm
    skip = ROW_ALIGN // tm
    return pl.pallas_call(
        _ffn_down_kernel,
        grid=(bsz, nt_out),
        in_specs=[
            pl.BlockSpec((tm, dff), lambda b, t: (b * nt_in + skip + t, 0)),
            _resident((dff, d), lambda b, t: (0, 0)),
            pl.BlockSpec((tm, d), lambda b, t: (b * nt_in + skip + t, 0)),
            _resident((1, d), lambda b, t: (0, 0)),
        ],
        out_specs=pl.BlockSpec((tm, d), lambda b, t: (b * nt_out + t, 0)),
        out_shape=jax.ShapeDtypeStruct((bsz * nt_out * tm, d), F32),
        compiler_params=_params(("parallel", "parallel")),
        name="ffn_down",
    )(act, w_down, h1, g_final)


def _layer(head, x2, bsz, lp, g_mix, w_in, b_f, lam_re, lam_im, log_dt, b_re, b_im, c_re, c_im, d_skip,
           w_glu, w_attn_o, w_out, g_ffn, w_up, conv_w, conv_b, w_down):
    d = x2.shape[1]
    ds = d_skip.shape[0]
    nt = lp // ROW_ALIGN
    o_q, o_k, o_v, o_f, o_u, o_ga, o_gb = (0, D_ATTN, 2 * D_ATTN, 3 * D_ATTN, 3 * D_ATTN + HEADS,
                                            3 * D_ATTN + HEADS + ds, 3 * D_ATTN + HEADS + ds + d)
    w_main, w_f = _win_pack(w_in, o_f, o_u, HEAD_DIM ** -0.5 * LOG2E)
    bf = jnp.pad(b_f.astype(F32), (0, 128 - HEADS)).reshape(1, 128)

    z, u, qb, kb = _inproj(head, x2, g_mix.reshape(1, d), w_main, w_f, bf, bsz * lp, nt, 3 * D_ATTN, ds)
    attn = _flash(z, qb, kb, bsz, lp)
    y = _ssm(u, lam_re, lam_im, log_dt, b_re, b_im, c_re, c_im, bsz, lp)

    h1, n2 = _mix(y, z, attn, head, x2, nt, d_skip.reshape(1, ds).astype(F32), g_ffn.reshape(1, d).astype(F32),
                  w_glu.astype(BF16), w_attn_o.astype(BF16), w_out.astype(BF16))
    act = _ffn_up(n2, w_up.astype(F32), conv_w.astype(F32), conv_b.reshape(1, -1).astype(F32))
    return act, h1


def kernel(x, meta, g_mix, w_in, b_f, lam_re, lam_im, log_dt, b_re, b_im, c_re, c_im, d_skip,
           w_glu, w_attn_o, w_out, g_ffn, w_up, conv_w, conv_b, w_down, g_final):
    bsz, seq, d = x.shape
    depth = g_mix.shape[0]
    assert depth == 1 and meta.shape[0] == N_META and seq % ROW_ALIGN == 0
    lp = seq + ROW_ALIGN
    assert (bsz * lp) % 512 == 0
    head = jnp.concatenate([jnp.zeros((PAD, d), x.dtype), meta.astype(x.dtype)], axis=0)
    act, h1 = _layer(head, x.reshape(bsz * seq, d), bsz, lp, g_mix[0], w_in[0], b_f[0], lam_re[0], lam_im[0], log_dt[0],
                     b_re[0], b_im[0], c_re[0], c_im[0], d_skip[0], w_glu[0], w_attn_o[0], w_out[0],
                     g_ffn[0], w_up[0], conv_w[0], conv_b[0], w_down[0])
    out = _ffn_down(act, w_down[0].astype(BF16), h1, g_final.reshape(1, d).astype(F32), bsz, lp)
    return out.reshape(bsz, seq, d)
```

```python
import functools
import math

import jax
import jax.numpy as jnp
from jax import lax
from jax.experimental import pallas as pl
from jax.experimental.pallas import tpu as pltpu

N_META = 16
HEADS = 8
HEAD_DIM = 128
D_ATTN = HEADS * HEAD_DIM
SSM_GROUP = 16
SSM_STATE = 64
SSM_CHUNK = 12
SSM_SLAB = 8
CONV_WIDTH = 3
EPS = 1e-6

ROW_ALIGN = 256
PAD = ROW_ALIGN - N_META
ATT_TQ = 1024
ATT_TK = 512
QK_AUG = 2 * HEAD_DIM
LOG2E = math.log2(math.e)
MASK_BIG = 1e30
NEG = -3e38
VMEM_LIMIT = 56 * 1024 * 1024

F32 = jnp.float32
BF16 = jnp.bfloat16


def _sigmoid(x):
    return 1.0 / (1.0 + jnp.exp(-x))


def _gelu_tanh(x):
    c = math.sqrt(2.0 / math.pi)
    return 0.5 * x * (1.0 + jnp.tanh(c * (x + 0.044715 * (x * x * x))))


def _params(sem, limit=VMEM_LIMIT):
    return pltpu.CompilerParams(dimension_semantics=sem, vmem_limit_bytes=limit)


def _resident(shape, index_map):
    return pl.BlockSpec(shape, index_map, pipeline_mode=pl.Buffered(1))


def _win_pack_kernel(wt_ref, o_ref, f_ref, *, o_f, o_u, q_scale):
    n_in, tr = wt_ref.shape
    o_ref[:, 0:D_ATTN] = (wt_ref[0:D_ATTN, :] * q_scale).T.astype(o_ref.dtype)
    o_ref[:, D_ATTN:o_f] = wt_ref[D_ATTN:o_f, :].T.astype(o_ref.dtype)
    o_ref[:, o_f:o_f + (n_in - o_u)] = wt_ref[o_u:n_in, :].T.astype(o_ref.dtype)
    wf = jnp.concatenate([wt_ref[o_f:o_u, :], jnp.zeros((f_ref.shape[1] - (o_u - o_f), tr), F32)], axis=0)
    f_ref[...] = wf.T.astype(f_ref.dtype)


def _win_pack(w_in, o_f, o_u, q_scale, tr=256):
    d, n_in = w_in.shape
    n_out = n_in - (o_u - o_f)
    return pl.pallas_call(
        functools.partial(_win_pack_kernel, o_f=o_f, o_u=o_u, q_scale=q_scale),
        grid=(d // tr,),
        in_specs=[pl.BlockSpec((n_in, tr), lambda i: (0, i))],
        out_specs=[pl.BlockSpec((tr, n_out), lambda i: (i, 0)), pl.BlockSpec((tr, 128), lambda i: (i, 0))],
        out_shape=[jax.ShapeDtypeStruct((d, n_out), BF16), jax.ShapeDtypeStruct((d, 128), BF16)],
        compiler_params=_params(("parallel",)),
        name="win_pack",
    )(w_in.T)


def _seq_block(t, nt):
    return (t // nt) * (nt - 1) + jnp.maximum(t % nt - 1, 0)


def _padded_rows(t, nt, head_ref, x_ref):
    return jnp.where(t % nt == 0, head_ref[...], x_ref[...])


def _split3(x):
    hi = x.astype(BF16)
    r1 = x - hi.astype(F32)
    mid = r1.astype(BF16)
    lo = (r1 - mid.astype(F32)).astype(BF16)
    return hi, mid, lo


def _forget_bias(f, bf_ref, t, nt, carry, qb_ref, kb_ref, row0):
    tm = f.shape[0]
    first = t % nt == 0
    x = f + bf_ref[...]
    logf = jnp.minimum(x, 0.0) - jnp.log1p(jnp.exp(-jnp.abs(x)))
    row = lax.broadcasted_iota(jnp.int32, (tm, 1), 0)
    valid = jnp.logical_or(jnp.logical_not(first), row >= PAD)
    logf = jnp.where(valid, logf, 0.0)
    r = lax.broadcasted_iota(jnp.int32, (tm, tm), 0)
    c = lax.broadcasted_iota(jnp.int32, (tm, tm), 1)
    tri = (r >= c).astype(BF16)
    cs3 = jnp.dot(tri, jnp.concatenate(_split3(logf), axis=1), preferred_element_type=F32)
    w = f.shape[1]
    fcum = cs3[:, 0:w] + cs3[:, w:2 * w] + cs3[:, 2 * w:3 * w] + jnp.where(first, 0.0, carry[0:1, :])
    carry[0:1, :] = fcum[tm - 1:tm, :]

    lane = lax.broadcasted_iota(jnp.int32, (tm, HEAD_DIM), 1)
    one = jnp.ones((tm, HEAD_DIM), F32)
    zero = jnp.zeros((tm, HEAD_DIM), F32)
    for h in range(HEADS):
        hi, mid, lo = (p.astype(F32) for p in _split3(fcum[:, h:h + 1] * LOG2E))
        qb = jnp.where(lane == 0, hi, jnp.where(lane == 1, mid, jnp.where(lane == 2, lo,
             jnp.where(lane < 6, one, zero))))
        khi = jnp.where(valid, hi, MASK_BIG)
        kmid = jnp.where(valid, mid, 0.0)
        klo = jnp.where(valid, lo, 0.0)
        kb = jnp.where(lane < 3, one, jnp.where(lane == 3, -khi, jnp.where(lane == 4, -kmid,
             jnp.where(lane == 5, -klo, zero))))
        qb_ref[h, row0:row0 + tm, :] = qb.astype(qb_ref.dtype)
        kb_ref[h, row0:row0 + tm, :] = kb.astype(kb_ref.dtype)


def _inproj_kernel(head_ref, xa_ref, xb_ref, g_ref, w_ref, wf_ref, bf_ref, z_ref, u_ref, qb_ref, kb_ref,
                   n_scr, carry, *, nt, u_blk, u_lo):
    i = pl.program_id(0)
    j = pl.program_id(1)
    n_i = pl.num_programs(0)
    n_j = pl.num_programs(1)

    def prepare(t):
        x = jnp.concatenate([_padded_rows(2 * t, nt, head_ref, xa_ref),
                             _padded_rows(2 * t + 1, nt, head_ref, xb_ref)], axis=0)
        ms = jnp.mean(x * x, axis=-1, keepdims=True)
        n = (x * lax.rsqrt(ms + EPS) * g_ref[...]).astype(BF16)
        n_scr[t % 2] = n
        f = jnp.dot(n, wf_ref[...], preferred_element_type=F32)
        for blk in range(2):
            _forget_bias(f[blk * ROW_ALIGN:(blk + 1) * ROW_ALIGN], bf_ref, 2 * t + blk, nt, carry,
                         qb_ref, kb_ref, blk * ROW_ALIGN)

    def project():
        acc = jnp.dot(n_scr[i % 2], w_ref[...], preferred_element_type=F32)
        z_ref[...] = acc.astype(z_ref.dtype)
        return acc

    @pl.when(jnp.logical_and(i == 0, j == 0))
    def _():
        carry[...] = jnp.zeros_like(carry)
        prepare(0)

    lookahead = jnp.logical_and(j == n_j - 1, i + 1 < n_i)

    @pl.when(lookahead)
    def _():
        prepare(i + 1)
        project()

    @pl.when(jnp.logical_not(lookahead))
    def _():
        acc = project()

        @pl.when(j == u_blk)
        def _():
            u_ref[...] = acc[:, u_lo:u_lo + u_ref.shape[1]]


def _inproj(head, x2, g, w, wf, bf, rows, nt, u_off, u_width, tn=2048):
    d = x2.shape[1]
    tm = 2 * ROW_ALIGN
    n_out = w.shape[1]
    n_i, n_j = rows // tm, n_out // tn
    u_blk, u_lo = divmod(u_off, tn)
    assert u_lo + u_width <= tn and rows % tm == 0 and u_blk < n_j - 1

    def tile(i, j):
        return jnp.minimum(i + (j == n_j - 1).astype(jnp.int32), n_i - 1)

    bias_spec = pl.BlockSpec((HEADS, tm, HEAD_DIM), lambda i, j: (0, tile(i, j), 0))
    bias_shape = jax.ShapeDtypeStruct((HEADS, rows, HEAD_DIM), BF16)
    return pl.pallas_call(
        functools.partial(_inproj_kernel, nt=nt, u_blk=u_blk, u_lo=u_lo),
        grid=(n_i, n_j),
        in_specs=[
            _resident((ROW_ALIGN, d), lambda i, j: (0, 0)),
            pl.BlockSpec((ROW_ALIGN, d), lambda i, j: (_seq_block(2 * tile(i, j), nt), 0)),
            pl.BlockSpec((ROW_ALIGN, d), lambda i, j: (_seq_block(2 * tile(i, j) + 1, nt), 0)),
            pl.BlockSpec((1, d), lambda i, j: (0, 0)),
            pl.BlockSpec((d, tn), lambda i, j: (0, j)),
            pl.BlockSpec((d, 128), lambda i, j: (0, 0)),
            pl.BlockSpec((1, 128), lambda i, j: (0, 0)),
        ],
        out_specs=[
            pl.BlockSpec((tm, tn), lambda i, j: (i, j)),
            pl.BlockSpec((tm, u_width), lambda i, j: (i, 0)),
            bias_spec, bias_spec,
        ],
        out_shape=[
            jax.ShapeDtypeStruct((rows, n_out), BF16),
            jax.ShapeDtypeStruct((rows, u_width), F32),
            bias_shape, bias_shape,
        ],
        scratch_shapes=[pltpu.VMEM((2, tm, d), BF16), pltpu.VMEM((8, 128), F32)],
        compiler_params=_params(("arbitrary", "arbitrary")),
        name="inproj",
    )(head, x2, x2, g, w, wf, bf)


def _flash_kernel(q_ref, k_ref, v_ref, qb_ref, kb_ref, o_ref,
                  qa_ref, ka_ref, qt_ref, vt_ref, sa_ref, sb_ref, mxa_ref, mxb_ref, m_ref, acc_ref):
    lp = v_ref.shape[0]
    tq, tk = ATT_TQ, ATT_TK
    n_main = (lp - ROW_ALIGN) // tq
    n_chunks = (lp - ROW_ALIGN) // tk
    va = vt_ref.shape[1]

    qa_ref[:, 0:HEAD_DIM] = q_ref[...]
    qa_ref[:, HEAD_DIM:QK_AUG] = qb_ref[0]
    ka_ref[:, 0:HEAD_DIM] = k_ref[...]
    ka_ref[:, HEAD_DIM:QK_AUG] = kb_ref[0]

    ones = jnp.ones((va - HEAD_DIM, tk), vt_ref.dtype)
    vt_ref[0, 0:HEAD_DIM, 0:ROW_ALIGN] = v_ref[0:ROW_ALIGN, :].T
    vt_ref[0, HEAD_DIM:va, :] = ones

    def fill(c, _):
        vt_ref[1 + c, 0:HEAD_DIM, :] = v_ref[pl.ds(pl.multiple_of(ROW_ALIGN + c * tk, ROW_ALIGN), tk), :].T
        vt_ref[1 + c, HEAD_DIM:va, :] = ones
        return 0

    lax.fori_loop(0, n_chunks, fill, 0)

    buf_a = (sa_ref, mxa_ref)
    buf_b = (sb_ref, mxb_ref)

    def load_queries(slot, q_off, n):
        qt_ref[slot, :, 0:n] = qa_ref[pl.ds(q_off, n), :].T

    def scores(buf, slot, nq, k_off, nk, col0):
        s_ref, mx_ref = buf
        kc = ka_ref[pl.ds(k_off, nk), :]
        s = jnp.dot(kc, qt_ref[slot, :, col0:col0 + nq], preferred_element_type=F32)
        s_ref[0:nk, col0:col0 + nq] = s
        mx_ref[0:1, col0:col0 + nq] = jnp.max(s, axis=0, keepdims=True)

    def absorb(buf, slot, nk, lo, hi, mask_shift=None, lane0=0):
        s_ref, mx_ref = buf
        s = s_ref[0:nk, lo:hi]
        if mask_shift is None:
            smax = mx_ref[0:1, lo:hi]
        else:
            kr = lax.broadcasted_iota(jnp.int32, s.shape, 0) + mask_shift
            qc = lax.broadcasted_iota(jnp.int32, s.shape, 1) + lo
            s = jnp.where(kr > qc, NEG, s)
            smax = jnp.max(s, axis=0, keepdims=True)
        m = m_ref[0:1, lo:hi]
        m_new = jnp.maximum(m, smax)
        alpha = jnp.exp2(m - m_new)
        p = jnp.exp2(s - m_new).astype(BF16)
        m_ref[0:1, lo:hi] = m_new
        pv = jnp.dot(vt_ref[slot, :, lane0:lane0 + nk], p, preferred_element_type=F32)
        acc_ref[:, lo:hi] = alpha * acc_ref[:, lo:hi] + pv

    def reset(n):
        m_ref[0:1, 0:n] = jnp.full((1, n), NEG, F32)
        acc_ref[:, 0:n] = jnp.zeros((va, n), F32)

    def finish(q_off, n):
        out = acc_ref[0:HEAD_DIM, 0:n] / acc_ref[HEAD_DIM:HEAD_DIM + 1, 0:n]
        o_ref[pl.ds(q_off, n), :] = out.T.astype(o_ref.dtype)

    reset(ROW_ALIGN)
    load_queries(1, 0, ROW_ALIGN)
    scores(buf_a, 1, ROW_ALIGN, 0, ROW_ALIGN, 0)
    absorb(buf_a, 0, ROW_ALIGN, 0, ROW_ALIGN, mask_shift=0)
    finish(0, ROW_ALIGN)

    per_tile = tq // tk
    load_queries(0, ROW_ALIGN, tq)

    def tile(j, _):
        q_off = pl.multiple_of(ROW_ALIGN + j * tq, ROW_ALIGN)
        qs = j % 2
        reset(tq)
        scores(buf_a, qs, tq, PAD, N_META, 0)
        scores(buf_b, qs, tq, ROW_ALIGN, tk, 0)
        absorb(buf_a, 0, N_META, 0, tq, lane0=PAD)

        def pair(p):
            kb = pl.multiple_of(ROW_ALIGN + p * (2 * tk), ROW_ALIGN)
            scores(buf_a, qs, tq, kb + tk, tk, 0)
            absorb(buf_b, 1 + 2 * p, tk, 0, tq)
            scores(buf_b, qs, tq, kb + 2 * tk, tk, 0)
            absorb(buf_a, 2 + 2 * p, tk, 0, tq)

        def two_pairs(pp, _):
            pair(2 * pp)
            pair(2 * pp + 1)
            return 0

        n_pairs = j * (per_tile // 2)
        lax.fori_loop(0, n_pairs // 2, two_pairs, 0)

        @pl.when(n_pairs % 2 == 1)
        def _():
            pair(n_pairs - 1)
        scores(buf_a, qs, tq - tk, q_off + tk, tk, tk)
        load_queries(1 - qs, pl.multiple_of(jnp.minimum(q_off + tq, lp - tq), ROW_ALIGN), tq)
        absorb(buf_b, 1 + j * per_tile, tk, 0, tq, mask_shift=0)
        absorb(buf_a, 2 + j * per_tile, tk, tk, tq, mask_shift=tk)
        finish(q_off, tq)
        return 0

    lax.fori_loop(0, n_main, tile, 0)


def _flash(z, qb, kb, bsz, lp):
    assert (lp - ROW_ALIGN) % ATT_TQ == 0 and ATT_TQ == 2 * ATT_TK
    bias_spec = pl.BlockSpec((1, lp, HEAD_DIM), lambda b, h: (h, b, 0))
    va = HEAD_DIM + 16
    return pl.pallas_call(
        _flash_kernel,
        grid=(bsz, HEADS),
        in_specs=[
            pl.BlockSpec((lp, HEAD_DIM), lambda b, h: (b, h)),
            pl.BlockSpec((lp, HEAD_DIM), lambda b, h: (b, HEADS + h)),
            pl.BlockSpec((lp, HEAD_DIM), lambda b, h: (b, 2 * HEADS + h)),
            bias_spec, bias_spec,
        ],
        out_specs=pl.BlockSpec((lp, HEAD_DIM), lambda b, h: (b, h)),
        out_shape=jax.ShapeDtypeStruct((bsz * lp, D_ATTN), BF16),
        scratch_shapes=[
            pltpu.VMEM((lp, QK_AUG), BF16),
            pltpu.VMEM((lp, QK_AUG), BF16),
            pltpu.VMEM((2, QK_AUG, ATT_TQ), BF16),
            pltpu.VMEM((1 + (lp - ROW_ALIGN) // ATT_TK, va, ATT_TK), BF16),
            pltpu.VMEM((ATT_TK, ATT_TQ + 128), F32),
            pltpu.VMEM((ATT_TK, ATT_TQ + 128), F32),
            pltpu.VMEM((8, ATT_TQ), F32),
            pltpu.VMEM((8, ATT_TQ), F32),
            pltpu.VMEM((8, ATT_TQ), F32),
            pltpu.VMEM((va, ATT_TQ + 128), F32),
        ],
        compiler_params=_params(("parallel", "parallel")),
        name="flash",
    )(z, z, z, qb, kb)


def _tile_lanes(x, reps):
    return jnp.concatenate([x] * reps, axis=1)


def _ssm_kernel(u_ref, lr_ref, li_ref, dt_ref, lrl_ref, lil_ref, dtl_ref, btr_ref, bti_ref, cr_ref, ci_ref,
                y_ref, lhs_scr, toep_scr, inj_scr, out_scr, s_scr, hp_scr):
    t = SSM_CHUNK
    lanes = SSM_SLAB * SSM_GROUP
    ns = SSM_SLAB * SSM_STATE
    nc = u_ref.shape[0] // t

    @pl.when(pl.program_id(1) == 0)
    def _build_weights():
        lr = lr_ref[0]
        li = li_ref[0]
        dt = jnp.exp(dt_ref[0])
        btr, bti = btr_ref[0], bti_ref[0]
        cr, ci = cr_ref[0], ci_ref[0]

        mag = jnp.exp(lr * dt)
        a_re = mag * jnp.cos(li * dt)
        a_im = mag * jnp.sin(li * dt)
        powers = [(jnp.ones_like(a_re), jnp.zeros_like(a_re))]
        for _ in range(t):
            pr, pi = powers[-1]
            powers.append((pr * a_re - pi * a_im, pr * a_im + pi * a_re))
        den = lr * lr + li * li
        nr = a_re - 1.0
        z_re = (nr * lr + a_im * li) / den
        z_im = (a_im * lr - nr * li) / den
        row_g = lax.broadcasted_iota(jnp.int32, (lanes, ns), 0) // SSM_GROUP
        col_g = lax.broadcasted_iota(jnp.int32, (lanes, ns), 1) // SSM_STATE
        same = row_g == col_g

        def spread(x):
            return jnp.where(same, _tile_lanes(x, SSM_SLAB), 0.0)

        caz_re, caz_im = [], []
        for d in range(t):
            pr, pi = powers[d]
            azr = pr * z_re - pi * z_im
            azi = pr * z_im + pi * z_re
            caz_re.append(cr * azr - ci * azi)
            caz_im.append(cr * azi + ci * azr)
            i = t - 1 - d
            sr = btr * azr - bti * azi
            si = btr * azi + bti * azr
            inj_scr[i * lanes:(i + 1) * lanes, 0:ns] = spread(sr).astype(BF16)
            inj_scr[i * lanes:(i + 1) * lanes, ns:2 * ns] = spread(si).astype(BF16)
            qr, qi = powers[d + 1]
            er = cr * qr - ci * qi
            ei = cr * qi + ci * qr
            out_scr[d * lanes:(d + 1) * lanes, 0:ns] = spread(er).astype(BF16)
            out_scr[d * lanes:(d + 1) * lanes, ns:2 * ns] = spread(-ei).astype(BF16)
        nt_dims = (((1,), (1,)), ((), ()))
        hp = lax.Precision.HIGHEST
        r0 = (lax.dot_general(btr, jnp.concatenate(caz_re, axis=0), nt_dims, precision=hp, preferred_element_type=F32)
              - lax.dot_general(bti, jnp.concatenate(caz_im, axis=0), nt_dims, precision=hp, preferred_element_type=F32))
        rg = lax.broadcasted_iota(jnp.int32, r0.shape, 0) // SSM_GROUP
        cg = (lax.broadcasted_iota(jnp.int32, r0.shape, 1) % lanes) // SSM_GROUP
        r0 = jnp.where(rg == cg, r0, 0.0).astype(BF16)
        toep_scr[...] = jnp.zeros_like(toep_scr)
        for i in range(t):
            toep_scr[i * lanes:(i + 1) * lanes, i * lanes:t * lanes] = r0[:, 0:(t - i) * lanes]

    for i in range(t):
        lhs_scr[:, i * lanes:(i + 1) * lanes] = u_ref[pl.ds(i, nc, stride=t), :].astype(BF16)

    s_scr[...] = jnp.dot(lhs_scr[...], inj_scr[...], preferred_element_type=F32)

    dtl = jnp.exp(dtl_ref[0])
    mag = jnp.exp(float(t) * (lrl_ref[0] * dtl))
    ang = float(t) * (lil_ref[0] * dtl)
    ar = mag * jnp.cos(ang)
    ai = mag * jnp.sin(ang)

    def step(n, carry):
        hr, hi = carry
        hp_scr[pl.ds(n, 1), 0:ns] = hr
        hp_scr[pl.ds(n, 1), ns:2 * ns] = hi
        sr = s_scr[pl.ds(n, 1), 0:ns]
        si = s_scr[pl.ds(n, 1), ns:2 * ns]
        return ar * hr - ai * hi + sr, ar * hi + ai * hr + si

    zero = jnp.zeros((1, ns), F32)
    lax.fori_loop(0, nc, step, (zero, zero), unroll=8)

    hprev = hp_scr[...].astype(BF16)
    nt_dims = (((1,), (1,)), ((), ()))
    pair = 2 * lanes
    for jj in range(t // 2):
        kdim = (jj + 1) * pair
        yj = (jnp.dot(lhs_scr[:, 0:kdim], toep_scr[0:kdim, jj * pair:(jj + 1) * pair], preferred_element_type=F32)
              + lax.dot_general(hprev, out_scr[jj * pair:(jj + 1) * pair, :], nt_dims, preferred_element_type=F32))
        y_ref[pl.ds(2 * jj, nc, stride=t), :] = yj[:, 0:lanes]
        y_ref[pl.ds(2 * jj + 1, nc, stride=t), :] = yj[:, lanes:pair]


def _ssm(u, lam_re, lam_im, log_dt, b_re, b_im, c_re, c_im, bsz, lp):
    rows, ds = u.shape
    g, p = lam_re.shape
    t, c = SSM_CHUNK, SSM_GROUP
    lanes = SSM_SLAB * c
    ns = SSM_SLAB * p
    nslab = g // SSM_SLAB
    nc = lp // t
    assert g % SSM_SLAB == 0 and lp % t == 0 and lanes == 128
    rep = lambda x: jnp.repeat(x, c, axis=0).reshape(nslab, lanes, -1)
    lane = lambda x: x.reshape(nslab, 1, ns)
    args = (
        u,
        rep(lam_re), rep(lam_im), rep(log_dt.reshape(g, 1)),
        lane(lam_re), lane(lam_im), lane(jnp.repeat(log_dt, p)),
        jnp.swapaxes(b_re, 1, 2).reshape(nslab, lanes, p), jnp.swapaxes(b_im, 1, 2).reshape(nslab, lanes, p),
        c_re.reshape(nslab, lanes, p), c_im.reshape(nslab, lanes, p),
    )
    rows_spec = pl.BlockSpec((1, lanes, p), lambda s, b: (s, 0, 0))
    lane_spec = pl.BlockSpec((1, 1, ns), lambda s, b: (s, 0, 0))
    y = pl.pallas_call(
        _ssm_kernel,
        grid=(nslab, bsz),
        in_specs=[
            pl.BlockSpec((lp, lanes), lambda s, b: (b, s)),
            rows_spec, rows_spec, pl.BlockSpec((1, lanes, 1), lambda s, b: (s, 0, 0)),
            lane_spec, lane_spec, lane_spec,
            rows_spec, rows_spec, rows_spec, rows_spec,
        ],
        out_specs=pl.BlockSpec((lp, lanes), lambda s, b: (b, s)),
        out_shape=jax.ShapeDtypeStruct((rows, ds), F32),
        scratch_shapes=[
            pltpu.VMEM((nc, t * lanes), BF16),
            pltpu.VMEM((t * lanes, t * lanes), BF16),
            pltpu.VMEM((t * lanes, 2 * ns), BF16),
            pltpu.VMEM((t * lanes, 2 * ns), BF16),
            pltpu.VMEM((nc, 2 * ns), F32),
            pltpu.VMEM((nc, 2 * ns), F32),
        ],
        compiler_params=_params(("parallel", "arbitrary")),
        name="ssm",
    )(*args)
    return y


def _mix_kernel(y_ref, u_ref, a_ref, ga_ref, gb_ref, head_ref, x_ref, dsk_ref, gffn_ref,
                wglu_ref, wao_ref, wout_ref, h_ref, n_ref, *, nt):
    d = x_ref.shape[1]
    ao = jnp.dot(a_ref[...], wao_ref[...], preferred_element_type=F32)
    y = y_ref[...].astype(F32) + dsk_ref[...] * u_ref[...].astype(F32)
    gy = _gelu_tanh(y).astype(BF16)
    yab = jnp.dot(gy, wglu_ref[...], preferred_element_type=F32)
    ssm_out = yab[:, 0:d] * _sigmoid(yab[:, d:2 * d])
    merged = (_sigmoid(ga_ref[...].astype(F32)) * ssm_out
              + _sigmoid(gb_ref[...].astype(F32)) * ao).astype(BF16)
    x = _padded_rows(pl.program_id(0), nt, head_ref, x_ref)
    h = x + jnp.dot(merged, wout_ref[...], preferred_element_type=F32)
    h_ref[...] = h
    ms = jnp.mean(h * h, axis=-1, keepdims=True)
    n_ref[...] = (h * lax.rsqrt(ms + EPS) * gffn_ref[...]).astype(n_ref.dtype)


def _mix(y, z, attn, head, x2, nt, d_skip, g_ffn, w_glu, w_ao, w_out):
    rows, ds = y.shape
    d = x2.shape[1]
    tm = ROW_ALIGN
    u_col = 3 * D_ATTN // ds
    ga_col = (3 * D_ATTN + ds) // d
    assert (3 * D_ATTN) % ds == 0 and (3 * D_ATTN + ds) % d == 0
    return pl.pallas_call(
        functools.partial(_mix_kernel, nt=nt),
        grid=(rows // tm,),
        in_specs=[
            pl.BlockSpec((tm, ds), lambda i: (i, 0)),
            pl.BlockSpec((tm, ds), lambda i: (i, u_col)),
            pl.BlockSpec((tm, D_ATTN), lambda i: (i, 0)),
            pl.BlockSpec((tm, d), lambda i: (i, ga_col)),
            pl.BlockSpec((tm, d), lambda i: (i, ga_col + 1)),
            _resident((tm, d), lambda i: (0, 0)),
            pl.BlockSpec((tm, d), lambda i: (_seq_block(i, nt), 0)),
            _resident((1, ds), lambda i: (0, 0)),
            _resident((1, d), lambda i: (0, 0)),
            _resident(w_glu.shape, lambda i: (0, 0)),
            _resident(w_ao.shape, lambda i: (0, 0)),
            _resident(w_out.shape, lambda i: (0, 0)),
        ],
        out_specs=[
            pl.BlockSpec((tm, d), lambda i: (i, 0)),
            pl.BlockSpec((tm, d), lambda i: (i, 0)),
        ],
        out_shape=[
            jax.ShapeDtypeStruct((rows, d), F32),
            jax.ShapeDtypeStruct((rows, d), BF16),
        ],
        compiler_params=_params(("parallel",)),
        name="mix_out",
    )(y, z, attn, z, z, head, x2, d_skip, g_ffn, w_glu, w_ao, w_out)


def _ffn_up_kernel(n_ref, wg_ref, wu_ref, cw_ref, cb_ref, a_ref, tail, wg_scr, wu_scr):
    tm = n_ref.shape[0]

    @pl.when(pl.program_id(1) == 0)
    def _():
        tail[...] = jnp.zeros_like(tail)
        wg_scr[...] = wg_ref[...].astype(wg_scr.dtype)
        wu_scr[...] = wu_ref[...].astype(wu_scr.dtype)

    ts = _largest_tile(tm, (512, 256))
    prev1 = tail[CONV_WIDTH - 2:CONV_WIDTH - 1, :]
    prev2 = tail[CONV_WIDTH - 3:CONV_WIDTH - 2, :]
    for r0 in range(0, tm, ts):
        n = n_ref[r0:r0 + ts, :]
        g = jnp.dot(n, wg_scr[...], preferred_element_type=F32)
        u = jnp.dot(n, wu_scr[...], preferred_element_type=F32)
        row = lax.broadcasted_iota(jnp.int32, g.shape, 0)
        g1 = jnp.where(row == 0, prev1, pltpu.roll(g, 1, 0))
        g2 = jnp.where(row == 0, prev2, jnp.where(row == 1, prev1, pltpu.roll(g, 2, 0)))
        gc = cb_ref[...] + cw_ref[0:1, :] * g2 + cw_ref[1:2, :] * g1 + cw_ref[2:3, :] * g
        a_ref[r0:r0 + ts, :] = (gc * _sigmoid(gc) * u).astype(a_ref.dtype)
        prev1 = g[ts - 1:ts, :]
        prev2 = g[ts - 2:ts - 1, :]
    tail[CONV_WIDTH - 2:CONV_WIDTH - 1, :] = prev1
    tail[CONV_WIDTH - 3:CONV_WIDTH - 2, :] = prev2


def _largest_tile(n, candidates):
    return next(c for c in candidates if n % c == 0)


def _ffn_up(n2, w_up, conv_w, conv_b, tn=512):
    rows, d = n2.shape
    dff = conv_w.shape[1]
    nj = dff // tn
    tm = _largest_tile(rows, (1536, 1024, 768, 512, 256))
    assert dff % tn == 0
    return pl.pallas_call(
        _ffn_up_kernel,
        grid=(nj, rows // tm),
        in_specs=[
            pl.BlockSpec((tm, d), lambda j, i: (i, 0)),
            pl.BlockSpec((d, tn), lambda j, i: (0, j)),
            pl.BlockSpec((d, tn), lambda j, i: (0, nj + j)),
            pl.BlockSpec((CONV_WIDTH, tn), lambda j, i: (0, j)),
            pl.BlockSpec((1, tn), lambda j, i: (0, j)),
        ],
        out_specs=pl.BlockSpec((tm, tn), lambda j, i: (i, j)),
        out_shape=jax.ShapeDtypeStruct((rows, dff), BF16),
        scratch_shapes=[pltpu.VMEM((8, tn), F32), pltpu.VMEM((d, tn), BF16), pltpu.VMEM((d, tn), BF16)],
        compiler_params=_params(("parallel", "arbitrary")),
        name="ffn_up",
    )(n2, w_up, w_up, conv_w, conv_b)


def _ffn_down_kernel(a0_ref, a1_ref, w_ref, h0_ref, h1_ref, g_ref, o_ref):
    tm = a0_ref.shape[0]
    for s, (a_ref, h_ref) in enumerate(((a0_ref, h0_ref), (a1_ref, h1_ref))):
        h = h_ref[...] + jnp.dot(a_ref[...], w_ref[...], preferred_element_type=F32)
        ms = jnp.mean(h * h, axis=-1, keepdims=True)
        o_ref[s * tm:(s + 1) * tm, :] = (h * lax.rsqrt(ms + EPS) * g_ref[...]).astype(o_ref.dtype)


def _ffn_down(act, w_down, h1, g_final, bsz, lp):
    dff, d = w_down.shape
    tm = ROW_ALIGN
    nt_in = lp // tm
    nt_out = (nt_in - 1) // 2
    assert (nt_in - 1) % 2 == 0

    def rows(s):
        return lambda b, t: (b * nt_in + 1 + 2 * t + s, 0)

    return pl.pallas_call(
        _ffn_down_kernel,
        grid=(bsz, nt_out),
        in_specs=[
            pl.BlockSpec((tm, dff), rows(0)),
            pl.BlockSpec((tm, dff), rows(1)),
            _resident((dff, d), lambda b, t: (0, 0)),
            pl.BlockSpec((tm, d), rows(0)),
            pl.BlockSpec((tm, d), rows(1)),
            _resident((1, d), lambda b, t: (0, 0)),
        ],
        out_specs=pl.BlockSpec((2 * tm, d), lambda b, t: (b * nt_out + t, 0)),
        out_shape=jax.ShapeDtypeStruct((bsz * nt_out * 2 * tm, d), F32),
        compiler_params=_params(("parallel", "parallel")),
        name="ffn_down",
    )(act, act, w_down, h1, h1, g_final)


def _layer(head, x2, bsz, lp, g_mix, w_in, b_f, lam_re, lam_im, log_dt, b_re, b_im, c_re, c_im, d_skip,
           w_glu, w_attn_o, w_out, g_ffn, w_up, conv_w, conv_b, w_down):
    d = x2.shape[1]
    ds = d_skip.shape[0]
    nt = lp // ROW_ALIGN
    o_q, o_k, o_v, o_f, o_u, o_ga, o_gb = (0, D_ATTN, 2 * D_ATTN, 3 * D_ATTN, 3 * D_ATTN + HEADS,
                                            3 * D_ATTN + HEADS + ds, 3 * D_ATTN + HEADS + ds + d)
    w_main, w_f = _win_pack(w_in, o_f, o_u, HEAD_DIM ** -0.5 * LOG2E)
    bf = jnp.pad(b_f.astype(F32), (0, 128 - HEADS)).reshape(1, 128)

    z, u, qb, kb = _inproj(head, x2, g_mix.reshape(1, d), w_main, w_f, bf, bsz * lp, nt, 3 * D_ATTN, ds)
    attn = _flash(z, qb, kb, bsz, lp)
    y = _ssm(u, lam_re, lam_im, log_dt, b_re, b_im, c_re, c_im, bsz, lp)

    h1, n2 = _mix(y, z, attn, head, x2, nt, d_skip.reshape(1, ds).astype(F32), g_ffn.reshape(1, d).astype(F32),
                  w_glu.astype(BF16), w_attn_o.astype(BF16), w_out.astype(BF16))
    act = _ffn_up(n2, w_up.astype(F32), conv_w.astype(F32), conv_b.reshape(1, -1).astype(F32))
    return act, h1


def kernel(x, meta, g_mix, w_in, b_f, lam_re, lam_im, log_dt, b_re, b_im, c_re, c_im, d_skip,
           w_glu, w_attn_o, w_out, g_ffn, w_up, conv_w, conv_b, w_down, g_final):
    bsz, seq, d = x.shape
    depth = g_mix.shape[0]
    assert depth == 1 and meta.shape[0] == N_META and seq % ROW_ALIGN == 0
    lp = seq + ROW_ALIGN
    assert (bsz * lp) % 512 == 0
    head = jnp.concatenate([jnp.zeros((PAD, d), x.dtype), meta.astype(x.dtype)], axis=0)
    act, h1 = _layer(head, x.reshape(bsz * seq, d), bsz, lp, g_mix[0], w_in[0], b_f[0], lam_re[0], lam_im[0], log_dt[0],
                     b_re[0], b_im[0], c_re[0], c_im[0], d_skip[0], w_glu[0], w_attn_o[0], w_out[0],
                     g_ffn[0], w_up[0], conv_w[0], conv_b[0], w_down[0])
    out = _ffn_down(act, w_down[0].astype(BF16), h1, g_final.reshape(1, d).astype(F32), bsz, lp)
    return out.reshape(bsz, seq, d)
```

```python
import functools
import math

import jax
import jax.numpy as jnp
from jax import lax
from jax.experimental import pallas as pl
from jax.experimental.pallas import tpu as pltpu

N_META = 16
HEADS = 8
HEAD_DIM = 128
D_ATTN = HEADS * HEAD_DIM
SSM_GROUP = 16
SSM_STATE = 64
SSM_CHUNK = 12
SSM_SLAB = 8
CONV_WIDTH = 3
EPS = 1e-6

ROW_ALIGN = 256
PAD = ROW_ALIGN - N_META
ATT_TQ = 1024
ATT_TK = 512
QK_AUG = 2 * HEAD_DIM
LOG2E = math.log2(math.e)
MASK_BIG = 1e30
NEG = -3e38
VMEM_LIMIT = 56 * 1024 * 1024

F32 = jnp.float32
BF16 = jnp.bfloat16


def _sigmoid(x):
    return 1.0 / (1.0 + jnp.exp(-x))


def _gelu_tanh(x):
    c = math.sqrt(2.0 / math.pi)
    return 0.5 * x * (1.0 + jnp.tanh(c * (x + 0.044715 * (x * x * x))))


def _params(sem, limit=VMEM_LIMIT):
    return pltpu.CompilerParams(dimension_semantics=sem, vmem_limit_bytes=limit)


def _resident(shape, index_map):
    return pl.BlockSpec(shape, index_map, pipeline_mode=pl.Buffered(1))


def _win_pack_kernel(wt_ref, o_ref, f_ref, *, o_f, o_u, q_scale):
    n_in, tr = wt_ref.shape
    o_ref[:, 0:D_ATTN] = (wt_ref[0:D_ATTN, :] * q_scale).T.astype(o_ref.dtype)
    o_ref[:, D_ATTN:o_f] = wt_ref[D_ATTN:o_f, :].T.astype(o_ref.dtype)
    o_ref[:, o_f:o_f + (n_in - o_u)] = wt_ref[o_u:n_in, :].T.astype(o_ref.dtype)
    wf = jnp.concatenate([wt_ref[o_f:o_u, :], jnp.zeros((f_ref.shape[1] - (o_u - o_f), tr), F32)], axis=0)
    f_ref[...] = wf.T.astype(f_ref.dtype)


def _win_pack(w_in, o_f, o_u, q_scale, tr=256):
    d, n_in = w_in.shape
    n_out = n_in - (o_u - o_f)
    return pl.pallas_call(
        functools.partial(_win_pack_kernel, o_f=o_f, o_u=o_u, q_scale=q_scale),
        grid=(d // tr,),
        in_specs=[pl.BlockSpec((n_in, tr), lambda i: (0, i))],
        out_specs=[pl.BlockSpec((tr, n_out), lambda i: (i, 0)), pl.BlockSpec((tr, 128), lambda i: (i, 0))],
        out_shape=[jax.ShapeDtypeStruct((d, n_out), BF16), jax.ShapeDtypeStruct((d, 128), BF16)],
        compiler_params=_params(("parallel",)),
        name="win_pack",
    )(w_in.T)


def _seq_block(t, nt):
    return (t // nt) * (nt - 1) + jnp.maximum(t % nt - 1, 0)


def _padded_rows(t, nt, head_ref, x_ref):
    return jnp.where(t % nt == 0, head_ref[...], x_ref[...])


def _split3(x):
    hi = x.astype(BF16)
    r1 = x - hi.astype(F32)
    mid = r1.astype(BF16)
    lo = (r1 - mid.astype(F32)).astype(BF16)
    return hi, mid, lo


def _forget_bias(f, bf_ref, t, nt, carry, qb_ref, kb_ref, row0):
    tm = f.shape[0]
    first = t % nt == 0
    x = f + bf_ref[...]
    logf = jnp.minimum(x, 0.0) - jnp.log1p(jnp.exp(-jnp.abs(x)))
    row = lax.broadcasted_iota(jnp.int32, (tm, 1), 0)
    valid = jnp.logical_or(jnp.logical_not(first), row >= PAD)
    logf = jnp.where(valid, logf, 0.0)
    r = lax.broadcasted_iota(jnp.int32, (tm, tm), 0)
    c = lax.broadcasted_iota(jnp.int32, (tm, tm), 1)
    tri = (r >= c).astype(BF16)
    cs3 = jnp.dot(tri, jnp.concatenate(_split3(logf), axis=1), preferred_element_type=F32)
    w = f.shape[1]
    fcum = cs3[:, 0:w] + cs3[:, w:2 * w] + cs3[:, 2 * w:3 * w] + jnp.where(first, 0.0, carry[0:1, :])
    carry[0:1, :] = fcum[tm - 1:tm, :]

    lane = lax.broadcasted_iota(jnp.int32, (tm, HEAD_DIM), 1)
    one = jnp.ones((tm, HEAD_DIM), F32)
    zero = jnp.zeros((tm, HEAD_DIM), F32)
    for h in range(HEADS):
        hi, mid, lo = (p.astype(F32) for p in _split3(fcum[:, h:h + 1] * LOG2E))
        qb = jnp.where(lane == 0, hi, jnp.where(lane == 1, mid, jnp.where(lane == 2, lo,
             jnp.where(lane < 6, one, zero))))
        khi = jnp.where(valid, hi, MASK_BIG)
        kmid = jnp.where(valid, mid, 0.0)
        klo = jnp.where(valid, lo, 0.0)
        kb = jnp.where(lane < 3, one, jnp.where(lane == 3, -khi, jnp.where(lane == 4, -kmid,
             jnp.where(lane == 5, -klo, zero))))
        qb_ref[h, row0:row0 + tm, :] = qb.astype(qb_ref.dtype)
        kb_ref[h, row0:row0 + tm, :] = kb.astype(kb_ref.dtype)


def _inproj_kernel(head_ref, xa_ref, xb_ref, g_ref, w_ref, wf_ref, bf_ref, z_ref, u_ref, qb_ref, kb_ref,
                   n_scr, carry, *, nt, u_blk, u_lo):
    i = pl.program_id(0)
    j = pl.program_id(1)
    n_i = pl.num_programs(0)
    n_j = pl.num_programs(1)

    def prepare(t):
        x = jnp.concatenate([_padded_rows(2 * t, nt, head_ref, xa_ref),
                             _padded_rows(2 * t + 1, nt, head_ref, xb_ref)], axis=0)
        ms = jnp.mean(x * x, axis=-1, keepdims=True)
        n = (x * lax.rsqrt(ms + EPS) * g_ref[...]).astype(BF16)
        n_scr[t % 2] = n
        f = jnp.dot(n, wf_ref[...], preferred_element_type=F32)
        for blk in range(2):
            _forget_bias(f[blk * ROW_ALIGN:(blk + 1) * ROW_ALIGN], bf_ref, 2 * t + blk, nt, carry,
                         qb_ref, kb_ref, blk * ROW_ALIGN)

    def project():
        acc = jnp.dot(n_scr[i % 2], w_ref[...], preferred_element_type=F32)
        z_ref[...] = acc.astype(z_ref.dtype)
        return acc

    @pl.when(jnp.logical_and(i == 0, j == 0))
    def _():
        carry[...] = jnp.zeros_like(carry)
        prepare(0)

    lookahead = jnp.logical_and(j == n_j - 1, i + 1 < n_i)

    @pl.when(lookahead)
    def _():
        prepare(i + 1)
        project()

    @pl.when(jnp.logical_not(lookahead))
    def _():
        acc = project()

        @pl.when(j == u_blk)
        def _():
            u_ref[...] = acc[:, u_lo:u_lo + u_ref.shape[1]]


def _inproj(head, x2, g, w, wf, bf, rows, nt, u_off, u_width, tn=2048):
    d = x2.shape[1]
    tm = 2 * ROW_ALIGN
    n_out = w.shape[1]
    n_i, n_j = rows // tm, n_out // tn
    u_blk, u_lo = divmod(u_off, tn)
    assert u_lo + u_width <= tn and rows % tm == 0 and u_blk < n_j - 1

    def tile(i, j):
        return jnp.minimum(i + (j == n_j - 1).astype(jnp.int32), n_i - 1)

    bias_spec = pl.BlockSpec((HEADS, tm, HEAD_DIM), lambda i, j: (0, tile(i, j), 0))
    bias_shape = jax.ShapeDtypeStruct((HEADS, rows, HEAD_DIM), BF16)
    return pl.pallas_call(
        functools.partial(_inproj_kernel, nt=nt, u_blk=u_blk, u_lo=u_lo),
        grid=(n_i, n_j),
        in_specs=[
            _resident((ROW_ALIGN, d), lambda i, j: (0, 0)),
            pl.BlockSpec((ROW_ALIGN, d), lambda i, j: (_seq_block(2 * tile(i, j), nt), 0)),
            pl.BlockSpec((ROW_ALIGN, d), lambda i, j: (_seq_block(2 * tile(i, j) + 1, nt), 0)),
            pl.BlockSpec((1, d), lambda i, j: (0, 0)),
            pl.BlockSpec((d, tn), lambda i, j: (0, j)),
            pl.BlockSpec((d, 128), lambda i, j: (0, 0)),
            pl.BlockSpec((1, 128), lambda i, j: (0, 0)),
        ],
        out_specs=[
            pl.BlockSpec((tm, tn), lambda i, j: (i, j)),
            pl.BlockSpec((tm, u_width), lambda i, j: (i, 0)),
            bias_spec, bias_spec,
        ],
        out_shape=[
            jax.ShapeDtypeStruct((rows, n_out), BF16),
            jax.ShapeDtypeStruct((rows, u_width), F32),
            bias_shape, bias_shape,
        ],
        scratch_shapes=[pltpu.VMEM((2, tm, d), BF16), pltpu.VMEM((8, 128), F32)],
        compiler_params=_params(("arbitrary", "arbitrary")),
        name="inproj",
    )(head, x2, x2, g, w, wf, bf)


def _flash_kernel(q_ref, k_ref, v_ref, qb_ref, kb_ref, o_ref,
                  qa_ref, ka_ref, qt_ref, vt_ref, sa_ref, sb_ref, mxa_ref, mxb_ref, m_ref, acc_ref):
    lp = v_ref.shape[0]
    tq, tk = ATT_TQ, ATT_TK
    n_main = (lp - ROW_ALIGN) // tq
    n_chunks = (lp - ROW_ALIGN) // tk
    va = vt_ref.shape[1]

    qa_ref[:, 0:HEAD_DIM] = q_ref[...]
    qa_ref[:, HEAD_DIM:QK_AUG] = qb_ref[0]
    ka_ref[:, 0:HEAD_DIM] = k_ref[...]
    ka_ref[:, HEAD_DIM:QK_AUG] = kb_ref[0]

    ones = jnp.ones((va - HEAD_DIM, tk), vt_ref.dtype)
    vt_ref[0, 0:HEAD_DIM, 0:ROW_ALIGN] = v_ref[0:ROW_ALIGN, :].T
    vt_ref[0, HEAD_DIM:va, :] = ones

    def fill(c, _):
        vt_ref[1 + c, 0:HEAD_DIM, :] = v_ref[pl.ds(pl.multiple_of(ROW_ALIGN + c * tk, ROW_ALIGN), tk), :].T
        vt_ref[1 + c, HEAD_DIM:va, :] = ones
        return 0

    lax.fori_loop(0, n_chunks, fill, 0)

    buf_a = (sa_ref, mxa_ref)
    buf_b = (sb_ref, mxb_ref)

    def load_queries(slot, q_off, n):
        qt_ref[slot, :, 0:n] = qa_ref[pl.ds(q_off, n), :].T

    def scores(buf, slot, nq, k_off, nk, col0):
        s_ref, mx_ref = buf
        kc = ka_ref[pl.ds(k_off, nk), :]
        s = jnp.dot(kc, qt_ref[slot, :, col0:col0 + nq], preferred_element_type=F32)
        s_ref[0:nk, col0:col0 + nq] = s
        mx_ref[0:1, col0:col0 + nq] = jnp.max(s, axis=0, keepdims=True)

    def absorb(buf, slot, nk, lo, hi, mask_shift=None, lane0=0):
        s_ref, mx_ref = buf
        s = s_ref[0:nk, lo:hi]
        if mask_shift is None:
            smax = mx_ref[0:1, lo:hi]
        else:
            kr = lax.broadcasted_iota(jnp.int32, s.shape, 0) + mask_shift
            qc = lax.broadcasted_iota(jnp.int32, s.shape, 1) + lo
            s = jnp.where(kr > qc, NEG, s)
            smax = jnp.max(s, axis=0, keepdims=True)
        m = m_ref[0:1, lo:hi]
        m_new = jnp.maximum(m, smax)
        alpha = jnp.exp2(m - m_new)
        p = jnp.exp2(s - m_new).astype(BF16)
        m_ref[0:1, lo:hi] = m_new
        pv = jnp.dot(vt_ref[slot, :, lane0:lane0 + nk], p, preferred_element_type=F32)
        acc_ref[:, lo:hi] = alpha * acc_ref[:, lo:hi] + pv

    def reset(n):
        m_ref[0:1, 0:n] = jnp.full((1, n), NEG, F32)
        acc_ref[:, 0:n] = jnp.zeros((va, n), F32)

    def finish(q_off, n):
        out = acc_ref[0:HEAD_DIM, 0:n] / acc_ref[HEAD_DIM:HEAD_DIM + 1, 0:n]
        o_ref[pl.ds(q_off, n), :] = out.T.astype(o_ref.dtype)

    reset(ROW_ALIGN)
    load_queries(1, 0, ROW_ALIGN)
    scores(buf_a, 1, ROW_ALIGN, 0, ROW_ALIGN, 0)
    absorb(buf_a, 0, ROW_ALIGN, 0, ROW_ALIGN, mask_shift=0)
    finish(0, ROW_ALIGN)

    per_tile = tq // tk
    load_queries(0, ROW_ALIGN, tq)

    def tile(j, _):
        q_off = pl.multiple_of(ROW_ALIGN + j * tq, ROW_ALIGN)
        qs = j % 2
        reset(tq)
        scores(buf_a, qs, tq, PAD, N_META, 0)
        scores(buf_b, qs, tq, ROW_ALIGN, tk, 0)
        absorb(buf_a, 0, N_META, 0, tq, lane0=PAD)

        def pair(p):
            kb = pl.multiple_of(ROW_ALIGN + p * (2 * tk), ROW_ALIGN)
            scores(buf_a, qs, tq, kb + tk, tk, 0)
            absorb(buf_b, 1 + 2 * p, tk, 0, tq)
            scores(buf_b, qs, tq, kb + 2 * tk, tk, 0)
            absorb(buf_a, 2 + 2 * p, tk, 0, tq)

        def two_pairs(pp, _):
            pair(2 * pp)
            pair(2 * pp + 1)
            return 0

        n_pairs = j * (per_tile // 2)
        lax.fori_loop(0, n_pairs // 2, two_pairs, 0)

        @pl.when(n_pairs % 2 == 1)
        def _():
            pair(n_pairs - 1)
        scores(buf_a, qs, tq - tk, q_off + tk, tk, tk)
        load_queries(1 - qs, pl.multiple_of(jnp.minimum(q_off + tq, lp - tq), ROW_ALIGN), tq)
        absorb(buf_b, 1 + j * per_tile, tk, 0, tq, mask_shift=0)
        absorb(buf_a, 2 + j * per_tile, tk, tk, tq, mask_shift=tk)
        finish(q_off, tq)
        return 0

    lax.fori_loop(0, n_main, tile, 0)


def _flash(z, qb, kb, bsz, lp):
    assert (lp - ROW_ALIGN) % ATT_TQ == 0 and ATT_TQ == 2 * ATT_TK
    bias_spec = pl.BlockSpec((1, lp, HEAD_DIM), lambda b, h: (h, b, 0))
    va = HEAD_DIM + 16
    return pl.pallas_call(
        _flash_kernel,
        grid=(bsz, HEADS),
        in_specs=[
            pl.BlockSpec((lp, HEAD_DIM), lambda b, h: (b, h)),
            pl.BlockSpec((lp, HEAD_DIM), lambda b, h: (b, HEADS + h)),
            pl.BlockSpec((lp, HEAD_DIM), lambda b, h: (b, 2 * HEADS + h)),
            bias_spec, bias_spec,
        ],
        out_specs=pl.BlockSpec((lp, HEAD_DIM), lambda b, h: (b, h)),
        out_shape=jax.ShapeDtypeStruct((bsz * lp, D_ATTN), BF16),
        scratch_shapes=[
            pltpu.VMEM((lp, QK_AUG), BF16),
            pltpu.VMEM((lp, QK_AUG), BF16),
            pltpu.VMEM((2, QK_AUG, ATT_TQ), BF16),
            pltpu.VMEM((1 + (lp - ROW_ALIGN) // ATT_TK, va, ATT_TK), BF16),
            pltpu.VMEM((ATT_TK, ATT_TQ + 128), F32),
            pltpu.VMEM((ATT_TK, ATT_TQ + 128), F32),
            pltpu.VMEM((8, ATT_TQ), F32),
            pltpu.VMEM((8, ATT_TQ), F32),
            pltpu.VMEM((8, ATT_TQ), F32),
            pltpu.VMEM((va, ATT_TQ + 128), F32),
        ],
        compiler_params=_params(("parallel", "parallel")),
        name="flash",
    )(z, z, z, qb, kb)


def _tile_lanes(x, reps):
    return jnp.concatenate([x] * reps, axis=1)


def _ssm_kernel(u_ref, lr_ref, li_ref, dt_ref, lrl_ref, lil_ref, dtl_ref, btr_ref, bti_ref, cr_ref, ci_ref,
                w0_ref, w1_ref, w2_ref, y_ref, w0b_ref, w1b_ref, w2b_ref,
                lhs_scr, toep_scr, inj_scr, out_scr, s_scr, hp_scr):
    for src, dst in ((w0_ref, w0b_ref), (w1_ref, w1b_ref), (w2_ref, w2b_ref)):
        dst[...] = src[...].astype(dst.dtype)
    t = SSM_CHUNK
    lanes = SSM_SLAB * SSM_GROUP
    ns = SSM_SLAB * SSM_STATE
    nc = u_ref.shape[0] // t

    @pl.when(pl.program_id(1) == 0)
    def _build_weights():
        lr = lr_ref[0]
        li = li_ref[0]
        dt = jnp.exp(dt_ref[0])
        btr, bti = btr_ref[0], bti_ref[0]
        cr, ci = cr_ref[0], ci_ref[0]

        mag = jnp.exp(lr * dt)
        a_re = mag * jnp.cos(li * dt)
        a_im = mag * jnp.sin(li * dt)
        powers = [(jnp.ones_like(a_re), jnp.zeros_like(a_re))]
        for _ in range(t):
            pr, pi = powers[-1]
            powers.append((pr * a_re - pi * a_im, pr * a_im + pi * a_re))
        den = lr * lr + li * li
        nr = a_re - 1.0
        z_re = (nr * lr + a_im * li) / den
        z_im = (a_im * lr - nr * li) / den
        row_g = lax.broadcasted_iota(jnp.int32, (lanes, ns), 0) // SSM_GROUP
        col_g = lax.broadcasted_iota(jnp.int32, (lanes, ns), 1) // SSM_STATE
        same = row_g == col_g

        def spread(x):
            return jnp.where(same, _tile_lanes(x, SSM_SLAB), 0.0)

        caz_re, caz_im = [], []
        for d in range(t):
            pr, pi = powers[d]
            azr = pr * z_re - pi * z_im
            azi = pr * z_im + pi * z_re
            caz_re.append(cr * azr - ci * azi)
            caz_im.append(cr * azi + ci * azr)
            i = t - 1 - d
            sr = btr * azr - bti * azi
            si = btr * azi + bti * azr
            inj_scr[i * lanes:(i + 1) * lanes, 0:ns] = spread(sr).astype(BF16)
            inj_scr[i * lanes:(i + 1) * lanes, ns:2 * ns] = spread(si).astype(BF16)
            qr, qi = powers[d + 1]
            er = cr * qr - ci * qi
            ei = cr * qi + ci * qr
            out_scr[d * lanes:(d + 1) * lanes, 0:ns] = spread(er).astype(BF16)
            out_scr[d * lanes:(d + 1) * lanes, ns:2 * ns] = spread(-ei).astype(BF16)
        nt_dims = (((1,), (1,)), ((), ()))
        hp = lax.Precision.HIGHEST
        r0 = (lax.dot_general(btr, jnp.concatenate(caz_re, axis=0), nt_dims, precision=hp, preferred_element_type=F32)
              - lax.dot_general(bti, jnp.concatenate(caz_im, axis=0), nt_dims, precision=hp, preferred_element_type=F32))
        rg = lax.broadcasted_iota(jnp.int32, r0.shape, 0) // SSM_GROUP
        cg = (lax.broadcasted_iota(jnp.int32, r0.shape, 1) % lanes) // SSM_GROUP
        r0 = jnp.where(rg == cg, r0, 0.0).astype(BF16)
        toep_scr[...] = jnp.zeros_like(toep_scr)
        for i in range(t):
            toep_scr[i * lanes:(i + 1) * lanes, i * lanes:t * lanes] = r0[:, 0:(t - i) * lanes]

    for i in range(t):
        lhs_scr[:, i * lanes:(i + 1) * lanes] = u_ref[pl.ds(i, nc, stride=t), :].astype(BF16)

    s_scr[...] = jnp.dot(lhs_scr[...], inj_scr[...], preferred_element_type=F32)

    dtl = jnp.exp(dtl_ref[0])
    mag = jnp.exp(float(t) * (lrl_ref[0] * dtl))
    ang = float(t) * (lil_ref[0] * dtl)
    ar = mag * jnp.cos(ang)
    ai = mag * jnp.sin(ang)

    def step(n, carry):
        hr, hi = carry
        hp_scr[pl.ds(n, 1), 0:ns] = hr
        hp_scr[pl.ds(n, 1), ns:2 * ns] = hi
        sr = s_scr[pl.ds(n, 1), 0:ns]
        si = s_scr[pl.ds(n, 1), ns:2 * ns]
        return ar * hr - ai * hi + sr, ar * hi + ai * hr + si

    zero = jnp.zeros((1, ns), F32)
    lax.fori_loop(0, nc, step, (zero, zero), unroll=8)

    hprev = hp_scr[...].astype(BF16)
    nt_dims = (((1,), (1,)), ((), ()))
    pair = 2 * lanes
    for jj in range(t // 2):
        kdim = (jj + 1) * pair
        yj = (jnp.dot(lhs_scr[:, 0:kdim], toep_scr[0:kdim, jj * pair:(jj + 1) * pair], preferred_element_type=F32)
              + lax.dot_general(hprev, out_scr[jj * pair:(jj + 1) * pair, :], nt_dims, preferred_element_type=F32))
        y_ref[pl.ds(2 * jj, nc, stride=t), :] = yj[:, 0:lanes]
        y_ref[pl.ds(2 * jj + 1, nc, stride=t), :] = yj[:, lanes:pair]


def _ssm(u, lam_re, lam_im, log_dt, b_re, b_im, c_re, c_im, bsz, lp, to_cast):
    rows, ds = u.shape
    g, p = lam_re.shape
    t, c = SSM_CHUNK, SSM_GROUP
    lanes = SSM_SLAB * c
    ns = SSM_SLAB * p
    nslab = g // SSM_SLAB
    nc = lp // t
    assert g % SSM_SLAB == 0 and lp % t == 0 and lanes == 128
    rep = lambda x: jnp.repeat(x, c, axis=0).reshape(nslab, lanes, -1)
    lane = lambda x: x.reshape(nslab, 1, ns)
    args = (
        u,
        rep(lam_re), rep(lam_im), rep(log_dt.reshape(g, 1)),
        lane(lam_re), lane(lam_im), lane(jnp.repeat(log_dt, p)),
        jnp.swapaxes(b_re, 1, 2).reshape(nslab, lanes, p), jnp.swapaxes(b_im, 1, 2).reshape(nslab, lanes, p),
        c_re.reshape(nslab, lanes, p), c_im.reshape(nslab, lanes, p),
    )
    rows_spec = pl.BlockSpec((1, lanes, p), lambda s, b: (s, 0, 0))
    lane_spec = pl.BlockSpec((1, 1, ns), lambda s, b: (s, 0, 0))
    steps = nslab * bsz
    assert all(w.shape[0] % (16 * steps) == 0 for w in to_cast)
    cast_specs = [pl.BlockSpec((w.shape[0] // steps, w.shape[1]), lambda s, b: (s * bsz + b, 0)) for w in to_cast]
    y, *cast = pl.pallas_call(
        _ssm_kernel,
        grid=(nslab, bsz),
        in_specs=[
            pl.BlockSpec((lp, lanes), lambda s, b: (b, s)),
            rows_spec, rows_spec, pl.BlockSpec((1, lanes, 1), lambda s, b: (s, 0, 0)),
            lane_spec, lane_spec, lane_spec,
            rows_spec, rows_spec, rows_spec, rows_spec,
            *cast_specs,
        ],
        out_specs=[pl.BlockSpec((lp, lanes), lambda s, b: (b, s)), *cast_specs],
        out_shape=[jax.ShapeDtypeStruct((rows, ds), F32)] + [jax.ShapeDtypeStruct(w.shape, BF16) for w in to_cast],
        scratch_shapes=[
            pltpu.VMEM((nc, t * lanes), BF16),
            pltpu.VMEM((t * lanes, t * lanes), BF16),
            pltpu.VMEM((t * lanes, 2 * ns), BF16),
            pltpu.VMEM((t * lanes, 2 * ns), BF16),
            pltpu.VMEM((nc, 2 * ns), F32),
            pltpu.VMEM((nc, 2 * ns), F32),
        ],
        compiler_params=_params(("arbitrary", "arbitrary")),
        name="ssm",
    )(*args, *to_cast)
    return y, cast


def _mix_kernel(y_ref, u_ref, a_ref, ga_ref, gb_ref, head_ref, x_ref, dsk_ref, gffn_ref,
                wglu_ref, wao_ref, wout_ref, h_ref, n_ref, *, nt):
    d = x_ref.shape[1]
    ao = jnp.dot(a_ref[...], wao_ref[...], preferred_element_type=F32)
    y = y_ref[...].astype(F32) + dsk_ref[...] * u_ref[...].astype(F32)
    gy = _gelu_tanh(y).astype(BF16)
    yab = jnp.dot(gy, wglu_ref[...], preferred_element_type=F32)
    ssm_out = yab[:, 0:d] * _sigmoid(yab[:, d:2 * d])
    merged = (_sigmoid(ga_ref[...].astype(F32)) * ssm_out
              + _sigmoid(gb_ref[...].astype(F32)) * ao).astype(BF16)
    x = _padded_rows(pl.program_id(0), nt, head_ref, x_ref)
    h = x + jnp.dot(merged, wout_ref[...], preferred_element_type=F32)
    h_ref[...] = h
    ms = jnp.mean(h * h, axis=-1, keepdims=True)
    n_ref[...] = (h * lax.rsqrt(ms + EPS) * gffn_ref[...]).astype(n_ref.dtype)


def _mix(y, z, attn, head, x2, nt, d_skip, g_ffn, w_glu, w_ao, w_out):
    rows, ds = y.shape
    d = x2.shape[1]
    tm = ROW_ALIGN
    u_col = 3 * D_ATTN // ds
    ga_col = (3 * D_ATTN + ds) // d
    assert (3 * D_ATTN) % ds == 0 and (3 * D_ATTN + ds) % d == 0
    return pl.pallas_call(
        functools.partial(_mix_kernel, nt=nt),
        grid=(rows // tm,),
        in_specs=[
            pl.BlockSpec((tm, ds), lambda i: (i, 0)),
            pl.BlockSpec((tm, ds), lambda i: (i, u_col)),
            pl.BlockSpec((tm, D_ATTN), lambda i: (i, 0)),
            pl.BlockSpec((tm, d), lambda i: (i, ga_col)),
            pl.BlockSpec((tm, d), lambda i: (i, ga_col + 1)),
            _resident((tm, d), lambda i: (0, 0)),
            pl.BlockSpec((tm, d), lambda i: (_seq_block(i, nt), 0)),
            _resident((1, ds), lambda i: (0, 0)),
            _resident((1, d), lambda i: (0, 0)),
            _resident(w_glu.shape, lambda i: (0, 0)),
            _resident(w_ao.shape, lambda i: (0, 0)),
            _resident(w_out.shape, lambda i: (0, 0)),
        ],
        out_specs=[
            pl.BlockSpec((tm, d), lambda i: (i, 0)),
            pl.BlockSpec((tm, d), lambda i: (i, 0)),
        ],
        out_shape=[
            jax.ShapeDtypeStruct((rows, d), F32),
            jax.ShapeDtypeStruct((rows, d), BF16),
        ],
        compiler_params=_params(("parallel",)),
        name="mix_out",
    )(y, z, attn, z, z, head, x2, d_skip, g_ffn, w_glu, w_ao, w_out)


def _ffn_up_kernel(n_ref, wg_ref, wu_ref, cw_ref, cb_ref, wd_ref, a_ref, wdb_ref, tail, wg_scr, wu_scr):
    tm = n_ref.shape[0]

    @pl.when(pl.program_id(1) == 0)
    def _():
        tail[...] = jnp.zeros_like(tail)
        wg_scr[...] = wg_ref[...].astype(wg_scr.dtype)
        wu_scr[...] = wu_ref[...].astype(wu_scr.dtype)
        wdb_ref[...] = wd_ref[...].astype(wdb_ref.dtype)

    ts = _largest_tile(tm, (512, 256))
    prev1 = tail[CONV_WIDTH - 2:CONV_WIDTH - 1, :]
    prev2 = tail[CONV_WIDTH - 3:CONV_WIDTH - 2, :]
    for r0 in range(0, tm, ts):
        n = n_ref[r0:r0 + ts, :]
        g = jnp.dot(n, wg_scr[...], preferred_element_type=F32)
        u = jnp.dot(n, wu_scr[...], preferred_element_type=F32)
        row = lax.broadcasted_iota(jnp.int32, g.shape, 0)
        g1 = jnp.where(row == 0, prev1, pltpu.roll(g, 1, 0))
        g2 = jnp.where(row == 0, prev2, jnp.where(row == 1, prev1, pltpu.roll(g, 2, 0)))
        gc = cb_ref[...] + cw_ref[0:1, :] * g2 + cw_ref[1:2, :] * g1 + cw_ref[2:3, :] * g
        a_ref[r0:r0 + ts, :] = (gc * _sigmoid(gc) * u).astype(a_ref.dtype)
        prev1 = g[ts - 1:ts, :]
        prev2 = g[ts - 2:ts - 1, :]
    tail[CONV_WIDTH - 2:CONV_WIDTH - 1, :] = prev1
    tail[CONV_WIDTH - 3:CONV_WIDTH - 2, :] = prev2


def _largest_tile(n, candidates):
    return next(c for c in candidates if n % c == 0)


def _ffn_up(n2, w_up, conv_w, conv_b, w_down, tn=512):
    rows, d = n2.shape
    dff = conv_w.shape[1]
    nj = dff // tn
    tm = _largest_tile(rows, (1536, 1024, 768, 512, 256))
    assert dff % tn == 0
    return pl.pallas_call(
        _ffn_up_kernel,
        grid=(nj, rows // tm),
        in_specs=[
            pl.BlockSpec((tm, d), lambda j, i: (i, 0)),
            pl.BlockSpec((d, tn), lambda j, i: (0, j)),
            pl.BlockSpec((d, tn), lambda j, i: (0, nj + j)),
            pl.BlockSpec((CONV_WIDTH, tn), lambda j, i: (0, j)),
            pl.BlockSpec((1, tn), lambda j, i: (0, j)),
            pl.BlockSpec((tn, d), lambda j, i: (j, 0)),
        ],
        out_specs=[pl.BlockSpec((tm, tn), lambda j, i: (i, j)), pl.BlockSpec((tn, d), lambda j, i: (j, 0))],
        out_shape=[jax.ShapeDtypeStruct((rows, dff), BF16), jax.ShapeDtypeStruct((dff, d), BF16)],
        scratch_shapes=[pltpu.VMEM((8, tn), F32), pltpu.VMEM((d, tn), BF16), pltpu.VMEM((d, tn), BF16)],
        compiler_params=_params(("arbitrary", "arbitrary")),
        name="ffn_up",
    )(n2, w_up, w_up, conv_w, conv_b, w_down)


def _ffn_down_kernel(a0_ref, a1_ref, w_ref, h0_ref, h1_ref, g_ref, o_ref):
    tm = a0_ref.shape[0]
    for s, (a_ref, h_ref) in enumerate(((a0_ref, h0_ref), (a1_ref, h1_ref))):
        h = h_ref[...] + jnp.dot(a_ref[...], w_ref[...], preferred_element_type=F32)
        ms = jnp.mean(h * h, axis=-1, keepdims=True)
        o_ref[s * tm:(s + 1) * tm, :] = (h * lax.rsqrt(ms + EPS) * g_ref[...]).astype(o_ref.dtype)


def _ffn_down(act, w_down, h1, g_final, bsz, lp):
    dff, d = w_down.shape
    tm = ROW_ALIGN
    nt_in = lp // tm
    nt_out = (nt_in - 1) // 2
    assert (nt_in - 1) % 2 == 0

    def rows(s):
        return lambda b, t: (b * nt_in + 1 + 2 * t + s, 0)

    return pl.pallas_call(
        _ffn_down_kernel,
        grid=(bsz, nt_out),
        in_specs=[
            pl.BlockSpec((tm, dff), rows(0)),
            pl.BlockSpec((tm, dff), rows(1)),
            _resident((dff, d), lambda b, t: (0, 0)),
            pl.BlockSpec((tm, d), rows(0)),
            pl.BlockSpec((tm, d), rows(1)),
            _resident((1, d), lambda b, t: (0, 0)),
        ],
        out_specs=pl.BlockSpec((2 * tm, d), lambda b, t: (b * nt_out + t, 0)),
        out_shape=jax.ShapeDtypeStruct((bsz * nt_out * 2 * tm, d), F32),
        compiler_params=_params(("parallel", "parallel")),
        name="ffn_down",
    )(act, act, w_down, h1, h1, g_final)


def _layer(head, x2, bsz, lp, g_mix, w_in, b_f, lam_re, lam_im, log_dt, b_re, b_im, c_re, c_im, d_skip,
           w_glu, w_attn_o, w_out, g_ffn, w_up, conv_w, conv_b, w_down):
    d = x2.shape[1]
    ds = d_skip.shape[0]
    nt = lp // ROW_ALIGN
    o_q, o_k, o_v, o_f, o_u, o_ga, o_gb = (0, D_ATTN, 2 * D_ATTN, 3 * D_ATTN, 3 * D_ATTN + HEADS,
                                            3 * D_ATTN + HEADS + ds, 3 * D_ATTN + HEADS + ds + d)
    w_main, w_f = _win_pack(w_in, o_f, o_u, HEAD_DIM ** -0.5 * LOG2E)
    bf = jnp.pad(b_f.astype(F32), (0, 128 - HEADS)).reshape(1, 128)

    z, u, qb, kb = _inproj(head, x2, g_mix.reshape(1, d), w_main, w_f, bf, bsz * lp, nt, 3 * D_ATTN, ds)
    attn = _flash(z, qb, kb, bsz, lp)
    y, (w_glu_bf, w_ao_bf, w_out_bf) = _ssm(u, lam_re, lam_im, log_dt, b_re, b_im, c_re, c_im, bsz, lp,
                                            (w_glu.astype(F32), w_attn_o.astype(F32), w_out.astype(F32)))

    h1, n2 = _mix(y, z, attn, head, x2, nt, d_skip.reshape(1, ds).astype(F32), g_ffn.reshape(1, d).astype(F32),
                  w_glu_bf, w_ao_bf, w_out_bf)
    act, w_down_bf = _ffn_up(n2, w_up.astype(F32), conv_w.astype(F32), conv_b.reshape(1, -1).astype(F32),
                             w_down.astype(F32))
    return act, h1, w_down_bf


def kernel(x, meta, g_mix, w_in, b_f, lam_re, lam_im, log_dt, b_re, b_im, c_re, c_im, d_skip,
           w_glu, w_attn_o, w_out, g_ffn, w_up, conv_w, conv_b, w_down, g_final):
    bsz, seq, d = x.shape
    depth = g_mix.shape[0]
    assert depth == 1 and meta.shape[0] == N_META and seq % ROW_ALIGN == 0
    lp = seq + ROW_ALIGN
    assert (bsz * lp) % 512 == 0
    head = jnp.concatenate([jnp.zeros((PAD, d), x.dtype), meta.astype(x.dtype)], axis=0)
    act, h1, w_down_bf = _layer(head, x.reshape(bsz * seq, d), bsz, lp, g_mix[0], w_in[0], b_f[0], lam_re[0], lam_im[0], log_dt[0],
                     b_re[0], b_im[0], c_re[0], c_im[0], d_skip[0], w_glu[0], w_attn_o[0], w_out[0],
                     g_ffn[0], w_up[0], conv_w[0], conv_b[0], w_down[0])
    out = _ffn_down(act, w_down_bf, h1, g_final.reshape(1, d).astype(F32), bsz, lp)
    return out.reshape(bsz, seq, d)
```

```python
import functools
import math

import jax
import jax.numpy as jnp
from jax import lax
from jax.experimental import pallas as pl
from jax.experimental.pallas import tpu as pltpu

N_META = 16
HEADS = 8
HEAD_DIM = 128
D_ATTN = HEADS * HEAD_DIM
SSM_GROUP = 16
SSM_STATE = 64
SSM_CHUNK = 12
SSM_SLAB = 8
CONV_WIDTH = 3
EPS = 1e-6

ROW_ALIGN = 256
PAD = ROW_ALIGN - N_META
ATT_TQ = 1024
ATT_TK = 512
QK_AUG = 2 * HEAD_DIM
LOG2E = math.log2(math.e)
MASK_BIG = 1e30
NEG = -3e38
VMEM_LIMIT = 56 * 1024 * 1024

F32 = jnp.float32
BF16 = jnp.bfloat16


def _sigmoid(x):
    return 1.0 / (1.0 + jnp.exp(-x))


def _gelu_tanh(x):
    c = math.sqrt(2.0 / math.pi)
    return 0.5 * x * (1.0 + jnp.tanh(c * (x + 0.044715 * (x * x * x))))


def _params(sem, limit=VMEM_LIMIT):
    return pltpu.CompilerParams(dimension_semantics=sem, vmem_limit_bytes=limit)


def _resident(shape, index_map):
    return pl.BlockSpec(shape, index_map, pipeline_mode=pl.Buffered(1))


def _win_pack_kernel(wt_ref, o_ref, f_ref, *, o_f, o_u, q_scale):
    n_in, tr = wt_ref.shape
    o_ref[:, 0:D_ATTN] = (wt_ref[0:D_ATTN, :] * q_scale).T.astype(o_ref.dtype)
    o_ref[:, D_ATTN:o_f] = wt_ref[D_ATTN:o_f, :].T.astype(o_ref.dtype)
    o_ref[:, o_f:o_f + (n_in - o_u)] = wt_ref[o_u:n_in, :].T.astype(o_ref.dtype)
    wf = jnp.concatenate([wt_ref[o_f:o_u, :], jnp.zeros((f_ref.shape[1] - (o_u - o_f), tr), F32)], axis=0)
    f_ref[...] = wf.T.astype(f_ref.dtype)


def _win_pack(w_in, o_f, o_u, q_scale, tr=256):
    d, n_in = w_in.shape
    n_out = n_in - (o_u - o_f)
    return pl.pallas_call(
        functools.partial(_win_pack_kernel, o_f=o_f, o_u=o_u, q_scale=q_scale),
        grid=(d // tr,),
        in_specs=[pl.BlockSpec((n_in, tr), lambda i: (0, i))],
        out_specs=[pl.BlockSpec((tr, n_out), lambda i: (i, 0)), pl.BlockSpec((tr, 128), lambda i: (i, 0))],
        out_shape=[jax.ShapeDtypeStruct((d, n_out), BF16), jax.ShapeDtypeStruct((d, 128), BF16)],
        compiler_params=_params(("parallel",)),
        name="win_pack",
    )(w_in.T)


def _seq_block(t, nt):
    return (t // nt) * (nt - 1) + jnp.maximum(t % nt - 1, 0)


def _padded_rows(t, nt, head_ref, x_ref):
    return jnp.where(t % nt == 0, head_ref[...], x_ref[...])


def _split3(x):
    hi = x.astype(BF16)
    r1 = x - hi.astype(F32)
    mid = r1.astype(BF16)
    lo = (r1 - mid.astype(F32)).astype(BF16)
    return hi, mid, lo


def _forget_bias(f, bf_ref, t, nt, carry, qb_ref, kb_ref, row0):
    tm = f.shape[0]
    first = t % nt == 0
    x = f + bf_ref[...]
    logf = jnp.minimum(x, 0.0) - jnp.log1p(jnp.exp(-jnp.abs(x)))
    row = lax.broadcasted_iota(jnp.int32, (tm, 1), 0)
    valid = jnp.logical_or(jnp.logical_not(first), row >= PAD)
    logf = jnp.where(valid, logf, 0.0)
    r = lax.broadcasted_iota(jnp.int32, (tm, tm), 0)
    c = lax.broadcasted_iota(jnp.int32, (tm, tm), 1)
    tri = (r >= c).astype(BF16)
    cs3 = jnp.dot(tri, jnp.concatenate(_split3(logf), axis=1), preferred_element_type=F32)
    w = f.shape[1]
    fcum = cs3[:, 0:w] + cs3[:, w:2 * w] + cs3[:, 2 * w:3 * w] + jnp.where(first, 0.0, carry[0:1, :])
    carry[0:1, :] = fcum[tm - 1:tm, :]

    lane = lax.broadcasted_iota(jnp.int32, (tm, HEAD_DIM), 1)
    one = jnp.ones((tm, HEAD_DIM), F32)
    zero = jnp.zeros((tm, HEAD_DIM), F32)
    for h in range(HEADS):
        hi, mid, lo = (p.astype(F32) for p in _split3(fcum[:, h:h + 1] * LOG2E))
        qb = jnp.where(lane == 0, hi, jnp.where(lane == 1, mid, jnp.where(lane == 2, lo,
             jnp.where(lane < 6, one, zero))))
        khi = jnp.where(valid, hi, MASK_BIG)
        kmid = jnp.where(valid, mid, 0.0)
        klo = jnp.where(valid, lo, 0.0)
        kb = jnp.where(lane < 3, one, jnp.where(lane == 3, -khi, jnp.where(lane == 4, -kmid,
             jnp.where(lane == 5, -klo, zero))))
        qb_ref[h, row0:row0 + tm, :] = qb.astype(qb_ref.dtype)
        kb_ref[h, row0:row0 + tm, :] = kb.astype(kb_ref.dtype)


def _inproj_kernel(head_ref, xa_ref, xb_ref, g_ref, w_ref, wf_ref, bf_ref, z_ref, u_ref, qb_ref, kb_ref,
                   n_scr, carry, *, nt, u_blk, u_lo):
    i = pl.program_id(0)
    j = pl.program_id(1)
    n_i = pl.num_programs(0)
    n_j = pl.num_programs(1)

    def prepare(t):
        x = jnp.concatenate([_padded_rows(2 * t, nt, head_ref, xa_ref),
                             _padded_rows(2 * t + 1, nt, head_ref, xb_ref)], axis=0)
        ms = jnp.mean(x * x, axis=-1, keepdims=True)
        n = (x * lax.rsqrt(ms + EPS) * g_ref[...]).astype(BF16)
        n_scr[t % 2] = n
        f = jnp.dot(n, wf_ref[...], preferred_element_type=F32)
        for blk in range(2):
            _forget_bias(f[blk * ROW_ALIGN:(blk + 1) * ROW_ALIGN], bf_ref, 2 * t + blk, nt, carry,
                         qb_ref, kb_ref, blk * ROW_ALIGN)

    def project():
        acc = jnp.dot(n_scr[i % 2], w_ref[...], preferred_element_type=F32)
        z_ref[...] = acc.astype(z_ref.dtype)
        return acc

    @pl.when(jnp.logical_and(i == 0, j == 0))
    def _():
        carry[...] = jnp.zeros_like(carry)
        prepare(0)

    lookahead = jnp.logical_and(j == n_j - 1, i + 1 < n_i)

    @pl.when(lookahead)
    def _():
        prepare(i + 1)
        project()

    @pl.when(jnp.logical_not(lookahead))
    def _():
        acc = project()

        @pl.when(j == u_blk)
        def _():
            u_ref[...] = acc[:, u_lo:u_lo + u_ref.shape[1]]


def _inproj(head, x2, g, w, wf, bf, rows, nt, u_off, u_width, tn=2048):
    d = x2.shape[1]
    tm = 2 * ROW_ALIGN
    n_out = w.shape[1]
    n_i, n_j = rows // tm, n_out // tn
    u_blk, u_lo = divmod(u_off, tn)
    assert u_lo + u_width <= tn and rows % tm == 0 and u_blk < n_j - 1

    def tile(i, j):
        return jnp.minimum(i + (j == n_j - 1).astype(jnp.int32), n_i - 1)

    bias_spec = pl.BlockSpec((HEADS, tm, HEAD_DIM), lambda i, j: (0, tile(i, j), 0))
    bias_shape = jax.ShapeDtypeStruct((HEADS, rows, HEAD_DIM), BF16)
    return pl.pallas_call(
        functools.partial(_inproj_kernel, nt=nt, u_blk=u_blk, u_lo=u_lo),
        grid=(n_i, n_j),
        in_specs=[
            _resident((ROW_ALIGN, d), lambda i, j: (0, 0)),
            pl.BlockSpec((ROW_ALIGN, d), lambda i, j: (_seq_block(2 * tile(i, j), nt), 0)),
            pl.BlockSpec((ROW_ALIGN, d), lambda i, j: (_seq_block(2 * tile(i, j) + 1, nt), 0)),
            pl.BlockSpec((1, d), lambda i, j: (0, 0)),
            pl.BlockSpec((d, tn), lambda i, j: (0, j)),
            pl.BlockSpec((d, 128), lambda i, j: (0, 0)),
            pl.BlockSpec((1, 128), lambda i, j: (0, 0)),
        ],
        out_specs=[
            pl.BlockSpec((tm, tn), lambda i, j: (i, j)),
            pl.BlockSpec((tm, u_width), lambda i, j: (i, 0)),
            bias_spec, bias_spec,
        ],
        out_shape=[
            jax.ShapeDtypeStruct((rows, n_out), BF16),
            jax.ShapeDtypeStruct((rows, u_width), F32),
            bias_shape, bias_shape,
        ],
        scratch_shapes=[pltpu.VMEM((2, tm, d), BF16), pltpu.VMEM((8, 128), F32)],
        compiler_params=_params(("arbitrary", "arbitrary")),
        name="inproj",
    )(head, x2, x2, g, w, wf, bf)


def _flash_kernel(q_ref, k_ref, v_ref, qb_ref, kb_ref, o_ref,
                  qa_ref, ka_ref, qt_ref, vt_ref, sa_ref, sb_ref, mxa_ref, mxb_ref, m_ref, acc_ref):
    lp = v_ref.shape[0]
    tq, tk = ATT_TQ, ATT_TK
    n_main = (lp - ROW_ALIGN) // tq
    n_chunks = (lp - ROW_ALIGN) // tk
    va = vt_ref.shape[1]

    def augment(r0, n):
        rows = pl.ds(r0, n)
        qa_ref[rows, 0:HEAD_DIM] = q_ref[rows, :]
        qa_ref[rows, HEAD_DIM:QK_AUG] = qb_ref[0, rows, :]
        ka_ref[rows, 0:HEAD_DIM] = k_ref[rows, :]
        ka_ref[rows, HEAD_DIM:QK_AUG] = kb_ref[0, rows, :]

    augment(0, ROW_ALIGN + tq)

    ones = jnp.ones((va - HEAD_DIM, tk), vt_ref.dtype)
    vt_ref[0, 0:HEAD_DIM, 0:ROW_ALIGN] = v_ref[0:ROW_ALIGN, :].T
    vt_ref[0, HEAD_DIM:va, :] = ones

    def fill(c):
        vt_ref[1 + c, 0:HEAD_DIM, :] = v_ref[pl.ds(pl.multiple_of(ROW_ALIGN + c * tk, ROW_ALIGN), tk), :].T
        vt_ref[1 + c, HEAD_DIM:va, :] = ones

    buf_a = (sa_ref, mxa_ref)
    buf_b = (sb_ref, mxb_ref)

    def load_queries(slot, q_off, n):
        qt_ref[slot, :, 0:n] = qa_ref[pl.ds(q_off, n), :].T

    def scores(buf, slot, nq, k_off, nk, col0):
        s_ref, mx_ref = buf
        kc = ka_ref[pl.ds(k_off, nk), :]
        s = jnp.dot(kc, qt_ref[slot, :, col0:col0 + nq], preferred_element_type=F32)
        s_ref[0:nk, col0:col0 + nq] = s
        mx_ref[0:1, col0:col0 + nq] = jnp.max(s, axis=0, keepdims=True)

    def absorb(buf, slot, nk, lo, hi, mask_shift=None, lane0=0):
        s_ref, mx_ref = buf
        s = s_ref[0:nk, lo:hi]
        if mask_shift is None:
            smax = mx_ref[0:1, lo:hi]
        else:
            kr = lax.broadcasted_iota(jnp.int32, s.shape, 0) + mask_shift
            qc = lax.broadcasted_iota(jnp.int32, s.shape, 1) + lo
            s = jnp.where(kr > qc, NEG, s)
            smax = jnp.max(s, axis=0, keepdims=True)
        m = m_ref[0:1, lo:hi]
        m_new = jnp.maximum(m, smax)
        alpha = jnp.exp2(m - m_new)
        p = jnp.exp2(s - m_new).astype(BF16)
        m_ref[0:1, lo:hi] = m_new
        pv = jnp.dot(vt_ref[slot, :, lane0:lane0 + nk], p, preferred_element_type=F32)
        acc_ref[:, lo:hi] = alpha * acc_ref[:, lo:hi] + pv

    def reset(n):
        m_ref[0:1, 0:n] = jnp.full((1, n), NEG, F32)
        acc_ref[:, 0:n] = jnp.zeros((va, n), F32)

    def finish(q_off, n):
        out = acc_ref[0:HEAD_DIM, 0:n] / acc_ref[HEAD_DIM:HEAD_DIM + 1, 0:n]
        o_ref[pl.ds(q_off, n), :] = out.T.astype(o_ref.dtype)

    reset(ROW_ALIGN)
    load_queries(1, 0, ROW_ALIGN)
    scores(buf_a, 1, ROW_ALIGN, 0, ROW_ALIGN, 0)
    absorb(buf_a, 0, ROW_ALIGN, 0, ROW_ALIGN, mask_shift=0)
    finish(0, ROW_ALIGN)

    per_tile = tq // tk
    load_queries(0, ROW_ALIGN, tq)

    def tile(j, _):
        q_off = pl.multiple_of(ROW_ALIGN + j * tq, ROW_ALIGN)
        qs = j % 2
        reset(tq)
        for c in range(per_tile):
            fill(j * per_tile + c)
        scores(buf_a, qs, tq, PAD, N_META, 0)
        scores(buf_b, qs, tq, ROW_ALIGN, tk, 0)
        absorb(buf_a, 0, N_META, 0, tq, lane0=PAD)

        def pair(p):
            kb = pl.multiple_of(ROW_ALIGN + p * (2 * tk), ROW_ALIGN)
            scores(buf_a, qs, tq, kb + tk, tk, 0)
            absorb(buf_b, 1 + 2 * p, tk, 0, tq)
            scores(buf_b, qs, tq, kb + 2 * tk, tk, 0)
            absorb(buf_a, 2 + 2 * p, tk, 0, tq)

        def two_pairs(pp, _):
            pair(2 * pp)
            pair(2 * pp + 1)
            return 0

        n_pairs = j * (per_tile // 2)
        lax.fori_loop(0, n_pairs // 2, two_pairs, 0)

        @pl.when(n_pairs % 2 == 1)
        def _():
            pair(n_pairs - 1)
        scores(buf_a, qs, tq - tk, q_off + tk, tk, tk)
        q_next = pl.multiple_of(jnp.minimum(q_off + tq, lp - tq), ROW_ALIGN)
        augment(q_next, tq)
        load_queries(1 - qs, q_next, tq)
        absorb(buf_b, 1 + j * per_tile, tk, 0, tq, mask_shift=0)
        absorb(buf_a, 2 + j * per_tile, tk, tk, tq, mask_shift=tk)
        finish(q_off, tq)
        return 0

    lax.fori_loop(0, n_main, tile, 0)


def _flash(z, qb, kb, bsz, lp):
    assert (lp - ROW_ALIGN) % ATT_TQ == 0 and ATT_TQ == 2 * ATT_TK
    bias_spec = pl.BlockSpec((1, lp, HEAD_DIM), lambda b, h: (h, b, 0))
    va = HEAD_DIM + 16
    return pl.pallas_call(
        _flash_kernel,
        grid=(bsz, HEADS),
        in_specs=[
            pl.BlockSpec((lp, HEAD_DIM), lambda b, h: (b, h)),
            pl.BlockSpec((lp, HEAD_DIM), lambda b, h: (b, HEADS + h)),
            pl.BlockSpec((lp, HEAD_DIM), lambda b, h: (b, 2 * HEADS + h)),
            bias_spec, bias_spec,
        ],
        out_specs=pl.BlockSpec((lp, HEAD_DIM), lambda b, h: (b, h)),
        out_shape=jax.ShapeDtypeStruct((bsz * lp, D_ATTN), BF16),
        scratch_shapes=[
            pltpu.VMEM((lp, QK_AUG), BF16),
            pltpu.VMEM((lp, QK_AUG), BF16),
            pltpu.VMEM((2, QK_AUG, ATT_TQ), BF16),
            pltpu.VMEM((1 + (lp - ROW_ALIGN) // ATT_TK, va, ATT_TK), BF16),
            pltpu.VMEM((ATT_TK, ATT_TQ + 128), F32),
            pltpu.VMEM((ATT_TK, ATT_TQ + 128), F32),
            pltpu.VMEM((8, ATT_TQ), F32),
            pltpu.VMEM((8, ATT_TQ), F32),
            pltpu.VMEM((8, ATT_TQ), F32),
            pltpu.VMEM((va, ATT_TQ + 128), F32),
        ],
        compiler_params=_params(("parallel", "parallel")),
        name="flash",
    )(z, z, z, qb, kb)


def _tile_lanes(x, reps):
    return jnp.concatenate([x] * reps, axis=1)


def _ssm_kernel(u_ref, lr_ref, li_ref, dt_ref, lrl_ref, lil_ref, dtl_ref, btr_ref, bti_ref, cr_ref, ci_ref,
                w0_ref, w1_ref, w2_ref, y_ref, w0b_ref, w1b_ref, w2b_ref,
                lhs_scr, toep_scr, inj_scr, out_scr, s_scr, hp_scr):
    for src, dst in ((w0_ref, w0b_ref), (w1_ref, w1b_ref), (w2_ref, w2b_ref)):
        dst[...] = src[...].astype(dst.dtype)
    t = SSM_CHUNK
    lanes = SSM_SLAB * SSM_GROUP
    ns = SSM_SLAB * SSM_STATE
    nc = u_ref.shape[0] // t

    @pl.when(pl.program_id(1) == 0)
    def _build_weights():
        lr = lr_ref[0]
        li = li_ref[0]
        dt = jnp.exp(dt_ref[0])
        btr, bti = btr_ref[0], bti_ref[0]
        cr, ci = cr_ref[0], ci_ref[0]

        mag = jnp.exp(lr * dt)
        a_re = mag * jnp.cos(li * dt)
        a_im = mag * jnp.sin(li * dt)
        powers = [(jnp.ones_like(a_re), jnp.zeros_like(a_re))]
        for _ in range(t):
            pr, pi = powers[-1]
            powers.append((pr * a_re - pi * a_im, pr * a_im + pi * a_re))
        den = lr * lr + li * li
        nr = a_re - 1.0
        z_re = (nr * lr + a_im * li) / den
        z_im = (a_im * lr - nr * li) / den
        row_g = lax.broadcasted_iota(jnp.int32, (lanes, ns), 0) // SSM_GROUP
        col_g = lax.broadcasted_iota(jnp.int32, (lanes, ns), 1) // SSM_STATE
        same = row_g == col_g

        def spread(x):
            return jnp.where(same, _tile_lanes(x, SSM_SLAB), 0.0)

        caz_re, caz_im = [], []
        for d in range(t):
            pr, pi = powers[d]
            azr = pr * z_re - pi * z_im
            azi = pr * z_im + pi * z_re
            caz_re.append(cr * azr - ci * azi)
            caz_im.append(cr * azi + ci * azr)
            i = t - 1 - d
            sr = btr * azr - bti * azi
            si = btr * azi + bti * azr
            inj_scr[i * lanes:(i + 1) * lanes, 0:ns] = spread(sr).astype(BF16)
            inj_scr[i * lanes:(i + 1) * lanes, ns:2 * ns] = spread(si).astype(BF16)
            qr, qi = powers[d + 1]
            er = cr * qr - ci * qi
            ei = cr * qi + ci * qr
            out_scr[d * lanes:(d + 1) * lanes, 0:ns] = spread(er).astype(BF16)
            out_scr[d * lanes:(d + 1) * lanes, ns:2 * ns] = spread(-ei).astype(BF16)
        nt_dims = (((1,), (1,)), ((), ()))
        hp = lax.Precision.HIGHEST
        r0 = (lax.dot_general(btr, jnp.concatenate(caz_re, axis=0), nt_dims, precision=hp, preferred_element_type=F32)
              - lax.dot_general(bti, jnp.concatenate(caz_im, axis=0), nt_dims, precision=hp, preferred_element_type=F32))
        rg = lax.broadcasted_iota(jnp.int32, r0.shape, 0) // SSM_GROUP
        cg = (lax.broadcasted_iota(jnp.int32, r0.shape, 1) % lanes) // SSM_GROUP
        r0 = jnp.where(rg == cg, r0, 0.0).astype(BF16)
        toep_scr[...] = jnp.zeros_like(toep_scr)
        for i in range(t):
            toep_scr[i * lanes:(i + 1) * lanes, i * lanes:t * lanes] = r0[:, 0:(t - i) * lanes]

    for i in range(t):
        lhs_scr[:, i * lanes:(i + 1) * lanes] = u_ref[pl.ds(i, nc, stride=t), :].astype(BF16)

    s_scr[...] = jnp.dot(lhs_scr[...], inj_scr[...], preferred_element_type=F32)

    dtl = jnp.exp(dtl_ref[0])
    mag = jnp.exp(float(t) * (lrl_ref[0] * dtl))
    ang = float(t) * (lil_ref[0] * dtl)
    ar = mag * jnp.cos(ang)
    ai = mag * jnp.sin(ang)

    def step(n, carry):
        hr, hi = carry
        hp_scr[pl.ds(n, 1), 0:ns] = hr
        hp_scr[pl.ds(n, 1), ns:2 * ns] = hi
        sr = s_scr[pl.ds(n, 1), 0:ns]
        si = s_scr[pl.ds(n, 1), ns:2 * ns]
        return ar * hr - ai * hi + sr, ar * hi + ai * hr + si

    zero = jnp.zeros((1, ns), F32)
    lax.fori_loop(0, nc, step, (zero, zero), unroll=8)

    hprev = hp_scr[...].astype(BF16)
    nt_dims = (((1,), (1,)), ((), ()))
    pair = 2 * lanes
    for jj in range(t // 2):
        kdim = (jj + 1) * pair
        yj = (jnp.dot(lhs_scr[:, 0:kdim], toep_scr[0:kdim, jj * pair:(jj + 1) * pair], preferred_element_type=F32)
              + lax.dot_general(hprev, out_scr[jj * pair:(jj + 1) * pair, :], nt_dims, preferred_element_type=F32))
        y_ref[pl.ds(2 * jj, nc, stride=t), :] = yj[:, 0:lanes]
        y_ref[pl.ds(2 * jj + 1, nc, stride=t), :] = yj[:, lanes:pair]


def _ssm(u, lam_re, lam_im, log_dt, b_re, b_im, c_re, c_im, bsz, lp, to_cast):
    rows, ds = u.shape
    g, p = lam_re.shape
    t, c = SSM_CHUNK, SSM_GROUP
    lanes = SSM_SLAB * c
    ns = SSM_SLAB * p
    nslab = g // SSM_SLAB
    nc = lp // t
    assert g % SSM_SLAB == 0 and lp % t == 0 and lanes == 128
    rep = lambda x: jnp.repeat(x, c, axis=0).reshape(nslab, lanes, -1)
    lane = lambda x: x.reshape(nslab, 1, ns)
    args = (
        u,
        rep(lam_re), rep(lam_im), rep(log_dt.reshape(g, 1)),
        lane(lam_re), lane(lam_im), lane(jnp.repeat(log_dt, p)),
        jnp.swapaxes(b_re, 1, 2).reshape(nslab, lanes, p), jnp.swapaxes(b_im, 1, 2).reshape(nslab, lanes, p),
        c_re.reshape(nslab, lanes, p), c_im.reshape(nslab, lanes, p),
    )
    rows_spec = pl.BlockSpec((1, lanes, p), lambda s, b: (s, 0, 0))
    lane_spec = pl.BlockSpec((1, 1, ns), lambda s, b: (s, 0, 0))
    steps = nslab * bsz
    assert all(w.shape[0] % (16 * steps) == 0 for w in to_cast)
    cast_specs = [pl.BlockSpec((w.shape[0] // steps, w.shape[1]), lambda s, b: (s * bsz + b, 0)) for w in to_cast]
    y, *cast = pl.pallas_call(
        _ssm_kernel,
        grid=(nslab, bsz),
        in_specs=[
            pl.BlockSpec((lp, lanes), lambda s, b: (b, s)),
            rows_spec, rows_spec, pl.BlockSpec((1, lanes, 1), lambda s, b: (s, 0, 0)),
            lane_spec, lane_spec, lane_spec,
            rows_spec, rows_spec, rows_spec, rows_spec,
            *cast_specs,
        ],
        out_specs=[pl.BlockSpec((lp, lanes), lambda s, b: (b, s)), *cast_specs],
        out_shape=[jax.ShapeDtypeStruct((rows, ds), F32)] + [jax.ShapeDtypeStruct(w.shape, BF16) for w in to_cast],
        scratch_shapes=[
            pltpu.VMEM((nc, t * lanes), BF16),
            pltpu.VMEM((t * lanes, t * lanes), BF16),
            pltpu.VMEM((t * lanes, 2 * ns), BF16),
            pltpu.VMEM((t * lanes, 2 * ns), BF16),
            pltpu.VMEM((nc, 2 * ns), F32),
            pltpu.VMEM((nc, 2 * ns), F32),
        ],
        compiler_params=_params(("arbitrary", "arbitrary")),
        name="ssm",
    )(*args, *to_cast)
    return y, cast


def _mix_kernel(y_ref, u_ref, a_ref, ga_ref, gb_ref, head_ref, x_ref, dsk_ref, gffn_ref,
                wglu_ref, wao_ref, wout_ref, h_ref, n_ref, *, nt):
    d = x_ref.shape[1]
    ao = jnp.dot(a_ref[...], wao_ref[...], preferred_element_type=F32)
    y = y_ref[...].astype(F32) + dsk_ref[...] * u_ref[...].astype(F32)
    gy = _gelu_tanh(y).astype(BF16)
    yab = jnp.dot(gy, wglu_ref[...], preferred_element_type=F32)
    ssm_out = yab[:, 0:d] * _sigmoid(yab[:, d:2 * d])
    merged = (_sigmoid(ga_ref[...].astype(F32)) * ssm_out
              + _sigmoid(gb_ref[...].astype(F32)) * ao).astype(BF16)
    x = _padded_rows(pl.program_id(0), nt, head_ref, x_ref)
    h = x + jnp.dot(merged, wout_ref[...], preferred_element_type=F32)
    h_ref[...] = h
    ms = jnp.mean(h * h, axis=-1, keepdims=True)
    n_ref[...] = (h * lax.rsqrt(ms + EPS) * gffn_ref[...]).astype(n_ref.dtype)


def _mix(y, z, attn, head, x2, nt, d_skip, g_ffn, w_glu, w_ao, w_out):
    rows, ds = y.shape
    d = x2.shape[1]
    tm = ROW_ALIGN
    u_col = 3 * D_ATTN // ds
    ga_col = (3 * D_ATTN + ds) // d
    assert (3 * D_ATTN) % ds == 0 and (3 * D_ATTN + ds) % d == 0
    return pl.pallas_call(
        functools.partial(_mix_kernel, nt=nt),
        grid=(rows // tm,),
        in_specs=[
            pl.BlockSpec((tm, ds), lambda i: (i, 0)),
            pl.BlockSpec((tm, ds), lambda i: (i, u_col)),
            pl.BlockSpec((tm, D_ATTN), lambda i: (i, 0)),
            pl.BlockSpec((tm, d), lambda i: (i, ga_col)),
            pl.BlockSpec((tm, d), lambda i: (i, ga_col + 1)),
            _resident((tm, d), lambda i: (0, 0)),
            pl.BlockSpec((tm, d), lambda i: (_seq_block(i, nt), 0)),
            _resident((1, ds), lambda i: (0, 0)),
            _resident((1, d), lambda i: (0, 0)),
            _resident(w_glu.shape, lambda i: (0, 0)),
            _resident(w_ao.shape, lambda i: (0, 0)),
            _resident(w_out.shape, lambda i: (0, 0)),
        ],
        out_specs=[
            pl.BlockSpec((tm, d), lambda i: (i, 0)),
            pl.BlockSpec((tm, d), lambda i: (i, 0)),
        ],
        out_shape=[
            jax.ShapeDtypeStruct((rows, d), F32),
            jax.ShapeDtypeStruct((rows, d), BF16),
        ],
        compiler_params=_params(("parallel",)),
        name="mix_out",
    )(y, z, attn, z, z, head, x2, d_skip, g_ffn, w_glu, w_ao, w_out)


def _ffn_up_kernel(n_ref, wg_ref, wu_ref, cw_ref, cb_ref, wd_ref, a_ref, wdb_ref, tail, wg_scr, wu_scr):
    tm = n_ref.shape[0]

    @pl.when(pl.program_id(1) == 0)
    def _():
        tail[...] = jnp.zeros_like(tail)
        wg_scr[...] = wg_ref[...].astype(wg_scr.dtype)
        wu_scr[...] = wu_ref[...].astype(wu_scr.dtype)
        wdb_ref[...] = wd_ref[...].astype(wdb_ref.dtype)

    ts = _largest_tile(tm, (512, 256))
    prev1 = tail[CONV_WIDTH - 2:CONV_WIDTH - 1, :]
    prev2 = tail[CONV_WIDTH - 3:CONV_WIDTH - 2, :]
    for r0 in range(0, tm, ts):
        n = n_ref[r0:r0 + ts, :]
        g = jnp.dot(n, wg_scr[...], preferred_element_type=F32)
        u = jnp.dot(n, wu_scr[...], preferred_element_type=F32)
        row = lax.broadcasted_iota(jnp.int32, g.shape, 0)
        g1 = jnp.where(row == 0, prev1, pltpu.roll(g, 1, 0))
        g2 = jnp.where(row == 0, prev2, jnp.where(row == 1, prev1, pltpu.roll(g, 2, 0)))
        gc = cb_ref[...] + cw_ref[0:1, :] * g2 + cw_ref[1:2, :] * g1 + cw_ref[2:3, :] * g
        a_ref[r0:r0 + ts, :] = (gc * _sigmoid(gc) * u).astype(a_ref.dtype)
        prev1 = g[ts - 1:ts, :]
        prev2 = g[ts - 2:ts - 1, :]
    tail[CONV_WIDTH - 2:CONV_WIDTH - 1, :] = prev1
    tail[CONV_WIDTH - 3:CONV_WIDTH - 2, :] = prev2


def _largest_tile(n, candidates):
    return next(c for c in candidates if n % c == 0)


def _ffn_up(n2, w_up, conv_w, conv_b, w_down, tn=512):
    rows, d = n2.shape
    dff = conv_w.shape[1]
    nj = dff // tn
    tm = _largest_tile(rows, (1536, 1024, 768, 512, 256))
    assert dff % tn == 0
    return pl.pallas_call(
        _ffn_up_kernel,
        grid=(nj, rows // tm),
        in_specs=[
            pl.BlockSpec((tm, d), lambda j, i: (i, 0)),
            pl.BlockSpec((d, tn), lambda j, i: (0, j)),
            pl.BlockSpec((d, tn), lambda j, i: (0, nj + j)),
            pl.BlockSpec((CONV_WIDTH, tn), lambda j, i: (0, j)),
            pl.BlockSpec((1, tn), lambda j, i: (0, j)),
            pl.BlockSpec((tn, d), lambda j, i: (j, 0)),
        ],
        out_specs=[pl.BlockSpec((tm, tn), lambda j, i: (i, j)), pl.BlockSpec((tn, d), lambda j, i: (j, 0))],
        out_shape=[jax.ShapeDtypeStruct((rows, dff), BF16), jax.ShapeDtypeStruct((dff, d), BF16)],
        scratch_shapes=[pltpu.VMEM((8, tn), F32), pltpu.VMEM((d, tn), BF16), pltpu.VMEM((d, tn), BF16)],
        compiler_params=_params(("arbitrary", "arbitrary")),
        name="ffn_up",
    )(n2, w_up, w_up, conv_w, conv_b, w_down)


def _ffn_down_kernel(a0_ref, a1_ref, w_ref, h0_ref, h1_ref, g_ref, o_ref):
    tm = a0_ref.shape[0]
    for s, (a_ref, h_ref) in enumerate(((a0_ref, h0_ref), (a1_ref, h1_ref))):
        h = h_ref[...] + jnp.dot(a_ref[...], w_ref[...], preferred_element_type=F32)
        ms = jnp.mean(h * h, axis=-1, keepdims=True)
        o_ref[s * tm:(s + 1) * tm, :] = (h * lax.rsqrt(ms + EPS) * g_ref[...]).astype(o_ref.dtype)


def _ffn_down(act, w_down, h1, g_final, bsz, lp):
    dff, d = w_down.shape
    tm = ROW_ALIGN
    nt_in = lp // tm
    nt_out = (nt_in - 1) // 2
    assert (nt_in - 1) % 2 == 0

    def rows(s):
        return lambda b, t: (b * nt_in + 1 + 2 * t + s, 0)

    return pl.pallas_call(
        _ffn_down_kernel,
        grid=(bsz, nt_out),
        in_specs=[
            pl.BlockSpec((tm, dff), rows(0)),
            pl.BlockSpec((tm, dff), rows(1)),
            _resident((dff, d), lambda b, t: (0, 0)),
            pl.BlockSpec((tm, d), rows(0)),
            pl.BlockSpec((tm, d), rows(1)),
            _resident((1, d), lambda b, t: (0, 0)),
        ],
        out_specs=pl.BlockSpec((2 * tm, d), lambda b, t: (b * nt_out + t, 0)),
        out_shape=jax.ShapeDtypeStruct((bsz * nt_out * 2 * tm, d), F32),
        compiler_params=_params(("parallel", "parallel")),
        name="ffn_down",
    )(act, act, w_down, h1, h1, g_final)


def _layer(head, x2, bsz, lp, g_mix, w_in, b_f, lam_re, lam_im, log_dt, b_re, b_im, c_re, c_im, d_skip,
           w_glu, w_attn_o, w_out, g_ffn, w_up, conv_w, conv_b, w_down):
    d = x2.shape[1]
    ds = d_skip.shape[0]
    nt = lp // ROW_ALIGN
    o_q, o_k, o_v, o_f, o_u, o_ga, o_gb = (0, D_ATTN, 2 * D_ATTN, 3 * D_ATTN, 3 * D_ATTN + HEADS,
                                            3 * D_ATTN + HEADS + ds, 3 * D_ATTN + HEADS + ds + d)
    w_main, w_f = _win_pack(w_in, o_f, o_u, HEAD_DIM ** -0.5 * LOG2E)
    bf = jnp.pad(b_f.astype(F32), (0, 128 - HEADS)).reshape(1, 128)

    z, u, qb, kb = _inproj(head, x2, g_mix.reshape(1, d), w_main, w_f, bf, bsz * lp, nt, 3 * D_ATTN, ds)
    attn = _flash(z, qb, kb, bsz, lp)
    y, (w_glu_bf, w_ao_bf, w_out_bf) = _ssm(u, lam_re, lam_im, log_dt, b_re, b_im, c_re, c_im, bsz, lp,
                                            (w_glu.astype(F32), w_attn_o.astype(F32), w_out.astype(F32)))

    h1, n2 = _mix(y, z, attn, head, x2, nt, d_skip.reshape(1, ds).astype(F32), g_ffn.reshape(1, d).astype(F32),
                  w_glu_bf, w_ao_bf, w_out_bf)
    act, w_down_bf = _ffn_up(n2, w_up.astype(F32), conv_w.astype(F32), conv_b.reshape(1, -1).astype(F32),
                             w_down.astype(F32))
    return act, h1, w_down_bf


def kernel(x, meta, g_mix, w_in, b_f, lam_re, lam_im, log_dt, b_re, b_im, c_re, c_im, d_skip,
           w_glu, w_attn_o, w_out, g_ffn, w_up, conv_w, conv_b, w_down, g_final):
    bsz, seq, d = x.shape
    depth = g_mix.shape[0]
    assert depth == 1 and meta.shape[0] == N_META and seq % ROW_ALIGN == 0
    lp = seq + ROW_ALIGN
    assert (bsz * lp) % 512 == 0
    head = jnp.concatenate([jnp.zeros((PAD, d), x.dtype), meta.astype(x.dtype)], axis=0)
    act, h1, w_down_bf = _layer(head, x.reshape(bsz * seq, d), bsz, lp, g_mix[0], w_in[0], b_f[0], lam_re[0], lam_im[0], log_dt[0],
                     b_re[0], b_im[0], c_re[0], c_im[0], d_skip[0], w_glu[0], w_attn_o[0], w_out[0],
                     g_ffn[0], w_up[0], conv_w[0], conv_b[0], w_down[0])
    out = _ffn_down(act, w_down_bf, h1, g_final.reshape(1, d).astype(F32), bsz, lp)
    return out.reshape(bsz, seq, d)
```

```python
import functools
import math

import jax
import jax.numpy as jnp
from jax import lax
from jax.experimental import pallas as pl
from jax.experimental.pallas import tpu as pltpu

N_META = 16
HEADS = 8
HEAD_DIM = 128
D_ATTN = HEADS * HEAD_DIM
SSM_GROUP = 16
SSM_STATE = 64
SSM_CHUNK = 12
SSM_SLAB = 8
CONV_WIDTH = 3
EPS = 1e-6

ROW_ALIGN = 256
PAD = ROW_ALIGN - N_META
ATT_TQ = 1024
ATT_TK = 512
QK_AUG = 2 * HEAD_DIM
LOG2E = math.log2(math.e)
MASK_BIG = 1e30
NEG = -3e38
VMEM_LIMIT = 56 * 1024 * 1024

F32 = jnp.float32
BF16 = jnp.bfloat16


def _sigmoid(x):
    return 1.0 / (1.0 + jnp.exp(-x))


def _gelu_tanh(x):
    c = math.sqrt(2.0 / math.pi)
    return 0.5 * x * (1.0 + jnp.tanh(c * (x + 0.044715 * (x * x * x))))


def _params(sem, limit=VMEM_LIMIT):
    return pltpu.CompilerParams(dimension_semantics=sem, vmem_limit_bytes=limit)


def _resident(shape, index_map):
    return pl.BlockSpec(shape, index_map, pipeline_mode=pl.Buffered(1))


def _win_pack_kernel(wt_ref, o_ref, f_ref, *, o_f, o_u, q_scale):
    n_in, tr = wt_ref.shape
    o_ref[:, 0:D_ATTN] = (wt_ref[0:D_ATTN, :] * q_scale).T.astype(o_ref.dtype)
    o_ref[:, D_ATTN:o_f] = wt_ref[D_ATTN:o_f, :].T.astype(o_ref.dtype)
    o_ref[:, o_f:o_f + (n_in - o_u)] = wt_ref[o_u:n_in, :].T.astype(o_ref.dtype)
    wf = jnp.concatenate([wt_ref[o_f:o_u, :], jnp.zeros((f_ref.shape[1] - (o_u - o_f), tr), F32)], axis=0)
    f_ref[...] = wf.T.astype(f_ref.dtype)


def _win_pack(w_in, o_f, o_u, q_scale, tr=256):
    d, n_in = w_in.shape
    n_out = n_in - (o_u - o_f)
    return pl.pallas_call(
        functools.partial(_win_pack_kernel, o_f=o_f, o_u=o_u, q_scale=q_scale),
        grid=(d // tr,),
        in_specs=[pl.BlockSpec((n_in, tr), lambda i: (0, i))],
        out_specs=[pl.BlockSpec((tr, n_out), lambda i: (i, 0)), pl.BlockSpec((tr, 128), lambda i: (i, 0))],
        out_shape=[jax.ShapeDtypeStruct((d, n_out), BF16), jax.ShapeDtypeStruct((d, 128), BF16)],
        compiler_params=_params(("parallel",)),
        name="win_pack",
    )(w_in.T)


def _seq_block(t, nt):
    return (t // nt) * (nt - 1) + jnp.maximum(t % nt - 1, 0)


def _padded_rows(t, nt, head_ref, x_ref):
    return jnp.where(t % nt == 0, head_ref[...], x_ref[...])


def _split3(x):
    hi = x.astype(BF16)
    r1 = x - hi.astype(F32)
    mid = r1.astype(BF16)
    lo = (r1 - mid.astype(F32)).astype(BF16)
    return hi, mid, lo


def _forget_bias(f, bf_ref, t, nt, carry, qb_ref, kb_ref, row0):
    tm = f.shape[0]
    first = t % nt == 0
    x = f + bf_ref[...]
    logf = jnp.minimum(x, 0.0) - jnp.log1p(jnp.exp(-jnp.abs(x)))
    row = lax.broadcasted_iota(jnp.int32, (tm, 1), 0)
    valid = jnp.logical_or(jnp.logical_not(first), row >= PAD)
    logf = jnp.where(valid, logf, 0.0)
    r = lax.broadcasted_iota(jnp.int32, (tm, tm), 0)
    c = lax.broadcasted_iota(jnp.int32, (tm, tm), 1)
    tri = (r >= c).astype(BF16)
    cs3 = jnp.dot(tri, jnp.concatenate(_split3(logf), axis=1), preferred_element_type=F32)
    w = f.shape[1]
    fcum = cs3[:, 0:w] + cs3[:, w:2 * w] + cs3[:, 2 * w:3 * w] + jnp.where(first, 0.0, carry[0:1, :])
    carry[0:1, :] = fcum[tm - 1:tm, :]

    lane = lax.broadcasted_iota(jnp.int32, (tm, HEAD_DIM), 1)
    one = jnp.ones((tm, HEAD_DIM), F32)
    zero = jnp.zeros((tm, HEAD_DIM), F32)
    for h in range(HEADS):
        hi, mid, lo = (p.astype(F32) for p in _split3(fcum[:, h:h + 1] * LOG2E))
        qb = jnp.where(lane == 0, hi, jnp.where(lane == 1, mid, jnp.where(lane == 2, lo,
             jnp.where(lane < 6, one, zero))))
        khi = jnp.where(valid, hi, MASK_BIG)
        kmid = jnp.where(valid, mid, 0.0)
        klo = jnp.where(valid, lo, 0.0)
        kb = jnp.where(lane < 3, one, jnp.where(lane == 3, -khi, jnp.where(lane == 4, -kmid,
             jnp.where(lane == 5, -klo, zero))))
        qb_ref[h, row0:row0 + tm, :] = qb.astype(qb_ref.dtype)
        kb_ref[h, row0:row0 + tm, :] = kb.astype(kb_ref.dtype)


def _inproj_kernel(head_ref, xa_ref, xb_ref, g_ref, w_ref, wf_ref, bf_ref, z_ref, u_ref, qb_ref, kb_ref,
                   n_scr, carry, *, nt, u_blk, u_lo):
    i = pl.program_id(0)
    j = pl.program_id(1)
    n_i = pl.num_programs(0)
    n_j = pl.num_programs(1)

    def prepare(t):
        x = jnp.concatenate([_padded_rows(2 * t, nt, head_ref, xa_ref),
                             _padded_rows(2 * t + 1, nt, head_ref, xb_ref)], axis=0)
        ms = jnp.mean(x * x, axis=-1, keepdims=True)
        n = (x * lax.rsqrt(ms + EPS) * g_ref[...]).astype(BF16)
        n_scr[t % 2] = n
        f = jnp.dot(n, wf_ref[...], preferred_element_type=F32)
        for blk in range(2):
            _forget_bias(f[blk * ROW_ALIGN:(blk + 1) * ROW_ALIGN], bf_ref, 2 * t + blk, nt, carry,
                         qb_ref, kb_ref, blk * ROW_ALIGN)

    def project():
        acc = jnp.dot(n_scr[i % 2], w_ref[...], preferred_element_type=F32)
        z_ref[...] = acc.astype(z_ref.dtype)
        return acc

    @pl.when(jnp.logical_and(i == 0, j == 0))
    def _():
        carry[...] = jnp.zeros_like(carry)
        prepare(0)

    lookahead = jnp.logical_and(j == n_j - 1, i + 1 < n_i)

    @pl.when(lookahead)
    def _():
        prepare(i + 1)
        project()

    @pl.when(jnp.logical_not(lookahead))
    def _():
        acc = project()

        @pl.when(j == u_blk)
        def _():
            u_ref[...] = acc[:, u_lo:u_lo + u_ref.shape[1]]


def _inproj(head, x2, g, w, wf, bf, rows, nt, u_off, u_width, tn=2048):
    d = x2.shape[1]
    tm = 2 * ROW_ALIGN
    n_out = w.shape[1]
    n_i, n_j = rows // tm, n_out // tn
    u_blk, u_lo = divmod(u_off, tn)
    assert u_lo + u_width <= tn and rows % tm == 0 and u_blk < n_j - 1

    def tile(i, j):
        return jnp.minimum(i + (j == n_j - 1).astype(jnp.int32), n_i - 1)

    bias_spec = pl.BlockSpec((HEADS, tm, HEAD_DIM), lambda i, j: (0, tile(i, j), 0))
    bias_shape = jax.ShapeDtypeStruct((HEADS, rows, HEAD_DIM), BF16)
    return pl.pallas_call(
        functools.partial(_inproj_kernel, nt=nt, u_blk=u_blk, u_lo=u_lo),
        grid=(n_i, n_j),
        in_specs=[
            _resident((ROW_ALIGN, d), lambda i, j: (0, 0)),
            pl.BlockSpec((ROW_ALIGN, d), lambda i, j: (_seq_block(2 * tile(i, j), nt), 0)),
            pl.BlockSpec((ROW_ALIGN, d), lambda i, j: (_seq_block(2 * tile(i, j) + 1, nt), 0)),
            pl.BlockSpec((1, d), lambda i, j: (0, 0)),
            pl.BlockSpec((d, tn), lambda i, j: (0, j)),
            pl.BlockSpec((d, 128), lambda i, j: (0, 0)),
            pl.BlockSpec((1, 128), lambda i, j: (0, 0)),
        ],
        out_specs=[
            pl.BlockSpec((tm, tn), lambda i, j: (i, j)),
            pl.BlockSpec((tm, u_width), lambda i, j: (i, 0)),
            bias_spec, bias_spec,
        ],
        out_shape=[
            jax.ShapeDtypeStruct((rows, n_out), BF16),
            jax.ShapeDtypeStruct((rows, u_width), F32),
            bias_shape, bias_shape,
        ],
        scratch_shapes=[pltpu.VMEM((2, tm, d), BF16), pltpu.VMEM((8, 128), F32)],
        compiler_params=_params(("arbitrary", "arbitrary")),
        name="inproj",
    )(head, x2, x2, g, w, wf, bf)


def _flash_kernel(q_ref, k_ref, v_ref, qb_ref, kb_ref, o_ref,
                  qa_ref, ka_ref, qt_ref, vt_ref, sa_ref, sb_ref, mxa_ref, mxb_ref, m_ref, acc_ref):
    lp = v_ref.shape[0]
    tq, tk = ATT_TQ, ATT_TK
    n_main = (lp - ROW_ALIGN) // tq
    n_chunks = (lp - ROW_ALIGN) // tk
    va = vt_ref.shape[1]

    def augment(r0, n):
        rows = pl.ds(r0, n)
        qa_ref[rows, 0:HEAD_DIM] = q_ref[rows, :]
        qa_ref[rows, HEAD_DIM:QK_AUG] = qb_ref[0, rows, :]
        ka_ref[rows, 0:HEAD_DIM] = k_ref[rows, :]
        ka_ref[rows, HEAD_DIM:QK_AUG] = kb_ref[0, rows, :]

    augment(0, ROW_ALIGN + tq)

    ones = jnp.ones((va - HEAD_DIM, tk), vt_ref.dtype)
    vt_ref[0, 0:HEAD_DIM, 0:ROW_ALIGN] = v_ref[0:ROW_ALIGN, :].T
    vt_ref[0, HEAD_DIM:va, :] = ones

    def fill(c):
        vt_ref[1 + c, 0:HEAD_DIM, :] = v_ref[pl.ds(pl.multiple_of(ROW_ALIGN + c * tk, ROW_ALIGN), tk), :].T
        vt_ref[1 + c, HEAD_DIM:va, :] = ones

    buf_a = (sa_ref, mxa_ref)
    buf_b = (sb_ref, mxb_ref)

    def load_queries(slot, q_off, n):
        qt_ref[slot, :, 0:n] = qa_ref[pl.ds(q_off, n), :].T

    def scores(buf, slot, nq, k_off, nk, col0):
        s_ref, mx_ref = buf
        kc = ka_ref[pl.ds(k_off, nk), :]
        s = jnp.dot(kc, qt_ref[slot, :, col0:col0 + nq], preferred_element_type=F32)
        s_ref[0:nk, col0:col0 + nq] = s
        mx_ref[0:1, col0:col0 + nq] = jnp.max(s, axis=0, keepdims=True)

    def absorb(buf, slot, nk, lo, hi, mask_shift=None, lane0=0):
        s_ref, mx_ref = buf
        s = s_ref[0:nk, lo:hi]
        if mask_shift is None:
            smax = mx_ref[0:1, lo:hi]
        else:
            kr = lax.broadcasted_iota(jnp.int32, s.shape, 0) + mask_shift
            qc = lax.broadcasted_iota(jnp.int32, s.shape, 1) + lo
            s = jnp.where(kr > qc, NEG, s)
            smax = jnp.max(s, axis=0, keepdims=True)
        m = m_ref[0:1, lo:hi]
        m_new = jnp.maximum(m, smax)
        alpha = jnp.exp2(m - m_new)
        p = jnp.exp2(s - m_new).astype(BF16)
        m_ref[0:1, lo:hi] = m_new
        pv = jnp.dot(vt_ref[slot, :, lane0:lane0 + nk], p, preferred_element_type=F32)
        acc_ref[:, lo:hi] = alpha * acc_ref[:, lo:hi] + pv

    def reset(n):
        m_ref[0:1, 0:n] = jnp.full((1, n), NEG, F32)
        acc_ref[:, 0:n] = jnp.zeros((va, n), F32)

    def finish(q_off, n):
        out = acc_ref[0:HEAD_DIM, 0:n] / acc_ref[HEAD_DIM:HEAD_DIM + 1, 0:n]
        o_ref[pl.ds(q_off, n), :] = out.T.astype(o_ref.dtype)

    reset(ROW_ALIGN)
    load_queries(1, 0, ROW_ALIGN)
    scores(buf_a, 1, ROW_ALIGN, 0, ROW_ALIGN, 0)
    absorb(buf_a, 0, ROW_ALIGN, 0, ROW_ALIGN, mask_shift=0)
    finish(0, ROW_ALIGN)

    per_tile = tq // tk
    load_queries(0, ROW_ALIGN, tq)

    def tile(j, _):
        q_off = pl.multiple_of(ROW_ALIGN + j * tq, ROW_ALIGN)
        qs = j % 2
        reset(tq)
        for c in range(per_tile):
            fill(j * per_tile + c)
        scores(buf_a, qs, tq, PAD, N_META, 0)
        scores(buf_b, qs, tq, ROW_ALIGN, tk, 0)
        absorb(buf_a, 0, N_META, 0, tq, lane0=PAD)

        def pair(p):
            kb = pl.multiple_of(ROW_ALIGN + p * (2 * tk), ROW_ALIGN)
            scores(buf_a, qs, tq, kb + tk, tk, 0)
            absorb(buf_b, 1 + 2 * p, tk, 0, tq)
            scores(buf_b, qs, tq, kb + 2 * tk, tk, 0)
            absorb(buf_a, 2 + 2 * p, tk, 0, tq)

        def two_pairs(pp, _):
            pair(2 * pp)
            pair(2 * pp + 1)
            return 0

        n_pairs = j * (per_tile // 2)
        lax.fori_loop(0, n_pairs // 2, two_pairs, 0)

        @pl.when(n_pairs % 2 == 1)
        def _():
            pair(n_pairs - 1)
        scores(buf_a, qs, tq - tk, q_off + tk, tk, tk)
        q_next = pl.multiple_of(jnp.minimum(q_off + tq, lp - tq), ROW_ALIGN)
        augment(q_next, tq)
        load_queries(1 - qs, q_next, tq)
        absorb(buf_b, 1 + j * per_tile, tk, 0, tq, mask_shift=0)
        absorb(buf_a, 2 + j * per_tile, tk, tk, tq, mask_shift=tk)
        finish(q_off, tq)
        return 0

    lax.fori_loop(0, n_main, tile, 0)


def _flash(z, qb, kb, bsz, lp):
    assert (lp - ROW_ALIGN) % ATT_TQ == 0 and ATT_TQ == 2 * ATT_TK
    bias_spec = pl.BlockSpec((1, lp, HEAD_DIM), lambda b, h: (h, b, 0))
    va = HEAD_DIM + 16
    return pl.pallas_call(
        _flash_kernel,
        grid=(bsz, HEADS),
        in_specs=[
            pl.BlockSpec((lp, HEAD_DIM), lambda b, h: (b, h)),
            pl.BlockSpec((lp, HEAD_DIM), lambda b, h: (b, HEADS + h)),
            pl.BlockSpec((lp, HEAD_DIM), lambda b, h: (b, 2 * HEADS + h)),
            bias_spec, bias_spec,
        ],
        out_specs=pl.BlockSpec((lp, HEAD_DIM), lambda b, h: (b, h)),
        out_shape=jax.ShapeDtypeStruct((bsz * lp, D_ATTN), BF16),
        scratch_shapes=[
            pltpu.VMEM((lp, QK_AUG), BF16),
            pltpu.VMEM((lp, QK_AUG), BF16),
            pltpu.VMEM((2, QK_AUG, ATT_TQ), BF16),
            pltpu.VMEM((1 + (lp - ROW_ALIGN) // ATT_TK, va, ATT_TK), BF16),
            pltpu.VMEM((ATT_TK, ATT_TQ + 128), F32),
            pltpu.VMEM((ATT_TK, ATT_TQ + 128), F32),
            pltpu.VMEM((8, ATT_TQ), F32),
            pltpu.VMEM((8, ATT_TQ), F32),
            pltpu.VMEM((8, ATT_TQ), F32),
            pltpu.VMEM((va, ATT_TQ + 128), F32),
        ],
        compiler_params=_params(("parallel", "parallel")),
        name="flash",
    )(z, z, z, qb, kb)


def _tile_lanes(x, reps):
    return jnp.concatenate([x] * reps, axis=1)


def _ssm_kernel(u_ref, lr_ref, li_ref, dt_ref, lrl_ref, lil_ref, dtl_ref, btr_ref, bti_ref, cr_ref, ci_ref,
                w0_ref, w1_ref, w2_ref, y_ref, w0b_ref, w1b_ref, w2b_ref,
                lhs_scr, w_scr, inj_scr, s_scr, hp_scr):
    for src, dst in ((w0_ref, w0b_ref), (w1_ref, w1b_ref), (w2_ref, w2b_ref)):
        dst[...] = src[...].astype(dst.dtype)
    t = SSM_CHUNK
    lanes = SSM_SLAB * SSM_GROUP
    ns = SSM_SLAB * SSM_STATE
    nc = u_ref.shape[0] // t
    hs = 2 * ns

    @pl.when(pl.program_id(1) == 0)
    def _build_weights():
        lr = lr_ref[0]
        li = li_ref[0]
        dt = jnp.exp(dt_ref[0])
        btr, bti = btr_ref[0], bti_ref[0]
        cr, ci = cr_ref[0], ci_ref[0]

        mag = jnp.exp(lr * dt)
        a_re = mag * jnp.cos(li * dt)
        a_im = mag * jnp.sin(li * dt)
        powers = [(jnp.ones_like(a_re), jnp.zeros_like(a_re))]
        for _ in range(t):
            pr, pi = powers[-1]
            powers.append((pr * a_re - pi * a_im, pr * a_im + pi * a_re))
        den = lr * lr + li * li
        nr = a_re - 1.0
        z_re = (nr * lr + a_im * li) / den
        z_im = (a_im * lr - nr * li) / den
        row_g = lax.broadcasted_iota(jnp.int32, (lanes, ns), 0) // SSM_GROUP
        col_g = lax.broadcasted_iota(jnp.int32, (lanes, ns), 1) // SSM_STATE
        same = row_g == col_g

        def spread(x):
            return jnp.where(same, _tile_lanes(x, SSM_SLAB), 0.0)

        caz_re, caz_im = [], []
        for d in range(t):
            pr, pi = powers[d]
            azr = pr * z_re - pi * z_im
            azi = pr * z_im + pi * z_re
            caz_re.append(cr * azr - ci * azi)
            caz_im.append(cr * azi + ci * azr)
            i = t - 1 - d
            sr = btr * azr - bti * azi
            si = btr * azi + bti * azr
            inj_scr[i * lanes:(i + 1) * lanes, 0:ns] = spread(sr).astype(BF16)
            inj_scr[i * lanes:(i + 1) * lanes, ns:2 * ns] = spread(si).astype(BF16)
            qr, qi = powers[d + 1]
            er = cr * qr - ci * qi
            ei = cr * qi + ci * qr
            w_scr[0:ns, d * lanes:(d + 1) * lanes] = spread(er).T.astype(BF16)
            w_scr[ns:2 * ns, d * lanes:(d + 1) * lanes] = spread(-ei).T.astype(BF16)
        nt_dims = (((1,), (1,)), ((), ()))
        hp = lax.Precision.HIGHEST
        r0 = (lax.dot_general(btr, jnp.concatenate(caz_re, axis=0), nt_dims, precision=hp, preferred_element_type=F32)
              - lax.dot_general(bti, jnp.concatenate(caz_im, axis=0), nt_dims, precision=hp, preferred_element_type=F32))
        rg = lax.broadcasted_iota(jnp.int32, r0.shape, 0) // SSM_GROUP
        cg = (lax.broadcasted_iota(jnp.int32, r0.shape, 1) % lanes) // SSM_GROUP
        r0 = jnp.where(rg == cg, r0, 0.0).astype(BF16)
        w_scr[hs:hs + t * lanes, :] = jnp.zeros((t * lanes, t * lanes), w_scr.dtype)
        for i in range(t):
            w_scr[hs + i * lanes:hs + (i + 1) * lanes, i * lanes:t * lanes] = r0[:, 0:(t - i) * lanes]

    for i in range(t):
        lhs_scr[:, hs + i * lanes:hs + (i + 1) * lanes] = u_ref[pl.ds(i, nc, stride=t), :].astype(BF16)

    s_scr[...] = jnp.dot(lhs_scr[:, hs:hs + t * lanes], inj_scr[...], preferred_element_type=F32)

    dtl = jnp.exp(dtl_ref[0])
    mag = jnp.exp(float(t) * (lrl_ref[0] * dtl))
    ang = float(t) * (lil_ref[0] * dtl)
    ar = mag * jnp.cos(ang)
    ai = mag * jnp.sin(ang)

    def step(n, carry):
        hr, hi = carry
        hp_scr[pl.ds(n, 1), 0:ns] = hr
        hp_scr[pl.ds(n, 1), ns:2 * ns] = hi
        sr = s_scr[pl.ds(n, 1), 0:ns]
        si = s_scr[pl.ds(n, 1), ns:2 * ns]
        return ar * hr - ai * hi + sr, ar * hi + ai * hr + si

    zero = jnp.zeros((1, ns), F32)
    lax.fori_loop(0, nc, step, (zero, zero), unroll=8)

    lhs_scr[:, 0:hs] = hp_scr[...].astype(BF16)
    steps = 4
    wide = steps * lanes
    for jq in range(t // steps):
        kdim = hs + (jq + 1) * wide
        yq = jnp.dot(lhs_scr[:, 0:kdim], w_scr[0:kdim, jq * wide:(jq + 1) * wide], preferred_element_type=F32)
        for i in range(steps):
            y_ref[pl.ds(steps * jq + i, nc, stride=t), :] = yq[:, i * lanes:(i + 1) * lanes]


def _ssm(u, lam_re, lam_im, log_dt, b_re, b_im, c_re, c_im, bsz, lp, to_cast):
    rows, ds = u.shape
    g, p = lam_re.shape
    t, c = SSM_CHUNK, SSM_GROUP
    lanes = SSM_SLAB * c
    ns = SSM_SLAB * p
    nslab = g // SSM_SLAB
    nc = lp // t
    assert g % SSM_SLAB == 0 and lp % t == 0 and lanes == 128 and t % 4 == 0
    rep = lambda x: jnp.repeat(x, c, axis=0).reshape(nslab, lanes, -1)
    lane = lambda x: x.reshape(nslab, 1, ns)
    args = (
        u,
        rep(lam_re), rep(lam_im), rep(log_dt.reshape(g, 1)),
        lane(lam_re), lane(lam_im), lane(jnp.repeat(log_dt, p)),
        jnp.swapaxes(b_re, 1, 2).reshape(nslab, lanes, p), jnp.swapaxes(b_im, 1, 2).reshape(nslab, lanes, p),
        c_re.reshape(nslab, lanes, p), c_im.reshape(nslab, lanes, p),
    )
    rows_spec = pl.BlockSpec((1, lanes, p), lambda s, b: (s, 0, 0))
    lane_spec = pl.BlockSpec((1, 1, ns), lambda s, b: (s, 0, 0))
    steps = nslab * bsz
    assert all(w.shape[0] % (16 * steps) == 0 for w in to_cast)
    cast_specs = [pl.BlockSpec((w.shape[0] // steps, w.shape[1]), lambda s, b: (s * bsz + b, 0)) for w in to_cast]
    y, *cast = pl.pallas_call(
        _ssm_kernel,
        grid=(nslab, bsz),
        in_specs=[
            pl.BlockSpec((lp, lanes), lambda s, b: (b, s)),
            rows_spec, rows_spec, pl.BlockSpec((1, lanes, 1), lambda s, b: (s, 0, 0)),
            lane_spec, lane_spec, lane_spec,
            rows_spec, rows_spec, rows_spec, rows_spec,
            *cast_specs,
        ],
        out_specs=[pl.BlockSpec((lp, lanes), lambda s, b: (b, s)), *cast_specs],
        out_shape=[jax.ShapeDtypeStruct((rows, ds), F32)] + [jax.ShapeDtypeStruct(w.shape, BF16) for w in to_cast],
        scratch_shapes=[
            pltpu.VMEM((nc, 2 * ns + t * lanes), BF16),
            pltpu.VMEM((2 * ns + t * lanes, t * lanes), BF16),
            pltpu.VMEM((t * lanes, 2 * ns), BF16),
            pltpu.VMEM((nc, 2 * ns), F32),
            pltpu.VMEM((nc, 2 * ns), F32),
        ],
        compiler_params=_params(("arbitrary", "arbitrary")),
        name="ssm",
    )(*args, *to_cast)
    return y, cast


def _mix_kernel(y_ref, u_ref, a_ref, ga_ref, gb_ref, head_ref, x_ref, dsk_ref, gffn_ref,
                wglu_ref, wao_ref, wout_ref, h_ref, n_ref, *, nt):
    d = x_ref.shape[1]
    ao = jnp.dot(a_ref[...], wao_ref[...], preferred_element_type=F32)
    y = y_ref[...].astype(F32) + dsk_ref[...] * u_ref[...].astype(F32)
    gy = _gelu_tanh(y).astype(BF16)
    yab = jnp.dot(gy, wglu_ref[...], preferred_element_type=F32)
    ssm_out = yab[:, 0:d] * _sigmoid(yab[:, d:2 * d])
    merged = (_sigmoid(ga_ref[...].astype(F32)) * ssm_out
              + _sigmoid(gb_ref[...].astype(F32)) * ao).astype(BF16)
    x = _padded_rows(pl.program_id(0), nt, head_ref, x_ref)
    h = x + jnp.dot(merged, wout_ref[...], preferred_element_type=F32)
    h_ref[...] = h
    ms = jnp.mean(h * h, axis=-1, keepdims=True)
    n_ref[...] = (h * lax.rsqrt(ms + EPS) * gffn_ref[...]).astype(n_ref.dtype)


def _mix(y, z, attn, head, x2, nt, d_skip, g_ffn, w_glu, w_ao, w_out):
    rows, ds = y.shape
    d = x2.shape[1]
    tm = ROW_ALIGN
    u_col = 3 * D_ATTN // ds
    ga_col = (3 * D_ATTN + ds) // d
    assert (3 * D_ATTN) % ds == 0 and (3 * D_ATTN + ds) % d == 0
    return pl.pallas_call(
        functools.partial(_mix_kernel, nt=nt),
        grid=(rows // tm,),
        in_specs=[
            pl.BlockSpec((tm, ds), lambda i: (i, 0)),
            pl.BlockSpec((tm, ds), lambda i: (i, u_col)),
            pl.BlockSpec((tm, D_ATTN), lambda i: (i, 0)),
            pl.BlockSpec((tm, d), lambda i: (i, ga_col)),
            pl.BlockSpec((tm, d), lambda i: (i, ga_col + 1)),
            _resident((tm, d), lambda i: (0, 0)),
            pl.BlockSpec((tm, d), lambda i: (_seq_block(i, nt), 0)),
            _resident((1, ds), lambda i: (0, 0)),
            _resident((1, d), lambda i: (0, 0)),
            _resident(w_glu.shape, lambda i: (0, 0)),
            _resident(w_ao.shape, lambda i: (0, 0)),
            _resident(w_out.shape, lambda i: (0, 0)),
        ],
        out_specs=[
            pl.BlockSpec((tm, d), lambda i: (i, 0)),
            pl.BlockSpec((tm, d), lambda i: (i, 0)),
        ],
        out_shape=[
            jax.ShapeDtypeStruct((rows, d), F32),
            jax.ShapeDtypeStruct((rows, d), BF16),
        ],
        compiler_params=_params(("parallel",)),
        name="mix_out",
    )(y, z, attn, z, z, head, x2, d_skip, g_ffn, w_glu, w_ao, w_out)


def _ffn_up_kernel(n_ref, wg_ref, wu_ref, cw_ref, cb_ref, wd_ref, a_ref, wdb_ref, tail, wg_scr, wu_scr):
    tm = n_ref.shape[0]

    @pl.when(pl.program_id(1) == 0)
    def _():
        tail[...] = jnp.zeros_like(tail)
        wg_scr[...] = wg_ref[...].astype(wg_scr.dtype)
        wu_scr[...] = wu_ref[...].astype(wu_scr.dtype)
        wdb_ref[...] = wd_ref[...].astype(wdb_ref.dtype)

    ts = _largest_tile(tm, (512, 256))
    prev1 = tail[CONV_WIDTH - 2:CONV_WIDTH - 1, :]
    prev2 = tail[CONV_WIDTH - 3:CONV_WIDTH - 2, :]
    for r0 in range(0, tm, ts):
        n = n_ref[r0:r0 + ts, :]
        g = jnp.dot(n, wg_scr[...], preferred_element_type=F32)
        u = jnp.dot(n, wu_scr[...], preferred_element_type=F32)
        row = lax.broadcasted_iota(jnp.int32, g.shape, 0)
        g1 = jnp.where(row == 0, prev1, pltpu.roll(g, 1, 0))
        g2 = jnp.where(row == 0, prev2, jnp.where(row == 1, prev1, pltpu.roll(g, 2, 0)))
        gc = cb_ref[...] + cw_ref[0:1, :] * g2 + cw_ref[1:2, :] * g1 + cw_ref[2:3, :] * g
        a_ref[r0:r0 + ts, :] = (gc * _sigmoid(gc) * u).astype(a_ref.dtype)
        prev1 = g[ts - 1:ts, :]
        prev2 = g[ts - 2:ts - 1, :]
    tail[CONV_WIDTH - 2:CONV_WIDTH - 1, :] = prev1
    tail[CONV_WIDTH - 3:CONV_WIDTH - 2, :] = prev2


def _largest_tile(n, candidates):
    return next(c for c in candidates if n % c == 0)


def _ffn_up(n2, w_up, conv_w, conv_b, w_down, tn=512):
    rows, d = n2.shape
    dff = conv_w.shape[1]
    nj = dff // tn
    tm = _largest_tile(rows, (1536, 1024, 768, 512, 256))
    assert dff % tn == 0
    return pl.pallas_call(
        _ffn_up_kernel,
        grid=(nj, rows // tm),
        in_specs=[
            pl.BlockSpec((tm, d), lambda j, i: (i, 0)),
            pl.BlockSpec((d, tn), lambda j, i: (0, j)),
            pl.BlockSpec((d, tn), lambda j, i: (0, nj + j)),
            pl.BlockSpec((CONV_WIDTH, tn), lambda j, i: (0, j)),
            pl.BlockSpec((1, tn), lambda j, i: (0, j)),
            pl.BlockSpec((tn, d), lambda j, i: (j, 0)),
        ],
        out_specs=[pl.BlockSpec((tm, tn), lambda j, i: (i, j)), pl.BlockSpec((tn, d), lambda j, i: (j, 0))],
        out_shape=[jax.ShapeDtypeStruct((rows, dff), BF16), jax.ShapeDtypeStruct((dff, d), BF16)],
        scratch_shapes=[pltpu.VMEM((8, tn), F32), pltpu.VMEM((d, tn), BF16), pltpu.VMEM((d, tn), BF16)],
        compiler_params=_params(("arbitrary", "arbitrary")),
        name="ffn_up",
    )(n2, w_up, w_up, conv_w, conv_b, w_down)


def _ffn_down_kernel(a0_ref, a1_ref, w_ref, h0_ref, h1_ref, g_ref, o_ref):
    tm = a0_ref.shape[0]
    for s, (a_ref, h_ref) in enumerate(((a0_ref, h0_ref), (a1_ref, h1_ref))):
        h = h_ref[...] + jnp.dot(a_ref[...], w_ref[...], preferred_element_type=F32)
        ms = jnp.mean(h * h, axis=-1, keepdims=True)
        o_ref[s * tm:(s + 1) * tm, :] = (h * lax.rsqrt(ms + EPS) * g_ref[...]).astype(o_ref.dtype)


def _ffn_down(act, w_down, h1, g_final, bsz, lp):
    dff, d = w_down.shape
    tm = ROW_ALIGN
    nt_in = lp // tm
    nt_out = (nt_in - 1) // 2
    assert (nt_in - 1) % 2 == 0

    def rows(s):
        return lambda b, t: (b * nt_in + 1 + 2 * t + s, 0)

    return pl.pallas_call(
        _ffn_down_kernel,
        grid=(bsz, nt_out),
        in_specs=[
            pl.BlockSpec((tm, dff), rows(0)),
            pl.BlockSpec((tm, dff), rows(1)),
            _resident((dff, d), lambda b, t: (0, 0)),
            pl.BlockSpec((tm, d), rows(0)),
            pl.BlockSpec((tm, d), rows(1)),
            _resident((1, d), lambda b, t: (0, 0)),
        ],
        out_specs=pl.BlockSpec((2 * tm, d), lambda b, t: (b * nt_out + t, 0)),
        out_shape=jax.ShapeDtypeStruct((bsz * nt_out * 2 * tm, d), F32),
        compiler_params=_params(("parallel", "parallel")),
        name="ffn_down",
    )(act, act, w_down, h1, h1, g_final)


def _layer(head, x2, bsz, lp, g_mix, w_in, b_f, lam_re, lam_im, log_dt, b_re, b_im, c_re, c_im, d_skip,
           w_glu, w_attn_o, w_out, g_ffn, w_up, conv_w, conv_b, w_down):
    d = x2.shape[1]
    ds = d_skip.shape[0]
    nt = lp // ROW_ALIGN
    o_q, o_k, o_v, o_f, o_u, o_ga, o_gb = (0, D_ATTN, 2 * D_ATTN, 3 * D_ATTN, 3 * D_ATTN + HEADS,
                                            3 * D_ATTN + HEADS + ds, 3 * D_ATTN + HEADS + ds + d)
    w_main, w_f = _win_pack(w_in, o_f, o_u, HEAD_DIM ** -0.5 * LOG2E)
    bf = jnp.pad(b_f.astype(F32), (0, 128 - HEADS)).reshape(1, 128)

    z, u, qb, kb = _inproj(head, x2, g_mix.reshape(1, d), w_main, w_f, bf, bsz * lp, nt, 3 * D_ATTN, ds)
    attn = _flash(z, qb, kb, bsz, lp)
    y, (w_glu_bf, w_ao_bf, w_out_bf) = _ssm(u, lam_re, lam_im, log_dt, b_re, b_im, c_re, c_im, bsz, lp,
                                            (w_glu.astype(F32), w_attn_o.astype(F32), w_out.astype(F32)))

    h1, n2 = _mix(y, z, attn, head, x2, nt, d_skip.reshape(1, ds).astype(F32), g_ffn.reshape(1, d).astype(F32),
                  w_glu_bf, w_ao_bf, w_out_bf)
    act, w_down_bf = _ffn_up(n2, w_up.astype(F32), conv_w.astype(F32), conv_b.reshape(1, -1).astype(F32),
                             w_down.astype(F32))
    return act, h1, w_down_bf


def kernel(x, meta, g_mix, w_in, b_f, lam_re, lam_im, log_dt, b_re, b_im, c_re, c_im, d_skip,
           w_glu, w_attn_o, w_out, g_ffn, w_up, conv_w, conv_b, w_down, g_final):
    bsz, seq, d = x.shape
    depth = g_mix.shape[0]
    assert depth == 1 and meta.shape[0] == N_META and seq % ROW_ALIGN == 0
    lp = seq + ROW_ALIGN
    assert (bsz * lp) % 512 == 0
    head = jnp.concatenate([jnp.zeros((PAD, d), x.dtype), meta.astype(x.dtype)], axis=0)
    act, h1, w_down_bf = _layer(head, x.reshape(bsz * seq, d), bsz, lp, g_mix[0], w_in[0], b_f[0], lam_re[0], lam_im[0], log_dt[0],
                     b_re[0], b_im[0], c_re[0], c_im[0], d_skip[0], w_glu[0], w_attn_o[0], w_out[0],
                     g_ffn[0], w_up[0], conv_w[0], conv_b[0], w_down[0])
    out = _ffn_down(act, w_down_bf, h1, g_final.reshape(1, d).astype(F32), bsz, lp)
    return out.reshape(bsz, seq, d)
```

```python
import functools
import math

import jax
import jax.numpy as jnp
from jax import lax
from jax.experimental import pallas as pl
from jax.experimental.pallas import tpu as pltpu

N_META = 16
HEADS = 8
HEAD_DIM = 128
D_ATTN = HEADS * HEAD_DIM
SSM_GROUP = 16
SSM_STATE = 64
SSM_CHUNK = 12
SSM_SLAB = 8
CONV_WIDTH = 3
EPS = 1e-6

ROW_ALIGN = 256
PAD = ROW_ALIGN - N_META
ATT_TQ = 1024
ATT_TK = 512
QK_AUG = 2 * HEAD_DIM
LOG2E = math.log2(math.e)
MASK_BIG = 1e30
NEG = -3e38
VMEM_LIMIT = 56 * 1024 * 1024

F32 = jnp.float32
BF16 = jnp.bfloat16


def _sigmoid(x):
    return 1.0 / (1.0 + jnp.exp(-x))


def _gelu_tanh(x):
    c = math.sqrt(2.0 / math.pi)
    return 0.5 * x * (1.0 + jnp.tanh(c * (x + 0.044715 * (x * x * x))))


def _params(sem, limit=VMEM_LIMIT):
    return pltpu.CompilerParams(dimension_semantics=sem, vmem_limit_bytes=limit)


def _resident(shape, index_map):
    return pl.BlockSpec(shape, index_map, pipeline_mode=pl.Buffered(1))


def _win_pack_kernel(wt_ref, o_ref, f_ref, *, o_f, o_u, q_scale):
    n_in, tr = wt_ref.shape
    o_ref[:, 0:D_ATTN] = (wt_ref[0:D_ATTN, :] * q_scale).T.astype(o_ref.dtype)
    o_ref[:, D_ATTN:o_f] = wt_ref[D_ATTN:o_f, :].T.astype(o_ref.dtype)
    o_ref[:, o_f:o_f + (n_in - o_u)] = wt_ref[o_u:n_in, :].T.astype(o_ref.dtype)
    wf = jnp.concatenate([wt_ref[o_f:o_u, :], jnp.zeros((f_ref.shape[1] - (o_u - o_f), tr), F32)], axis=0)
    f_ref[...] = wf.T.astype(f_ref.dtype)


def _win_pack(w_in, o_f, o_u, q_scale, tr=256):
    d, n_in = w_in.shape
    n_out = n_in - (o_u - o_f)
    return pl.pallas_call(
        functools.partial(_win_pack_kernel, o_f=o_f, o_u=o_u, q_scale=q_scale),
        grid=(d // tr,),
        in_specs=[pl.BlockSpec((n_in, tr), lambda i: (0, i))],
        out_specs=[pl.BlockSpec((tr, n_out), lambda i: (i, 0)), pl.BlockSpec((tr, 128), lambda i: (i, 0))],
        out_shape=[jax.ShapeDtypeStruct((d, n_out), BF16), jax.ShapeDtypeStruct((d, 128), BF16)],
        compiler_params=_params(("parallel",)),
        name="win_pack",
    )(w_in.T)


def _seq_block(t, nt):
    return (t // nt) * (nt - 1) + jnp.maximum(t % nt - 1, 0)


def _padded_rows(t, nt, head_ref, x_ref):
    return jnp.where(t % nt == 0, head_ref[...], x_ref[...])


def _split3(x):
    hi = x.astype(BF16)
    r1 = x - hi.astype(F32)
    mid = r1.astype(BF16)
    lo = (r1 - mid.astype(F32)).astype(BF16)
    return hi, mid, lo


def _forget_bias(f, bf_ref, t, nt, carry, qb_ref, kb_ref, row0):
    tm = f.shape[0]
    first = t % nt == 0
    x = f + bf_ref[...]
    logf = jnp.minimum(x, 0.0) - jnp.log1p(jnp.exp(-jnp.abs(x)))
    row = lax.broadcasted_iota(jnp.int32, (tm, 1), 0)
    valid = jnp.logical_or(jnp.logical_not(first), row >= PAD)
    logf = jnp.where(valid, logf, 0.0)
    r = lax.broadcasted_iota(jnp.int32, (tm, tm), 0)
    c = lax.broadcasted_iota(jnp.int32, (tm, tm), 1)
    tri = (r >= c).astype(BF16)
    cs3 = jnp.dot(tri, jnp.concatenate(_split3(logf), axis=1), preferred_element_type=F32)
    w = f.shape[1]
    fcum = cs3[:, 0:w] + cs3[:, w:2 * w] + cs3[:, 2 * w:3 * w] + jnp.where(first, 0.0, carry[0:1, :])
    carry[0:1, :] = fcum[tm - 1:tm, :]

    lane = lax.broadcasted_iota(jnp.int32, (tm, HEAD_DIM), 1)
    one = jnp.ones((tm, HEAD_DIM), F32)
    zero = jnp.zeros((tm, HEAD_DIM), F32)
    for h in range(HEADS):
        hi, mid, lo = (p.astype(F32) for p in _split3(fcum[:, h:h + 1] * LOG2E))
        qb = jnp.where(lane == 0, hi, jnp.where(lane == 1, mid, jnp.where(lane == 2, lo,
             jnp.where(lane < 6, one, zero))))
        khi = jnp.where(valid, hi, MASK_BIG)
        kmid = jnp.where(valid, mid, 0.0)
        klo = jnp.where(valid, lo, 0.0)
        kb = jnp.where(lane < 3, one, jnp.where(lane == 3, -khi, jnp.where(lane == 4, -kmid,
             jnp.where(lane == 5, -klo, zero))))
        qb_ref[h, row0:row0 + tm, :] = qb.astype(qb_ref.dtype)
        kb_ref[h, row0:row0 + tm, :] = kb.astype(kb_ref.dtype)


def _inproj_kernel(head_ref, xa_ref, xb_ref, g_ref, w_ref, wf_ref, bf_ref, z_ref, u_ref, qb_ref, kb_ref,
                   n_scr, carry, *, nt, u_blk, u_lo):
    i = pl.program_id(0)
    j = pl.program_id(1)
    n_i = pl.num_programs(0)
    n_j = pl.num_programs(1)

    def prepare(t):
        x = jnp.concatenate([_padded_rows(2 * t, nt, head_ref, xa_ref),
                             _padded_rows(2 * t + 1, nt, head_ref, xb_ref)], axis=0)
        ms = jnp.mean(x * x, axis=-1, keepdims=True)
        n = (x * lax.rsqrt(ms + EPS) * g_ref[...]).astype(BF16)
        n_scr[t % 2] = n
        f = jnp.dot(n, wf_ref[...], preferred_element_type=F32)
        for blk in range(2):
            _forget_bias(f[blk * ROW_ALIGN:(blk + 1) * ROW_ALIGN], bf_ref, 2 * t + blk, nt, carry,
                         qb_ref, kb_ref, blk * ROW_ALIGN)

    def project():
        acc = jnp.dot(n_scr[i % 2], w_ref[...], preferred_element_type=F32)
        z_ref[...] = acc.astype(z_ref.dtype)
        return acc

    @pl.when(jnp.logical_and(i == 0, j == 0))
    def _():
        carry[...] = jnp.zeros_like(carry)
        prepare(0)

    lookahead = jnp.logical_and(j == n_j - 1, i + 1 < n_i)

    @pl.when(lookahead)
    def _():
        prepare(i + 1)
        project()

    @pl.when(jnp.logical_not(lookahead))
    def _():
        acc = project()

        @pl.when(j == u_blk)
        def _():
            u_ref[...] = acc[:, u_lo:u_lo + u_ref.shape[1]]


def _inproj(head, x2, g, w, wf, bf, rows, nt, u_off, u_width, tn=2048):
    d = x2.shape[1]
    tm = 2 * ROW_ALIGN
    n_out = w.shape[1]
    n_i, n_j = rows // tm, n_out // tn
    u_blk, u_lo = divmod(u_off, tn)
    assert u_lo + u_width <= tn and rows % tm == 0 and u_blk < n_j - 1

    def tile(i, j):
        return jnp.minimum(i + (j == n_j - 1).astype(jnp.int32), n_i - 1)

    bias_spec = pl.BlockSpec((HEADS, tm, HEAD_DIM), lambda i, j: (0, tile(i, j), 0))
    bias_shape = jax.ShapeDtypeStruct((HEADS, rows, HEAD_DIM), BF16)
    return pl.pallas_call(
        functools.partial(_inproj_kernel, nt=nt, u_blk=u_blk, u_lo=u_lo),
        grid=(n_i, n_j),
        in_specs=[
            _resident((ROW_ALIGN, d), lambda i, j: (0, 0)),
            pl.BlockSpec((ROW_ALIGN, d), lambda i, j: (_seq_block(2 * tile(i, j), nt), 0)),
            pl.BlockSpec((ROW_ALIGN, d), lambda i, j: (_seq_block(2 * tile(i, j) + 1, nt), 0)),
            pl.BlockSpec((1, d), lambda i, j: (0, 0)),
            pl.BlockSpec((d, tn), lambda i, j: (0, j)),
            pl.BlockSpec((d, 128), lambda i, j: (0, 0)),
            pl.BlockSpec((1, 128), lambda i, j: (0, 0)),
        ],
        out_specs=[
            pl.BlockSpec((tm, tn), lambda i, j: (i, j)),
            pl.BlockSpec((tm, u_width), lambda i, j: (i, 0)),
            bias_spec, bias_spec,
        ],
        out_shape=[
            jax.ShapeDtypeStruct((rows, n_out), BF16),
            jax.ShapeDtypeStruct((rows, u_width), F32),
            bias_shape, bias_shape,
        ],
        scratch_shapes=[pltpu.VMEM((2, tm, d), BF16), pltpu.VMEM((8, 128), F32)],
        compiler_params=_params(("arbitrary", "arbitrary")),
        name="inproj",
    )(head, x2, x2, g, w, wf, bf)


def _flash_kernel(q_ref, k_ref, v_ref, qb_ref, kb_ref, o_ref,
                  qa_ref, ka_ref, qt_ref, vt_ref, sa_ref, sb_ref, mxa_ref, mxb_ref, m_ref, acc_ref):
    lp = v_ref.shape[0]
    tq, tk = ATT_TQ, ATT_TK
    n_main = (lp - ROW_ALIGN) // tq
    n_chunks = (lp - ROW_ALIGN) // tk
    va = vt_ref.shape[1]

    def augment(r0, n):
        rows = pl.ds(r0, n)
        qa_ref[rows, 0:HEAD_DIM] = q_ref[rows, :]
        qa_ref[rows, HEAD_DIM:QK_AUG] = qb_ref[0, rows, :]
        ka_ref[rows, 0:HEAD_DIM] = k_ref[rows, :]
        ka_ref[rows, HEAD_DIM:QK_AUG] = kb_ref[0, rows, :]

    augment(0, ROW_ALIGN + tq)

    ones = jnp.ones((va - HEAD_DIM, tk), vt_ref.dtype)
    vt_ref[0, 0:HEAD_DIM, 0:ROW_ALIGN] = v_ref[0:ROW_ALIGN, :].T
    vt_ref[0, HEAD_DIM:va, :] = ones

    def fill(c):
        vt_ref[1 + c, 0:HEAD_DIM, :] = v_ref[pl.ds(pl.multiple_of(ROW_ALIGN + c * tk, ROW_ALIGN), tk), :].T
        vt_ref[1 + c, HEAD_DIM:va, :] = ones

    buf_a = (sa_ref, mxa_ref)
    buf_b = (sb_ref, mxb_ref)

    def load_queries(slot, q_off, n):
        qt_ref[slot, :, 0:n] = qa_ref[pl.ds(q_off, n), :].T

    def scores(buf, slot, nq, k_off, nk, col0):
        s_ref, mx_ref = buf
        kc = ka_ref[pl.ds(k_off, nk), :]
        s = jnp.dot(kc, qt_ref[slot, :, col0:col0 + nq], preferred_element_type=F32)
        s_ref[0:nk, col0:col0 + nq] = s
        mx_ref[0:1, col0:col0 + nq] = jnp.max(s, axis=0, keepdims=True)

    def absorb(buf, slot, nk, lo, hi, mask_shift=None, lane0=0):
        s_ref, mx_ref = buf
        s = s_ref[0:nk, lo:hi]
        if mask_shift is None:
            smax = mx_ref[0:1, lo:hi]
        else:
            kr = lax.broadcasted_iota(jnp.int32, s.shape, 0) + mask_shift
            qc = lax.broadcasted_iota(jnp.int32, s.shape, 1) + lo
            s = jnp.where(kr > qc, NEG, s)
            smax = jnp.max(s, axis=0, keepdims=True)
        m = m_ref[0:1, lo:hi]
        m_new = jnp.maximum(m, smax)
        alpha = jnp.exp2(m - m_new)
        p = jnp.exp2(s - m_new).astype(BF16)
        m_ref[0:1, lo:hi] = m_new
        pv = jnp.dot(vt_ref[slot, :, lane0:lane0 + nk], p, preferred_element_type=F32)
        acc_ref[:, lo:hi] = alpha * acc_ref[:, lo:hi] + pv

    def reset(n):
        m_ref[0:1, 0:n] = jnp.full((1, n), NEG, F32)
        acc_ref[:, 0:n] = jnp.zeros((va, n), F32)

    def finish(q_off, n):
        out = acc_ref[0:HEAD_DIM, 0:n] / acc_ref[HEAD_DIM:HEAD_DIM + 1, 0:n]
        o_ref[pl.ds(q_off, n), :] = out.T.astype(o_ref.dtype)

    top = ROW_ALIGN - HEAD_DIM
    o_ref[0:top, :] = jnp.zeros((top, HEAD_DIM), o_ref.dtype)
    reset(HEAD_DIM)
    load_queries(1, top, HEAD_DIM)
    scores(buf_a, 1, HEAD_DIM, top, HEAD_DIM, 0)
    absorb(buf_a, 0, HEAD_DIM, 0, HEAD_DIM, mask_shift=0, lane0=top)
    finish(top, HEAD_DIM)

    per_tile = tq // tk
    load_queries(0, ROW_ALIGN, tq)

    def tile(j, _):
        q_off = pl.multiple_of(ROW_ALIGN + j * tq, ROW_ALIGN)
        qs = j % 2
        reset(tq)
        for c in range(per_tile):
            fill(j * per_tile + c)
        scores(buf_a, qs, tq, PAD, N_META, 0)
        scores(buf_b, qs, tq, ROW_ALIGN, tk, 0)
        absorb(buf_a, 0, N_META, 0, tq, lane0=PAD)

        def pair(p):
            kb = pl.multiple_of(ROW_ALIGN + p * (2 * tk), ROW_ALIGN)
            scores(buf_a, qs, tq, kb + tk, tk, 0)
            absorb(buf_b, 1 + 2 * p, tk, 0, tq)
            scores(buf_b, qs, tq, kb + 2 * tk, tk, 0)
            absorb(buf_a, 2 + 2 * p, tk, 0, tq)

        def two_pairs(pp, _):
            pair(2 * pp)
            pair(2 * pp + 1)
            return 0

        n_pairs = j * (per_tile // 2)
        lax.fori_loop(0, n_pairs // 2, two_pairs, 0)

        @pl.when(n_pairs % 2 == 1)
        def _():
            pair(n_pairs - 1)
        scores(buf_a, qs, tq - tk, q_off + tk, tk, tk)
        q_next = pl.multiple_of(jnp.minimum(q_off + tq, lp - tq), ROW_ALIGN)
        augment(q_next, tq)
        load_queries(1 - qs, q_next, tq)
        absorb(buf_b, 1 + j * per_tile, tk, 0, tq, mask_shift=0)
        absorb(buf_a, 2 + j * per_tile, tk, tk, tq, mask_shift=tk)
        finish(q_off, tq)
        return 0

    lax.fori_loop(0, n_main, tile, 0)


def _flash(z, qb, kb, bsz, lp):
    assert (lp - ROW_ALIGN) % ATT_TQ == 0 and ATT_TQ == 2 * ATT_TK
    bias_spec = pl.BlockSpec((1, lp, HEAD_DIM), lambda b, h: (h, b, 0))
    va = HEAD_DIM + 16
    return pl.pallas_call(
        _flash_kernel,
        grid=(bsz, HEADS),
        in_specs=[
            pl.BlockSpec((lp, HEAD_DIM), lambda b, h: (b, h)),
            pl.BlockSpec((lp, HEAD_DIM), lambda b, h: (b, HEADS + h)),
            pl.BlockSpec((lp, HEAD_DIM), lambda b, h: (b, 2 * HEADS + h)),
            bias_spec, bias_spec,
        ],
        out_specs=pl.BlockSpec((lp, HEAD_DIM), lambda b, h: (b, h)),
        out_shape=jax.ShapeDtypeStruct((bsz * lp, D_ATTN), BF16),
        scratch_shapes=[
            pltpu.VMEM((lp, QK_AUG), BF16),
            pltpu.VMEM((lp, QK_AUG), BF16),
            pltpu.VMEM((2, QK_AUG, ATT_TQ), BF16),
            pltpu.VMEM((1 + (lp - ROW_ALIGN) // ATT_TK, va, ATT_TK), BF16),
            pltpu.VMEM((ATT_TK, ATT_TQ + 128), F32),
            pltpu.VMEM((ATT_TK, ATT_TQ + 128), F32),
            pltpu.VMEM((8, ATT_TQ), F32),
            pltpu.VMEM((8, ATT_TQ), F32),
            pltpu.VMEM((8, ATT_TQ), F32),
            pltpu.VMEM((va, ATT_TQ + 128), F32),
        ],
        compiler_params=_params(("parallel", "parallel")),
        name="flash",
    )(z, z, z, qb, kb)


def _tile_lanes(x, reps):
    return jnp.concatenate([x] * reps, axis=1)


def _ssm_kernel(u_ref, lr_ref, li_ref, dt_ref, lrl_ref, lil_ref, dtl_ref, btr_ref, bti_ref, cr_ref, ci_ref,
                w0_ref, w1_ref, w2_ref, y_ref, w0b_ref, w1b_ref, w2b_ref,
                lhs_scr, w_scr, inj_scr, s_scr, hp_scr):
    for src, dst in ((w0_ref, w0b_ref), (w1_ref, w1b_ref), (w2_ref, w2b_ref)):
        dst[...] = src[...].astype(dst.dtype)
    t = SSM_CHUNK
    lanes = SSM_SLAB * SSM_GROUP
    ns = SSM_SLAB * SSM_STATE
    nc = u_ref.shape[0] // t
    hs = 2 * ns

    @pl.when(pl.program_id(1) == 0)
    def _build_weights():
        lr = lr_ref[0]
        li = li_ref[0]
        dt = jnp.exp(dt_ref[0])
        btr, bti = btr_ref[0], bti_ref[0]
        cr, ci = cr_ref[0], ci_ref[0]

        mag = jnp.exp(lr * dt)
        a_re = mag * jnp.cos(li * dt)
        a_im = mag * jnp.sin(li * dt)
        powers = [(jnp.ones_like(a_re), jnp.zeros_like(a_re))]
        for _ in range(t):
            pr, pi = powers[-1]
            powers.append((pr * a_re - pi * a_im, pr * a_im + pi * a_re))
        den = lr * lr + li * li
        nr = a_re - 1.0
        z_re = (nr * lr + a_im * li) / den
        z_im = (a_im * lr - nr * li) / den
        row_g = lax.broadcasted_iota(jnp.int32, (lanes, ns), 0) // SSM_GROUP
        col_g = lax.broadcasted_iota(jnp.int32, (lanes, ns), 1) // SSM_STATE
        same = row_g == col_g

        def spread(x):
            return jnp.where(same, _tile_lanes(x, SSM_SLAB), 0.0)

        caz_re, caz_im = [], []
        for d in range(t):
            pr, pi = powers[d]
            azr = pr * z_re - pi * z_im
            azi = pr * z_im + pi * z_re
            caz_re.append(cr * azr - ci * azi)
            caz_im.append(cr * azi + ci * azr)
            i = t - 1 - d
            sr = btr * azr - bti * azi
            si = btr * azi + bti * azr
            inj_scr[i * lanes:(i + 1) * lanes, 0:ns] = spread(sr).astype(BF16)
            inj_scr[i * lanes:(i + 1) * lanes, ns:2 * ns] = spread(si).astype(BF16)
            qr, qi = powers[d + 1]
            er = cr * qr - ci * qi
            ei = cr * qi + ci * qr
            w_scr[0:ns, d * lanes:(d + 1) * lanes] = spread(er).T.astype(BF16)
            w_scr[ns:2 * ns, d * lanes:(d + 1) * lanes] = spread(-ei).T.astype(BF16)
        nt_dims = (((1,), (1,)), ((), ()))
        hp = lax.Precision.HIGHEST
        r0 = (lax.dot_general(btr, jnp.concatenate(caz_re, axis=0), nt_dims, precision=hp, preferred_element_type=F32)
              - lax.dot_general(bti, jnp.concatenate(caz_im, axis=0), nt_dims, precision=hp, preferred_element_type=F32))
        rg = lax.broadcasted_iota(jnp.int32, r0.shape, 0) // SSM_GROUP
        cg = (lax.broadcasted_iota(jnp.int32, r0.shape, 1) % lanes) // SSM_GROUP
        r0 = jnp.where(rg == cg, r0, 0.0).astype(BF16)
        w_scr[hs:hs + t * lanes, :] = jnp.zeros((t * lanes, t * lanes), w_scr.dtype)
        for i in range(t):
            w_scr[hs + i * lanes:hs + (i + 1) * lanes, i * lanes:t * lanes] = r0[:, 0:(t - i) * lanes]

    for i in range(t):
        lhs_scr[:, hs + i * lanes:hs + (i + 1) * lanes] = u_ref[pl.ds(i, nc, stride=t), :].astype(BF16)

    s_scr[...] = jnp.dot(lhs_scr[:, hs:hs + t * lanes], inj_scr[...], preferred_element_type=F32)

    dtl = jnp.exp(dtl_ref[0])
    mag = jnp.exp(float(t) * (lrl_ref[0] * dtl))
    ang = float(t) * (lil_ref[0] * dtl)
    ar = mag * jnp.cos(ang)
    ai = mag * jnp.sin(ang)

    def step(n, carry):
        hr, hi = carry
        hp_scr[pl.ds(n, 1), 0:ns] = hr
        hp_scr[pl.ds(n, 1), ns:2 * ns] = hi
        sr = s_scr[pl.ds(n, 1), 0:ns]
        si = s_scr[pl.ds(n, 1), ns:2 * ns]
        return ar * hr - ai * hi + sr, ar * hi + ai * hr + si

    zero = jnp.zeros((1, ns), F32)
    lax.fori_loop(0, nc, step, (zero, zero), unroll=8)

    lhs_scr[:, 0:hs] = hp_scr[...].astype(BF16)
    steps = 4
    wide = steps * lanes
    for jq in range(t // steps):
        kdim = hs + (jq + 1) * wide
        yq = jnp.dot(lhs_scr[:, 0:kdim], w_scr[0:kdim, jq * wide:(jq + 1) * wide], preferred_element_type=F32)
        for i in range(steps):
            y_ref[pl.ds(steps * jq + i, nc, stride=t), :] = yq[:, i * lanes:(i + 1) * lanes]


def _ssm(u, lam_re, lam_im, log_dt, b_re, b_im, c_re, c_im, bsz, lp, to_cast):
    rows, ds = u.shape
    g, p = lam_re.shape
    t, c = SSM_CHUNK, SSM_GROUP
    lanes = SSM_SLAB * c
    ns = SSM_SLAB * p
    nslab = g // SSM_SLAB
    nc = lp // t
    assert g % SSM_SLAB == 0 and lp % t == 0 and lanes == 128 and t % 4 == 0
    rep = lambda x: jnp.repeat(x, c, axis=0).reshape(nslab, lanes, -1)
    lane = lambda x: x.reshape(nslab, 1, ns)
    args = (
        u,
        rep(lam_re), rep(lam_im), rep(log_dt.reshape(g, 1)),
        lane(lam_re), lane(lam_im), lane(jnp.repeat(log_dt, p)),
        jnp.swapaxes(b_re, 1, 2).reshape(nslab, lanes, p), jnp.swapaxes(b_im, 1, 2).reshape(nslab, lanes, p),
        c_re.reshape(nslab, lanes, p), c_im.reshape(nslab, lanes, p),
    )
    rows_spec = pl.BlockSpec((1, lanes, p), lambda s, b: (s, 0, 0))
    lane_spec = pl.BlockSpec((1, 1, ns), lambda s, b: (s, 0, 0))
    steps = nslab * bsz
    assert all(w.shape[0] % (16 * steps) == 0 for w in to_cast)
    cast_specs = [pl.BlockSpec((w.shape[0] // steps, w.shape[1]), lambda s, b: (s * bsz + b, 0)) for w in to_cast]
    y, *cast = pl.pallas_call(
        _ssm_kernel,
        grid=(nslab, bsz),
        in_specs=[
            pl.BlockSpec((lp, lanes), lambda s, b: (b, s)),
            rows_spec, rows_spec, pl.BlockSpec((1, lanes, 1), lambda s, b: (s, 0, 0)),
            lane_spec, lane_spec, lane_spec,
            rows_spec, rows_spec, rows_spec, rows_spec,
            *cast_specs,
        ],
        out_specs=[pl.BlockSpec((lp, lanes), lambda s, b: (b, s)), *cast_specs],
        out_shape=[jax.ShapeDtypeStruct((rows, ds), F32)] + [jax.ShapeDtypeStruct(w.shape, BF16) for w in to_cast],
        scratch_shapes=[
            pltpu.VMEM((nc, 2 * ns + t * lanes), BF16),
            pltpu.VMEM((2 * ns + t * lanes, t * lanes), BF16),
            pltpu.VMEM((t * lanes, 2 * ns), BF16),
            pltpu.VMEM((nc, 2 * ns), F32),
            pltpu.VMEM((nc, 2 * ns), F32),
        ],
        compiler_params=_params(("arbitrary", "arbitrary")),
        name="ssm",
    )(*args, *to_cast)
    return y, cast


def _mix_kernel(y_ref, u_ref, a_ref, ga_ref, gb_ref, head_ref, x_ref, dsk_ref, gffn_ref,
                wglu_ref, wao_ref, wout_ref, h_ref, n_ref, *, nt):
    d = x_ref.shape[1]
    ao = jnp.dot(a_ref[...], wao_ref[...], preferred_element_type=F32)
    y = y_ref[...].astype(F32) + dsk_ref[...] * u_ref[...].astype(F32)
    gy = _gelu_tanh(y).astype(BF16)
    yab = jnp.dot(gy, wglu_ref[...], preferred_element_type=F32)
    ssm_out = yab[:, 0:d] * _sigmoid(yab[:, d:2 * d])
    merged = (_sigmoid(ga_ref[...].astype(F32)) * ssm_out
              + _sigmoid(gb_ref[...].astype(F32)) * ao).astype(BF16)
    x = _padded_rows(pl.program_id(0), nt, head_ref, x_ref)
    h = x + jnp.dot(merged, wout_ref[...], preferred_element_type=F32)
    h_ref[...] = h
    ms = jnp.mean(h * h, axis=-1, keepdims=True)
    n_ref[...] = (h * lax.rsqrt(ms + EPS) * gffn_ref[...]).astype(n_ref.dtype)


def _mix(y, z, attn, head, x2, nt, d_skip, g_ffn, w_glu, w_ao, w_out):
    rows, ds = y.shape
    d = x2.shape[1]
    tm = ROW_ALIGN
    u_col = 3 * D_ATTN // ds
    ga_col = (3 * D_ATTN + ds) // d
    assert (3 * D_ATTN) % ds == 0 and (3 * D_ATTN + ds) % d == 0
    return pl.pallas_call(
        functools.partial(_mix_kernel, nt=nt),
        grid=(rows // tm,),
        in_specs=[
            pl.BlockSpec((tm, ds), lambda i: (i, 0)),
            pl.BlockSpec((tm, ds), lambda i: (i, u_col)),
            pl.BlockSpec((tm, D_ATTN), lambda i: (i, 0)),
            pl.BlockSpec((tm, d), lambda i: (i, ga_col)),
            pl.BlockSpec((tm, d), lambda i: (i, ga_col + 1)),
            _resident((tm, d), lambda i: (0, 0)),
            pl.BlockSpec((tm, d), lambda i: (_seq_block(i, nt), 0)),
            _resident((1, ds), lambda i: (0, 0)),
            _resident((1, d), lambda i: (0, 0)),
            _resident(w_glu.shape, lambda i: (0, 0)),
            _resident(w_ao.shape, lambda i: (0, 0)),
            _resident(w_out.shape, lambda i: (0, 0)),
        ],
        out_specs=[
            pl.BlockSpec((tm, d), lambda i: (i, 0)),
            pl.BlockSpec((tm, d), lambda i: (i, 0)),
        ],
        out_shape=[
            jax.ShapeDtypeStruct((rows, d), F32),
            jax.ShapeDtypeStruct((rows, d), BF16),
        ],
        compiler_params=_params(("parallel",)),
        name="mix_out",
    )(y, z, attn, z, z, head, x2, d_skip, g_ffn, w_glu, w_ao, w_out)


def _ffn_up_kernel(n_ref, wg_ref, wu_ref, cw_ref, cb_ref, wd_ref, a_ref, wdb_ref, tail, wg_scr, wu_scr):
    tm = n_ref.shape[0]

    @pl.when(pl.program_id(1) == 0)
    def _():
        tail[...] = jnp.zeros_like(tail)
        wg_scr[...] = wg_ref[...].astype(wg_scr.dtype)
        wu_scr[...] = wu_ref[...].astype(wu_scr.dtype)
        wdb_ref[...] = wd_ref[...].astype(wdb_ref.dtype)

    ts = _largest_tile(tm, (512, 256))
    prev1 = tail[CONV_WIDTH - 2:CONV_WIDTH - 1, :]
    prev2 = tail[CONV_WIDTH - 3:CONV_WIDTH - 2, :]
    for r0 in range(0, tm, ts):
        n = n_ref[r0:r0 + ts, :]
        g = jnp.dot(n, wg_scr[...], preferred_element_type=F32)
        u = jnp.dot(n, wu_scr[...], preferred_element_type=F32)
        row = lax.broadcasted_iota(jnp.int32, g.shape, 0)
        g1 = jnp.where(row == 0, prev1, pltpu.roll(g, 1, 0))
        g2 = jnp.where(row == 0, prev2, jnp.where(row == 1, prev1, pltpu.roll(g, 2, 0)))
        gc = cb_ref[...] + cw_ref[0:1, :] * g2 + cw_ref[1:2, :] * g1 + cw_ref[2:3, :] * g
        a_ref[r0:r0 + ts, :] = (gc * _sigmoid(gc) * u).astype(a_ref.dtype)
        prev1 = g[ts - 1:ts, :]
        prev2 = g[ts - 2:ts - 1, :]
    tail[CONV_WIDTH - 2:CONV_WIDTH - 1, :] = prev1
    tail[CONV_WIDTH - 3:CONV_WIDTH - 2, :] = prev2


def _largest_tile(n, candidates):
    return next(c for c in candidates if n % c == 0)


def _ffn_up(n2, w_up, conv_w, conv_b, w_down, tn=512):
    rows, d = n2.shape
    dff = conv_w.shape[1]
    nj = dff // tn
    tm = _largest_tile(rows, (1536, 1024, 768, 512, 256))
    assert dff % tn == 0
    return pl.pallas_call(
        _ffn_up_kernel,
        grid=(nj, rows // tm),
        in_specs=[
            pl.BlockSpec((tm, d), lambda j, i: (i, 0)),
            pl.BlockSpec((d, tn), lambda j, i: (0, j)),
            pl.BlockSpec((d, tn), lambda j, i: (0, nj + j)),
            pl.BlockSpec((CONV_WIDTH, tn), lambda j, i: (0, j)),
            pl.BlockSpec((1, tn), lambda j, i: (0, j)),
            pl.BlockSpec((tn, d), lambda j, i: (j, 0)),
        ],
        out_specs=[pl.BlockSpec((tm, tn), lambda j, i: (i, j)), pl.BlockSpec((tn, d), lambda j, i: (j, 0))],
        out_shape=[jax.ShapeDtypeStruct((rows, dff), BF16), jax.ShapeDtypeStruct((dff, d), BF16)],
        scratch_shapes=[pltpu.VMEM((8, tn), F32), pltpu.VMEM((d, tn), BF16), pltpu.VMEM((d, tn), BF16)],
        compiler_params=_params(("arbitrary", "arbitrary")),
        name="ffn_up",
    )(n2, w_up, w_up, conv_w, conv_b, w_down)


def _ffn_down_kernel(a0_ref, a1_ref, w_ref, h0_ref, h1_ref, g_ref, o_ref):
    tm = a0_ref.shape[0]
    for s, (a_ref, h_ref) in enumerate(((a0_ref, h0_ref), (a1_ref, h1_ref))):
        h = h_ref[...] + jnp.dot(a_ref[...], w_ref[...], preferred_element_type=F32)
        ms = jnp.mean(h * h, axis=-1, keepdims=True)
        o_ref[s * tm:(s + 1) * tm, :] = (h * lax.rsqrt(ms + EPS) * g_ref[...]).astype(o_ref.dtype)


def _ffn_down(act, w_down, h1, g_final, bsz, lp):
    dff, d = w_down.shape
    tm = ROW_ALIGN
    nt_in = lp // tm
    nt_out = (nt_in - 1) // 2
    assert (nt_in - 1) % 2 == 0

    def rows(s):
        return lambda b, t: (b * nt_in + 1 + 2 * t + s, 0)

    return pl.pallas_call(
        _ffn_down_kernel,
        grid=(bsz, nt_out),
        in_specs=[
            pl.BlockSpec((tm, dff), rows(0)),
            pl.BlockSpec((tm, dff), rows(1)),
            _resident((dff, d), lambda b, t: (0, 0)),
            pl.BlockSpec((tm, d), rows(0)),
            pl.BlockSpec((tm, d), rows(1)),
            _resident((1, d), lambda b, t: (0, 0)),
        ],
        out_specs=pl.BlockSpec((2 * tm, d), lambda b, t: (b * nt_out + t, 0)),
        out_shape=jax.ShapeDtypeStruct((bsz * nt_out * 2 * tm, d), F32),
        compiler_params=_params(("parallel", "parallel")),
        name="ffn_down",
    )(act, act, w_down, h1, h1, g_final)


def _layer(head, x2, bsz, lp, g_mix, w_in, b_f, lam_re, lam_im, log_dt, b_re, b_im, c_re, c_im, d_skip,
           w_glu, w_attn_o, w_out, g_ffn, w_up, conv_w, conv_b, w_down):
    d = x2.shape[1]
    ds = d_skip.shape[0]
    nt = lp // ROW_ALIGN
    o_q, o_k, o_v, o_f, o_u, o_ga, o_gb = (0, D_ATTN, 2 * D_ATTN, 3 * D_ATTN, 3 * D_ATTN + HEADS,
                                            3 * D_ATTN + HEADS + ds, 3 * D_ATTN + HEADS + ds + d)
    w_main, w_f = _win_pack(w_in, o_f, o_u, HEAD_DIM ** -0.5 * LOG2E)
    bf = jnp.pad(b_f.astype(F32), (0, 128 - HEADS)).reshape(1, 128)

    z, u, qb, kb = _inproj(head, x2, g_mix.reshape(1, d), w_main, w_f, bf, bsz * lp, nt, 3 * D_ATTN, ds)
    attn = _flash(z, qb, kb, bsz, lp)
    y, (w_glu_bf, w_ao_bf, w_out_bf) = _ssm(u, lam_re, lam_im, log_dt, b_re, b_im, c_re, c_im, bsz, lp,
                                            (w_glu.astype(F32), w_attn_o.astype(F32), w_out.astype(F32)))

    h1, n2 = _mix(y, z, attn, head, x2, nt, d_skip.reshape(1, ds).astype(F32), g_ffn.reshape(1, d).astype(F32),
                  w_glu_bf, w_ao_bf, w_out_bf)
    act, w_down_bf = _ffn_up(n2, w_up.astype(F32), conv_w.astype(F32), conv_b.reshape(1, -1).astype(F32),
                             w_down.astype(F32))
    return act, h1, w_down_bf


def kernel(x, meta, g_mix, w_in, b_f, lam_re, lam_im, log_dt, b_re, b_im, c_re, c_im, d_skip,
           w_glu, w_attn_o, w_out, g_ffn, w_up, conv_w, conv_b, w_down, g_final):
    bsz, seq, d = x.shape
    depth = g_mix.shape[0]
    assert depth == 1 and meta.shape[0] == N_META and seq % ROW_ALIGN == 0
    lp = seq + ROW_ALIGN
    assert (bsz * lp) % 512 == 0
    head = jnp.concatenate([jnp.zeros((PAD, d), x.dtype), meta.astype(x.dtype)], axis=0)
    act, h1, w_down_bf = _layer(head, x.reshape(bsz * seq, d), bsz, lp, g_mix[0], w_in[0], b_f[0], lam_re[0], lam_im[0], log_dt[0],
                     b_re[0], b_im[0], c_re[0], c_im[0], d_skip[0], w_glu[0], w_attn_o[0], w_out[0],
                     g_ffn[0], w_up[0], conv_w[0], conv_b[0], w_down[0])
    out = _ffn_down(act, w_down_bf, h1, g_final.reshape(1, d).astype(F32), bsz, lp)
    return out.reshape(bsz, seq, d)
```
